```python
import functools
import jax
import jax.numpy as jnp
from jax import lax
import numpy as np

D_MODEL = 2048
BATCH = 8
SEQ = 4096
DEPTH = 4

CTX_LEN = 256
GRID_W = 64
ROPE_BASE = 10000.0
NORM_EPS = 1e-6
NEG_INF = -1e30
QBLOCK = 128

MLA_HEADS = 4
MLA_NOPE = 128
MLA_ROPE = 64
MLA_V = 128
MLA_Q_LORA = 512
MLA_KV_LORA = 256
MLA_SCALE = (MLA_NOPE + MLA_ROPE) ** -0.5
POOL_WINDOWS = (2, 4, 8, 16)
POOL_GROUP = 128
POOL_WIDTH = POOL_GROUP * len(POOL_WINDOWS)
SWA_HEADS = 8
SWA_KV_HEADS = 2
SWA_HEAD_DIM = 64
SWA_WINDOW = 128
SWA_BLOCK = 128
NA_HEADS = 8
NA_HEAD_DIM = 64
NA_KH = 8
NA_KW = 16
NA_QC = 16
NA_KC = NA_QC + NA_KW
FFN_DIM = 5632

A_COLS = MLA_Q_LORA + MLA_KV_LORA + MLA_ROPE
B_COLS = POOL_WIDTH
C_COLS = (SWA_HEADS + 2 * SWA_KV_HEADS) * SWA_HEAD_DIM
D_COLS = 3 * NA_HEADS * NA_HEAD_DIM
IN_COLS = A_COLS + B_COLS + C_COLS + D_COLS
IN_SPLITS = (A_COLS, A_COLS + B_COLS, A_COLS + B_COLS + C_COLS)
MIX_OUT = MLA_HEADS * MLA_V + POOL_WIDTH + SWA_HEADS * SWA_HEAD_DIM + NA_HEADS * NA_HEAD_DIM

kernel_name = 'hybrid_pargroup_dit_block'

F32 = jnp.float32


def rms_norm(x, g):
    xf = x.astype(F32)
    y = xf * lax.rsqrt(jnp.mean(xf * xf, axis=-1, keepdims=True) + NORM_EPS)
    return (y * g.astype(F32)).astype(x.dtype)


def adaln(cond, w_mod, b_mod):
    return jnp.split(jax.nn.silu(cond) @ w_mod + b_mod, 6, axis=-1)


def modulate(h, shift, scale):
    return h * (1.0 + scale) + shift


def axial_angles(n, d_rot):
    t = jnp.arange(n)
    row = (t // GRID_W).astype(F32)
    col = (t % GRID_W).astype(F32)
    d_axis = d_rot // 2
    inv_freq = ROPE_BASE ** (-jnp.arange(0, d_axis, 2, dtype=F32) / d_axis)
    return (row[:, None] * inv_freq, col[:, None] * inv_freq)


def rope_segment(x, ang):
    cos = jnp.cos(ang)[:, None, :].astype(x.dtype)
    sin = jnp.sin(ang)[:, None, :].astype(x.dtype)
    x1, x2 = jnp.split(x, 2, axis=-1)
    return jnp.concatenate([x1 * cos - x2 * sin, x2 * cos + x1 * sin], axis=-1)


def axial_rope(x, ang):
    half = x.shape[-1] // 2
    return jnp.concatenate([rope_segment(x[..., :half], ang[0]), rope_segment(x[..., half:], ang[1])], axis=-1)


def map_query_blocks(fn, qs):
    b, n = qs[0].shape[:2]
    nb = n // QBLOCK
    blocks = tuple(jnp.moveaxis(q.reshape(b, nb, QBLOCK, *q.shape[2:]), 1, 0) for q in qs)
    out = lax.map(lambda qb: fn(*qb), blocks)
    return jnp.moveaxis(out, 0, 1).reshape(b, n, *out.shape[3:])


def mla_project(p, q_a_norm, w_qb, kv_a_norm, w_kvb, q_nope_norm, q_rope_norm, k_nope_norm, k_rope_norm, ang):
    b, n, _ = p.shape
    cq, ckv, kr = jnp.split(p, [MLA_Q_LORA, MLA_Q_LORA + MLA_KV_LORA], axis=-1)
    q = (rms_norm(cq, q_a_norm) @ w_qb).reshape(b, n, MLA_HEADS, MLA_NOPE + MLA_ROPE)
    kv = (rms_norm(ckv, kv_a_norm) @ w_kvb).reshape(b, n, MLA_HEADS, MLA_NOPE + MLA_V)
    q_nope = rms_norm(q[..., :MLA_NOPE], q_nope_norm)
    q_rope = rms_norm(q[..., MLA_NOPE:], q_rope_norm)
    k_nope = rms_norm(kv[..., :MLA_NOPE], k_nope_norm)
    v = kv[..., MLA_NOPE:]
    k_rope = rms_norm(kr, k_rope_norm)[:, :, None, :]
    if ang is not None:
        q_rope = axial_rope(q_rope, ang)
        k_rope = axial_rope(k_rope, ang)
    return q_nope, q_rope, k_nope, k_rope[:, :, 0, :], v


def mla_attend(q_nope, q_rope, k_nope, k_rope, v):
    s = (jnp.einsum('bqhd,bkhd->bhqk', q_nope, k_nope, preferred_element_type=F32)
         + jnp.einsum('bqhr,bkr->bhqk', q_rope, k_rope, preferred_element_type=F32))
    p = jax.nn.softmax(s * MLA_SCALE, axis=-1).astype(v.dtype)
    return jnp.einsum('bhqk,bkhd->bqhd', p, v)


def pool_mixer(u, w_pool, scale):
    b, n, _ = u.shape
    uf = u.astype(F32)
    csum = jnp.pad(jnp.cumsum(uf, axis=1), ((0, 0), (1, 0), (0, 0)))
    t = jnp.arange(n)
    diffs = []
    for g, w in enumerate(POOL_WINDOWS):
        sl = slice(g * POOL_GROUP, (g + 1) * POOL_GROUP)
        lo = jnp.clip(t - w // 2, 0, n)
        hi = jnp.clip(t + w // 2, 0, n)
        cs = csum[..., sl]
        mean = (cs[:, hi] - cs[:, lo]) / (hi - lo).astype(F32)[None, :, None]
        diffs.append(mean - uf[..., sl])
    d = jnp.stack(diffs, axis=2).astype(u.dtype)
    y = jnp.einsum('bngc,gcd->bngd', d, w_pool).reshape(b, n, POOL_WIDTH)
    return y * scale


def swa_project(p, q_norm, k_norm, ang):
    b, n, _ = p.shape
    q, k, v = jnp.split(p, [SWA_HEADS * SWA_HEAD_DIM, (SWA_HEADS + SWA_KV_HEADS) * SWA_HEAD_DIM], axis=-1)
    q = rms_norm(q.reshape(b, n, SWA_HEADS, SWA_HEAD_DIM), q_norm)
    k = rms_norm(k.reshape(b, n, SWA_KV_HEADS, SWA_HEAD_DIM), k_norm)
    v = v.reshape(b, n, SWA_KV_HEADS, SWA_HEAD_DIM)
    if ang is not None:
        q = axial_rope(q, ang)
        k = axial_rope(k, ang)
    return q, k, v


def swa_latent(q, k, v, k_ctx, v_ctx, sink):
    b, n, hq, d = q.shape
    hkv = k.shape[2]
    grp = hq // hkv
    blk = SWA_BLOCK
    nb = n // blk
    scale = d ** -0.5
    qg = q.reshape(b, nb, blk, hkv, grp, d)

    def band(t):
        tp = jnp.pad(t, ((0, 0), (blk, blk), (0, 0), (0, 0))).reshape(b, nb + 2, blk, hkv, d)
        return jnp.concatenate([tp[:, :-2], tp[:, 1:-1], tp[:, 2:]], axis=2)

    k_band, v_band = band(k), band(v)
    s_loc = jnp.einsum('bnqhgd,bnkhd->bnhgqk', qg, k_band, preferred_element_type=F32) * scale
    qpos = jnp.arange(nb)[:, None] * blk + jnp.arange(blk)[None, :]
    kpos = (jnp.arange(nb)[:, None] - 1) * blk + jnp.arange(3 * blk)[None, :]
    valid = ((jnp.abs(kpos[:, None, :] - qpos[:, :, None]) <= SWA_WINDOW)
             & (kpos[:, None, :] >= 0) & (kpos[:, None, :] < n))
    s_loc = jnp.where(valid[None, :, None, None], s_loc, NEG_INF)
    s_ctx = jnp.einsum('bnqhgd,bkhd->bnhgqk', qg, k_ctx, preferred_element_type=F32) * scale
    s_sink = jnp.broadcast_to(sink.astype(F32).reshape(hkv, grp)[:, :, None, None], s_loc.shape[:-1] + (1,))
    p = jax.nn.softmax(jnp.concatenate([s_loc, s_ctx, s_sink], axis=-1), axis=-1).astype(v.dtype)
    nk = 3 * blk
    n_ctx = k_ctx.shape[1]
    o = (jnp.einsum('bnhgqk,bnkhd->bnqhgd', p[..., :nk], v_band)
         + jnp.einsum('bnhgqk,bkhd->bnqhgd', p[..., nk:nk + n_ctx], v_ctx))
    return o.reshape(b, n, hq * d)


def ctx_attention(q, k, v, sink):
    b, nq, hq, d = q.shape
    hkv = k.shape[2]
    grp = hq // hkv
    nk = k.shape[1]
    qg = q.reshape(b, nq, hkv, grp, d)
    s = jnp.einsum('bqhgd,bkhd->bhgqk', qg, k, preferred_element_type=F32) * (d ** -0.5)
    if sink is not None:
        s_sink = jnp.broadcast_to(sink.astype(F32).reshape(hkv, grp)[None, :, :, None, None], s.shape[:-1] + (1,))
        s = jnp.concatenate([s, s_sink], axis=-1)
    p = jax.nn.softmax(s, axis=-1).astype(v.dtype)[..., :nk]
    o = jnp.einsum('bhgqk,bkhd->bqhgd', p, v)
    return o.reshape(b, nq, hq * d)


def na_project(p, q_norm, k_norm):
    b, n, _ = p.shape
    q, k, v = jnp.split(p, 3, axis=-1)
    q = rms_norm(q.reshape(b, n, NA_HEADS, NA_HEAD_DIM), q_norm)
    k = rms_norm(k.reshape(b, n, NA_HEADS, NA_HEAD_DIM), k_norm)
    v = v.reshape(b, n, NA_HEADS, NA_HEAD_DIM)
    return q, k, v


def na_latent(q, k, v, k_ctx, v_ctx, rpb):
    b, n, h, d = q.shape
    rows = n // GRID_W
    kh = min(NA_KH, rows)
    ncb = GRID_W // NA_QC
    nk = kh * NA_KC
    r = jnp.arange(rows)
    row_idx = jnp.clip(r - kh // 2, 0, rows - kh)[:, None] + jnp.arange(kh)[None, :]
    cb = jnp.arange(ncb)
    col_idx = jnp.clip(cb * NA_QC - NA_KW // 2, 0, GRID_W - NA_KC)[:, None] + jnp.arange(NA_KC)[None, :]
    q_col = cb[:, None] * NA_QC + jnp.arange(NA_QC)[None, :]
    q_col0 = jnp.clip(q_col - NA_KW // 2, 0, GRID_W - NA_KW)
    col_ok = (col_idx[:, None, :] >= q_col0[:, :, None]) & (col_idx[:, None, :] < q_col0[:, :, None] + NA_KW)
    mask = jnp.broadcast_to(col_ok[:, :, None, :], (ncb, NA_QC, kh, NA_KC)).reshape(ncb, 1, NA_QC, nk)

    def gather(t):
        grid = t.reshape(b, rows, GRID_W, h, d)
        g = grid[:, row_idx[:, None, :, None], col_idx[None, :, None, :]]
        return g.reshape(b, rows, ncb, nk, h, d)

    k_nb, v_nb = gather(k), gather(v)
    q_blk = q.reshape(b, rows, ncb, NA_QC, h, d)
    scale = d ** -0.5
    dr = row_idx - r[:, None] + (NA_KH - 1)
    dc = jnp.clip(col_idx[:, None, :] - q_col[:, :, None], 1 - NA_KW, NA_KW - 1) + (NA_KW - 1)
    bias = rpb.astype(F32)[:, dr[:, None, None, :, None], dc[None, :, :, None, :]]
    bias = jnp.moveaxis(bias, 0, 2).reshape(rows, ncb, h, NA_QC, nk)
    s_loc = jnp.einsum('brcqhd,brckhd->brchqk', q_blk, k_nb, preferred_element_type=F32) * scale + bias
    s_loc = jnp.where(mask[None, None], s_loc, NEG_INF)
    s_ctx = jnp.einsum('brcqhd,bkhd->brchqk', q_blk, k_ctx, preferred_element_type=F32) * scale
    p = jax.nn.softmax(jnp.concatenate([s_loc, s_ctx], axis=-1), axis=-1).astype(v.dtype)
    o = (jnp.einsum('brchqk,brckhd->brcqhd', p[..., :nk], v_nb)
         + jnp.einsum('brchqk,bkhd->brcqhd', p[..., nk:], v_ctx))
    return o.reshape(b, n, h * d)


def conv_ffn(h, w_up, conv_w, conv_b, w_down):
    a = h @ w_up
    ap = jnp.pad(a, ((0, 0), (1, 1), (0, 0)))
    a = ap[:, :-2] * conv_w[0] + ap[:, 1:-1] * conv_w[1] + ap[:, 2:] * conv_w[2] + conv_b
    gate, val = jnp.split(a, 2, axis=-1)
    return (jax.nn.silu(gate) * val) @ w_down


def _fwd_setup_inputs(seed: int = 0) -> dict:
    key = jax.random.key(seed)
    ks = jax.random.split(key, 32)
    L = DEPTH

    def nrm(k, shape, scale):
        return jax.random.normal(k, shape, F32) * scale

    def gain(k, shape):
        return 1.0 + 0.1 * jax.random.normal(k, shape, F32)

    return {
        'x': nrm(ks[0], (BATCH, SEQ, D_MODEL), 1.0),
        'c': nrm(ks[1], (BATCH, D_MODEL), 1.0),
        'ctx': nrm(ks[2], (BATCH, CTX_LEN, D_MODEL), 1.0),
        'c_ctx': nrm(ks[3], (D_MODEL,), 1.0),
        'w_mod': nrm(ks[4], (L, D_MODEL, 6 * D_MODEL), 0.5 * D_MODEL ** -0.5),
        'b_mod': nrm(ks[5], (L, 6 * D_MODEL), 0.01),
        'g_mix': gain(ks[6], (L, D_MODEL)),
        'g_ffn': gain(ks[7], (L, D_MODEL)),
        'w_in': nrm(ks[8], (L, D_MODEL, IN_COLS), D_MODEL ** -0.5),
        'w_out': nrm(ks[9], (L, MIX_OUT, D_MODEL), MIX_OUT ** -0.5),
        'mla_q_a_norm': gain(ks[10], (L, MLA_Q_LORA)),
        'mla_w_qb': nrm(ks[11], (L, MLA_Q_LORA, MLA_HEADS * (MLA_NOPE + MLA_ROPE)), MLA_Q_LORA ** -0.5),
        'mla_kv_a_norm': gain(ks[12], (L, MLA_KV_LORA)),
        'mla_w_kvb': nrm(ks[13], (L, MLA_KV_LORA, MLA_HEADS * (MLA_NOPE + MLA_V)), MLA_KV_LORA ** -0.5),
        'mla_q_nope_norm': gain(ks[14], (L, MLA_NOPE)),
        'mla_q_rope_norm': gain(ks[15], (L, MLA_ROPE)),
        'mla_k_nope_norm': gain(ks[16], (L, MLA_NOPE)),
        'mla_k_rope_norm': gain(ks[17], (L, MLA_ROPE)),
        'pool_w': nrm(ks[18], (L, len(POOL_WINDOWS), POOL_GROUP, POOL_GROUP), POOL_GROUP ** -0.5),
        'pool_scale': gain(ks[19], (L, POOL_WIDTH)),
        'swa_q_norm': gain(ks[20], (L, SWA_HEAD_DIM)),
        'swa_k_norm': gain(ks[21], (L, SWA_HEAD_DIM)),
        'swa_sink': nrm(ks[22], (L, SWA_HEADS), 0.5),
        'na_q_norm': gain(ks[23], (L, NA_HEAD_DIM)),
        'na_k_norm': gain(ks[24], (L, NA_HEAD_DIM)),
        'na_rpb': nrm(ks[25], (L, NA_HEADS, 2 * NA_KH - 1, 2 * NA_KW - 1), 0.5),
        'ffn_w_up': nrm(ks[26], (L, D_MODEL, 2 * FFN_DIM), D_MODEL ** -0.5),
        'ffn_conv_w': nrm(ks[27], (L, 3, 2 * FFN_DIM), 3 ** -0.5),
        'ffn_conv_b': nrm(ks[28], (L, 2 * FFN_DIM), 0.02),
        'ffn_w_down': nrm(ks[29], (L, FFN_DIM, D_MODEL), FFN_DIM ** -0.5),
    }


def _fwd_reference(x, c, ctx, c_ctx, w_mod, b_mod, g_mix, g_ffn, w_in, w_out,
              mla_q_a_norm, mla_w_qb, mla_kv_a_norm, mla_w_kvb,
              mla_q_nope_norm, mla_q_rope_norm, mla_k_nope_norm, mla_k_rope_norm,
              pool_w, pool_scale, swa_q_norm, swa_k_norm, swa_sink,
              na_q_norm, na_k_norm, na_rpb,
              ffn_w_up, ffn_conv_w, ffn_conv_b, ffn_w_down):
    b, n, _ = x.shape
    n_ctx = ctx.shape[1]
    ang_mla = axial_angles(n, MLA_ROPE)
    ang_swa = axial_angles(n, SWA_HEAD_DIM)
    cond_x = c[:, None, :]
    for l in range(DEPTH):
        update_ctx = l < DEPTH - 1
        sh_m, sc_m, gt_m, sh_f, sc_f, gt_f = adaln(cond_x, w_mod[l], b_mod[l])
        csh_m, csc_m, cgt_m, csh_f, csc_f, cgt_f = adaln(c_ctx, w_mod[l], b_mod[l])

        px = modulate(rms_norm(x, g_mix[l]), sh_m, sc_m) @ w_in[l]
        pc = modulate(rms_norm(ctx, g_mix[l]), csh_m, csc_m) @ w_in[l]
        mla_x, pool_x, swa_x, na_x = jnp.split(px, IN_SPLITS, axis=-1)
        mla_c, pool_c, swa_c, na_c = jnp.split(pc, IN_SPLITS, axis=-1)

        mla_w = (mla_q_a_norm[l], mla_w_qb[l], mla_kv_a_norm[l], mla_w_kvb[l],
                 mla_q_nope_norm[l], mla_q_rope_norm[l], mla_k_nope_norm[l], mla_k_rope_norm[l])
        aq_n, aq_r, ak_n, ak_r, av = mla_project(mla_x, *mla_w, ang_mla)
        cq_n, cq_r, ck_n, ck_r, cv = mla_project(mla_c, *mla_w, None)
        attend_all = functools.partial(
            mla_attend,
            k_nope=jnp.concatenate([ak_n, ck_n], axis=1),
            k_rope=jnp.concatenate([ak_r, ck_r], axis=1),
            v=jnp.concatenate([av, cv], axis=1))
        out_a = map_query_blocks(attend_all, (aq_n, aq_r)).reshape(b, n, MLA_HEADS * MLA_V)

        out_b = pool_mixer(pool_x, pool_w[l], pool_scale[l])

        sq, sk, sv = swa_project(swa_x, swa_q_norm[l], swa_k_norm[l], ang_swa)
        csq, csk, csv = swa_project(swa_c, swa_q_norm[l], swa_k_norm[l], None)
        out_c = swa_latent(sq, sk, sv, csk, csv, swa_sink[l])

        nq, nkk, nv = na_project(na_x, na_q_norm[l], na_k_norm[l])
        cnq, cnk, cnv = na_project(na_c, na_q_norm[l], na_k_norm[l])
        out_d = na_latent(nq, nkk, nv, cnk, cnv, na_rpb[l])

        if update_ctx:
            mix_c = jnp.concatenate([
                mla_attend(cq_n, cq_r, ck_n, ck_r, cv).reshape(b, n_ctx, MLA_HEADS * MLA_V),
                pool_mixer(pool_c, pool_w[l], pool_scale[l]),
                ctx_attention(csq, csk, csv, swa_sink[l]),
                ctx_attention(cnq, cnk, cnv, None),
            ], axis=-1)
            ctx = ctx + cgt_m * (mix_c @ w_out[l])
            ctx = ctx + cgt_f * conv_ffn(modulate(rms_norm(ctx, g_ffn[l]), csh_f, csc_f),
                                         ffn_w_up[l], ffn_conv_w[l], ffn_conv_b[l], ffn_w_down[l])

        mix_x = jnp.concatenate([out_a, out_b, out_c, out_d], axis=-1)
        x = x + gt_m * (mix_x @ w_out[l])
        x = x + gt_f * conv_ffn(modulate(rms_norm(x, g_ffn[l]), sh_f, sc_f),
                                ffn_w_up[l], ffn_conv_w[l], ffn_conv_b[l], ffn_w_down[l])
    return x


import jax as _jax
import jax.numpy as _jnp

TWIN_FORMAT = 'train_step'
FWD_PARAMS = ['x', 'c', 'ctx', 'c_ctx', 'w_mod', 'b_mod', 'g_mix', 'g_ffn', 'w_in', 'w_out', 'mla_q_a_norm', 'mla_w_qb', 'mla_kv_a_norm', 'mla_w_kvb', 'mla_q_nope_norm', 'mla_q_rope_norm', 'mla_k_nope_norm', 'mla_k_rope_norm', 'pool_w', 'pool_scale', 'swa_q_norm', 'swa_k_norm', 'swa_sink', 'na_q_norm', 'na_k_norm', 'na_rpb', 'ffn_w_up', 'ffn_conv_w', 'ffn_conv_b', 'ffn_w_down']
TWIN_WEIGHTS = ['c_ctx', 'w_mod', 'b_mod', 'g_mix', 'g_ffn', 'w_in', 'w_out', 'mla_q_a_norm', 'mla_w_qb', 'mla_kv_a_norm', 'mla_w_kvb', 'mla_q_nope_norm', 'mla_q_rope_norm', 'mla_k_nope_norm', 'mla_k_rope_norm', 'pool_w', 'pool_scale', 'swa_q_norm', 'swa_k_norm', 'swa_sink', 'na_q_norm', 'na_k_norm', 'na_rpb', 'ffn_w_up', 'ffn_conv_w', 'ffn_conv_b', 'ffn_w_down']
TWIN_DIFF_INPUT = 'x'
TWIN_INPUTS = ['x', 'c', 'ctx', 'c_ctx', 'w_mod', 'b_mod', 'g_mix', 'g_ffn', 'w_in', 'w_out', 'mla_q_a_norm', 'mla_w_qb', 'mla_kv_a_norm', 'mla_w_kvb', 'mla_q_nope_norm', 'mla_q_rope_norm', 'mla_k_nope_norm', 'mla_k_rope_norm', 'pool_w', 'pool_scale', 'swa_q_norm', 'swa_k_norm', 'swa_sink', 'na_q_norm', 'na_k_norm', 'na_rpb', 'ffn_w_up', 'ffn_conv_w', 'ffn_conv_b', 'ffn_w_down', 'loss_target', 'm_c_ctx', 'm_w_mod', 'm_b_mod', 'm_g_mix', 'm_g_ffn', 'm_w_in', 'm_w_out', 'm_mla_q_a_norm', 'm_mla_w_qb', 'm_mla_kv_a_norm', 'm_mla_w_kvb', 'm_mla_q_nope_norm', 'm_mla_q_rope_norm', 'm_mla_k_nope_norm', 'm_mla_k_rope_norm', 'm_pool_w', 'm_pool_scale', 'm_swa_q_norm', 'm_swa_k_norm', 'm_swa_sink', 'm_na_q_norm', 'm_na_k_norm', 'm_na_rpb', 'm_ffn_w_up', 'm_ffn_conv_w', 'm_ffn_conv_b', 'm_ffn_w_down', 'v_c_ctx', 'v_w_mod', 'v_b_mod', 'v_g_mix', 'v_g_ffn', 'v_w_in', 'v_w_out', 'v_mla_q_a_norm', 'v_mla_w_qb', 'v_mla_kv_a_norm', 'v_mla_w_kvb', 'v_mla_q_nope_norm', 'v_mla_q_rope_norm', 'v_mla_k_nope_norm', 'v_mla_k_rope_norm', 'v_pool_w', 'v_pool_scale', 'v_swa_q_norm', 'v_swa_k_norm', 'v_swa_sink', 'v_na_q_norm', 'v_na_k_norm', 'v_na_rpb', 'v_ffn_w_up', 'v_ffn_conv_w', 'v_ffn_conv_b', 'v_ffn_w_down']
TWIN_OUTPUTS = ['loss', 'grad_x', 'grad_c_ctx', 'grad_w_mod', 'grad_b_mod', 'grad_g_mix', 'grad_g_ffn', 'grad_w_in', 'grad_w_out', 'grad_mla_q_a_norm', 'grad_mla_w_qb', 'grad_mla_kv_a_norm', 'grad_mla_w_kvb', 'grad_mla_q_nope_norm', 'grad_mla_q_rope_norm', 'grad_mla_k_nope_norm', 'grad_mla_k_rope_norm', 'grad_pool_w', 'grad_pool_scale', 'grad_swa_q_norm', 'grad_swa_k_norm', 'grad_swa_sink', 'grad_na_q_norm', 'grad_na_k_norm', 'grad_na_rpb', 'grad_ffn_w_up', 'grad_ffn_conv_w', 'grad_ffn_conv_b', 'grad_ffn_w_down', 'delta_c_ctx', 'delta_w_mod', 'delta_b_mod', 'delta_g_mix', 'delta_g_ffn', 'delta_w_in', 'delta_w_out', 'delta_mla_q_a_norm', 'delta_mla_w_qb', 'delta_mla_kv_a_norm', 'delta_mla_w_kvb', 'delta_mla_q_nope_norm', 'delta_mla_q_rope_norm', 'delta_mla_k_nope_norm', 'delta_mla_k_rope_norm', 'delta_pool_w', 'delta_pool_scale', 'delta_swa_q_norm', 'delta_swa_k_norm', 'delta_swa_sink', 'delta_na_q_norm', 'delta_na_k_norm', 'delta_na_rpb', 'delta_ffn_w_up', 'delta_ffn_conv_w', 'delta_ffn_conv_b', 'delta_ffn_w_down', 'new_m_c_ctx', 'new_m_w_mod', 'new_m_b_mod', 'new_m_g_mix', 'new_m_g_ffn', 'new_m_w_in', 'new_m_w_out', 'new_m_mla_q_a_norm', 'new_m_mla_w_qb', 'new_m_mla_kv_a_norm', 'new_m_mla_w_kvb', 'new_m_mla_q_nope_norm', 'new_m_mla_q_rope_norm', 'new_m_mla_k_nope_norm', 'new_m_mla_k_rope_norm', 'new_m_pool_w', 'new_m_pool_scale', 'new_m_swa_q_norm', 'new_m_swa_k_norm', 'new_m_swa_sink', 'new_m_na_q_norm', 'new_m_na_k_norm', 'new_m_na_rpb', 'new_m_ffn_w_up', 'new_m_ffn_conv_w', 'new_m_ffn_conv_b', 'new_m_ffn_w_down', 'new_v_c_ctx', 'new_v_w_mod', 'new_v_b_mod', 'new_v_g_mix', 'new_v_g_ffn', 'new_v_w_in', 'new_v_w_out', 'new_v_mla_q_a_norm', 'new_v_mla_w_qb', 'new_v_mla_kv_a_norm', 'new_v_mla_w_kvb', 'new_v_mla_q_nope_norm', 'new_v_mla_q_rope_norm', 'new_v_mla_k_nope_norm', 'new_v_mla_k_rope_norm', 'new_v_pool_w', 'new_v_pool_scale', 'new_v_swa_q_norm', 'new_v_swa_k_norm', 'new_v_swa_sink', 'new_v_na_q_norm', 'new_v_na_k_norm', 'new_v_na_rpb', 'new_v_ffn_w_up', 'new_v_ffn_conv_w', 'new_v_ffn_conv_b', 'new_v_ffn_w_down']
TWIN_LEAF_KINDS = {'loss': 'loss', 'grad_x': 'grad_x', 'grad_c_ctx': 'grad_w', 'grad_w_mod': 'grad_w', 'grad_b_mod': 'grad_w', 'grad_g_mix': 'grad_w', 'grad_g_ffn': 'grad_w', 'grad_w_in': 'grad_w', 'grad_w_out': 'grad_w', 'grad_mla_q_a_norm': 'grad_w', 'grad_mla_w_qb': 'grad_w', 'grad_mla_kv_a_norm': 'grad_w', 'grad_mla_w_kvb': 'grad_w', 'grad_mla_q_nope_norm': 'grad_w', 'grad_mla_q_rope_norm': 'grad_w', 'grad_mla_k_nope_norm': 'grad_w', 'grad_mla_k_rope_norm': 'grad_w', 'grad_pool_w': 'grad_w', 'grad_pool_scale': 'grad_w', 'grad_swa_q_norm': 'grad_w', 'grad_swa_k_norm': 'grad_w', 'grad_swa_sink': 'grad_w', 'grad_na_q_norm': 'grad_w', 'grad_na_k_norm': 'grad_w', 'grad_na_rpb': 'grad_w', 'grad_ffn_w_up': 'grad_w', 'grad_ffn_conv_w': 'grad_w', 'grad_ffn_conv_b': 'grad_w', 'grad_ffn_w_down': 'grad_w', 'delta_c_ctx': 'delta_w', 'delta_w_mod': 'delta_w', 'delta_b_mod': 'delta_w', 'delta_g_mix': 'delta_w', 'delta_g_ffn': 'delta_w', 'delta_w_in': 'delta_w', 'delta_w_out': 'delta_w', 'delta_mla_q_a_norm': 'delta_w', 'delta_mla_w_qb': 'delta_w', 'delta_mla_kv_a_norm': 'delta_w', 'delta_mla_w_kvb': 'delta_w', 'delta_mla_q_nope_norm': 'delta_w', 'delta_mla_q_rope_norm': 'delta_w', 'delta_mla_k_nope_norm': 'delta_w', 'delta_mla_k_rope_norm': 'delta_w', 'delta_pool_w': 'delta_w', 'delta_pool_scale': 'delta_w', 'delta_swa_q_norm': 'delta_w', 'delta_swa_k_norm': 'delta_w', 'delta_swa_sink': 'delta_w', 'delta_na_q_norm': 'delta_w', 'delta_na_k_norm': 'delta_w', 'delta_na_rpb': 'delta_w', 'delta_ffn_w_up': 'delta_w', 'delta_ffn_conv_w': 'delta_w', 'delta_ffn_conv_b': 'delta_w', 'delta_ffn_w_down': 'delta_w', 'new_m_c_ctx': 'new_m', 'new_m_w_mod': 'new_m', 'new_m_b_mod': 'new_m', 'new_m_g_mix': 'new_m', 'new_m_g_ffn': 'new_m', 'new_m_w_in': 'new_m', 'new_m_w_out': 'new_m', 'new_m_mla_q_a_norm': 'new_m', 'new_m_mla_w_qb': 'new_m', 'new_m_mla_kv_a_norm': 'new_m', 'new_m_mla_w_kvb': 'new_m', 'new_m_mla_q_nope_norm': 'new_m', 'new_m_mla_q_rope_norm': 'new_m', 'new_m_mla_k_nope_norm': 'new_m', 'new_m_mla_k_rope_norm': 'new_m', 'new_m_pool_w': 'new_m', 'new_m_pool_scale': 'new_m', 'new_m_swa_q_norm': 'new_m', 'new_m_swa_k_norm': 'new_m', 'new_m_swa_sink': 'new_m', 'new_m_na_q_norm': 'new_m', 'new_m_na_k_norm': 'new_m', 'new_m_na_rpb': 'new_m', 'new_m_ffn_w_up': 'new_m', 'new_m_ffn_conv_w': 'new_m', 'new_m_ffn_conv_b': 'new_m', 'new_m_ffn_w_down': 'new_m', 'new_v_c_ctx': 'new_v', 'new_v_w_mod': 'new_v', 'new_v_b_mod': 'new_v', 'new_v_g_mix': 'new_v', 'new_v_g_ffn': 'new_v', 'new_v_w_in': 'new_v', 'new_v_w_out': 'new_v', 'new_v_mla_q_a_norm': 'new_v', 'new_v_mla_w_qb': 'new_v', 'new_v_mla_kv_a_norm': 'new_v', 'new_v_mla_w_kvb': 'new_v', 'new_v_mla_q_nope_norm': 'new_v', 'new_v_mla_q_rope_norm': 'new_v', 'new_v_mla_k_nope_norm': 'new_v', 'new_v_mla_k_rope_norm': 'new_v', 'new_v_pool_w': 'new_v', 'new_v_pool_scale': 'new_v', 'new_v_swa_q_norm': 'new_v', 'new_v_swa_k_norm': 'new_v', 'new_v_swa_sink': 'new_v', 'new_v_na_q_norm': 'new_v', 'new_v_na_k_norm': 'new_v', 'new_v_na_rpb': 'new_v', 'new_v_ffn_w_up': 'new_v', 'new_v_ffn_conv_w': 'new_v', 'new_v_ffn_conv_b': 'new_v', 'new_v_ffn_w_down': 'new_v'}


def _forward(args):
    return _fwd_reference(*[args[k] for k in FWD_PARAMS])


def _output_shape():
    def fwd():
        inp = _fwd_setup_inputs(0)
        return _fwd_reference(*[inp[k] for k in FWD_PARAMS])
    out = _jax.eval_shape(fwd)
    return out.shape, out.dtype

N_MICROBATCH = 1
ADAM_LR = 0.001
ADAM_B1 = 0.9
ADAM_B2 = 0.999
ADAM_EPS = 1e-08
ADAM_WD = 0.01
ADAM_STEP = 10
PER_EXAMPLE_BATCH_AXIS = {'x': 0, 'c': 0, 'ctx': 0, 'loss_target': 0}
SHARED_INPUTS = []
_WEIGHT_DTYPES = {'c_ctx': _jnp.float32, 'w_mod': _jnp.float32, 'b_mod': _jnp.float32, 'g_mix': _jnp.float32, 'g_ffn': _jnp.float32, 'w_in': _jnp.float32, 'w_out': _jnp.float32, 'mla_q_a_norm': _jnp.float32, 'mla_w_qb': _jnp.float32, 'mla_kv_a_norm': _jnp.float32, 'mla_w_kvb': _jnp.float32, 'mla_q_nope_norm': _jnp.float32, 'mla_q_rope_norm': _jnp.float32, 'mla_k_nope_norm': _jnp.float32, 'mla_k_rope_norm': _jnp.float32, 'pool_w': _jnp.float32, 'pool_scale': _jnp.float32, 'swa_q_norm': _jnp.float32, 'swa_k_norm': _jnp.float32, 'swa_sink': _jnp.float32, 'na_q_norm': _jnp.float32, 'na_k_norm': _jnp.float32, 'na_rpb': _jnp.float32, 'ffn_w_up': _jnp.float32, 'ffn_conv_w': _jnp.float32, 'ffn_conv_b': _jnp.float32, 'ffn_w_down': _jnp.float32}
MOMENT_SCALE = {'c_ctx': 1.745626e-01, 'w_mod': 3.335011e-01, 'b_mod': 8.995278e-01, 'g_mix': 2.997351e-01, 'g_ffn': 1.600779e+00, 'w_in': 6.539519e-02, 'w_out': 6.915782e-02, 'mla_q_a_norm': 6.111653e-03, 'mla_w_qb': 5.282066e-03, 'mla_kv_a_norm': 2.299244e-01, 'mla_w_kvb': 5.850505e-02, 'mla_q_nope_norm': 1.794397e-02, 'mla_q_rope_norm': 8.204224e-03, 'mla_k_nope_norm': 1.796868e-02, 'mla_k_rope_norm': 8.236867e-03, 'pool_w': 1.131965e-01, 'pool_scale': 1.264862e+00, 'swa_q_norm': 7.270521e-02, 'swa_k_norm': 7.232797e-02, 'swa_sink': 6.386459e-03, 'na_q_norm': 9.705978e-02, 'na_k_norm': 9.589195e-02, 'na_rpb': 1.550092e-03, 'ffn_w_up': 4.540339e-02, 'ffn_conv_w': 2.416812e-01, 'ffn_conv_b': 2.016557e-01, 'ffn_w_down': 4.196733e-02}


def _to_microbatches(a, axis):
    t = _jnp.moveaxis(a, axis, 0)
    t = t.reshape((N_MICROBATCH, t.shape[0] // N_MICROBATCH) + t.shape[1:])
    return _jnp.moveaxis(t, 1, axis + 1)


def setup_inputs(seed: int = 0) -> dict:
    inp = _fwd_setup_inputs(seed)
    key = _jax.random.fold_in(_jax.random.key(seed), 7919)
    shape, _ = _output_shape()
    out = dict(inp)
    out["loss_target"] = _jax.random.normal(_jax.random.fold_in(key, 0), shape, _jnp.float32)
    for i, name in enumerate(TWIN_WEIGHTS):
        w = inp[name].astype(_jnp.float32)
        if MOMENT_SCALE is None:
            s = _jnp.sqrt(_jnp.mean(_jnp.square(w)) + 1e-30)
        else:
            s = MOMENT_SCALE[name]
        km, kv = _jax.random.split(_jax.random.fold_in(key, i + 1))
        out[name] = w
        out["m_" + name] = s * _jax.random.normal(km, w.shape, _jnp.float32)
        out["v_" + name] = (s * s) * _jax.random.uniform(kv, w.shape, _jnp.float32, 0.5, 1.5)
    if N_MICROBATCH > 1:
        for name, axis in PER_EXAMPLE_BATCH_AXIS.items():
            out[name] = _to_microbatches(out[name], axis)
    return {'x': out['x'], 'c': out['c'], 'ctx': out['ctx'], 'c_ctx': out['c_ctx'], 'w_mod': out['w_mod'], 'b_mod': out['b_mod'], 'g_mix': out['g_mix'], 'g_ffn': out['g_ffn'], 'w_in': out['w_in'], 'w_out': out['w_out'], 'mla_q_a_norm': out['mla_q_a_norm'], 'mla_w_qb': out['mla_w_qb'], 'mla_kv_a_norm': out['mla_kv_a_norm'], 'mla_w_kvb': out['mla_w_kvb'], 'mla_q_nope_norm': out['mla_q_nope_norm'], 'mla_q_rope_norm': out['mla_q_rope_norm'], 'mla_k_nope_norm': out['mla_k_nope_norm'], 'mla_k_rope_norm': out['mla_k_rope_norm'], 'pool_w': out['pool_w'], 'pool_scale': out['pool_scale'], 'swa_q_norm': out['swa_q_norm'], 'swa_k_norm': out['swa_k_norm'], 'swa_sink': out['swa_sink'], 'na_q_norm': out['na_q_norm'], 'na_k_norm': out['na_k_norm'], 'na_rpb': out['na_rpb'], 'ffn_w_up': out['ffn_w_up'], 'ffn_conv_w': out['ffn_conv_w'], 'ffn_conv_b': out['ffn_conv_b'], 'ffn_w_down': out['ffn_w_down'], 'loss_target': out['loss_target'], 'm_c_ctx': out['m_c_ctx'], 'm_w_mod': out['m_w_mod'], 'm_b_mod': out['m_b_mod'], 'm_g_mix': out['m_g_mix'], 'm_g_ffn': out['m_g_ffn'], 'm_w_in': out['m_w_in'], 'm_w_out': out['m_w_out'], 'm_mla_q_a_norm': out['m_mla_q_a_norm'], 'm_mla_w_qb': out['m_mla_w_qb'], 'm_mla_kv_a_norm': out['m_mla_kv_a_norm'], 'm_mla_w_kvb': out['m_mla_w_kvb'], 'm_mla_q_nope_norm': out['m_mla_q_nope_norm'], 'm_mla_q_rope_norm': out['m_mla_q_rope_norm'], 'm_mla_k_nope_norm': out['m_mla_k_nope_norm'], 'm_mla_k_rope_norm': out['m_mla_k_rope_norm'], 'm_pool_w': out['m_pool_w'], 'm_pool_scale': out['m_pool_scale'], 'm_swa_q_norm': out['m_swa_q_norm'], 'm_swa_k_norm': out['m_swa_k_norm'], 'm_swa_sink': out['m_swa_sink'], 'm_na_q_norm': out['m_na_q_norm'], 'm_na_k_norm': out['m_na_k_norm'], 'm_na_rpb': out['m_na_rpb'], 'm_ffn_w_up': out['m_ffn_w_up'], 'm_ffn_conv_w': out['m_ffn_conv_w'], 'm_ffn_conv_b': out['m_ffn_conv_b'], 'm_ffn_w_down': out['m_ffn_w_down'], 'v_c_ctx': out['v_c_ctx'], 'v_w_mod': out['v_w_mod'], 'v_b_mod': out['v_b_mod'], 'v_g_mix': out['v_g_mix'], 'v_g_ffn': out['v_g_ffn'], 'v_w_in': out['v_w_in'], 'v_w_out': out['v_w_out'], 'v_mla_q_a_norm': out['v_mla_q_a_norm'], 'v_mla_w_qb': out['v_mla_w_qb'], 'v_mla_kv_a_norm': out['v_mla_kv_a_norm'], 'v_mla_w_kvb': out['v_mla_w_kvb'], 'v_mla_q_nope_norm': out['v_mla_q_nope_norm'], 'v_mla_q_rope_norm': out['v_mla_q_rope_norm'], 'v_mla_k_nope_norm': out['v_mla_k_nope_norm'], 'v_mla_k_rope_norm': out['v_mla_k_rope_norm'], 'v_pool_w': out['v_pool_w'], 'v_pool_scale': out['v_pool_scale'], 'v_swa_q_norm': out['v_swa_q_norm'], 'v_swa_k_norm': out['v_swa_k_norm'], 'v_swa_sink': out['v_swa_sink'], 'v_na_q_norm': out['v_na_q_norm'], 'v_na_k_norm': out['v_na_k_norm'], 'v_na_rpb': out['v_na_rpb'], 'v_ffn_w_up': out['v_ffn_w_up'], 'v_ffn_conv_w': out['v_ffn_conv_w'], 'v_ffn_conv_b': out['v_ffn_conv_b'], 'v_ffn_w_down': out['v_ffn_w_down']}


def _loss(weights, diff, rest, loss_target):
    with _jax.named_scope("forward"):
        args = {**rest, TWIN_DIFF_INPUT: diff, **{k: w.astype(_WEIGHT_DTYPES[k]) for k, w in weights.items()}}
        y = _forward(args)
    with _jax.named_scope("loss_head"):
        err = _jnp.square(y.astype(_jnp.float32) - loss_target)
        return 0.5 * _jnp.sum(_jnp.mean(err, axis=-1)) if err.ndim else 0.5 * err


def _adamw(w, g, m, v):
    m = ADAM_B1 * m + (1.0 - ADAM_B1) * g
    v = ADAM_B2 * v + (1.0 - ADAM_B2) * _jnp.square(g)
    m_hat = m / (1.0 - ADAM_B1 ** ADAM_STEP)
    v_hat = v / (1.0 - ADAM_B2 ** ADAM_STEP)
    delta = -ADAM_LR * (m_hat / (_jnp.sqrt(v_hat) + ADAM_EPS) + ADAM_WD * w)
    return delta, m, v


def reference(x, c, ctx, c_ctx, w_mod, b_mod, g_mix, g_ffn, w_in, w_out, mla_q_a_norm, mla_w_qb, mla_kv_a_norm, mla_w_kvb, mla_q_nope_norm, mla_q_rope_norm, mla_k_nope_norm, mla_k_rope_norm, pool_w, pool_scale, swa_q_norm, swa_k_norm, swa_sink, na_q_norm, na_k_norm, na_rpb, ffn_w_up, ffn_conv_w, ffn_conv_b, ffn_w_down, loss_target, m_c_ctx, m_w_mod, m_b_mod, m_g_mix, m_g_ffn, m_w_in, m_w_out, m_mla_q_a_norm, m_mla_w_qb, m_mla_kv_a_norm, m_mla_w_kvb, m_mla_q_nope_norm, m_mla_q_rope_norm, m_mla_k_nope_norm, m_mla_k_rope_norm, m_pool_w, m_pool_scale, m_swa_q_norm, m_swa_k_norm, m_swa_sink, m_na_q_norm, m_na_k_norm, m_na_rpb, m_ffn_w_up, m_ffn_conv_w, m_ffn_conv_b, m_ffn_w_down, v_c_ctx, v_w_mod, v_b_mod, v_g_mix, v_g_ffn, v_w_in, v_w_out, v_mla_q_a_norm, v_mla_w_qb, v_mla_kv_a_norm, v_mla_w_kvb, v_mla_q_nope_norm, v_mla_q_rope_norm, v_mla_k_nope_norm, v_mla_k_rope_norm, v_pool_w, v_pool_scale, v_swa_q_norm, v_swa_k_norm, v_swa_sink, v_na_q_norm, v_na_k_norm, v_na_rpb, v_ffn_w_up, v_ffn_conv_w, v_ffn_conv_b, v_ffn_w_down):
    given = dict(x=x, c=c, ctx=ctx, c_ctx=c_ctx, w_mod=w_mod, b_mod=b_mod, g_mix=g_mix, g_ffn=g_ffn, w_in=w_in, w_out=w_out, mla_q_a_norm=mla_q_a_norm, mla_w_qb=mla_w_qb, mla_kv_a_norm=mla_kv_a_norm, mla_w_kvb=mla_w_kvb, mla_q_nope_norm=mla_q_nope_norm, mla_q_rope_norm=mla_q_rope_norm, mla_k_nope_norm=mla_k_nope_norm, mla_k_rope_norm=mla_k_rope_norm, pool_w=pool_w, pool_scale=pool_scale, swa_q_norm=swa_q_norm, swa_k_norm=swa_k_norm, swa_sink=swa_sink, na_q_norm=na_q_norm, na_k_norm=na_k_norm, na_rpb=na_rpb, ffn_w_up=ffn_w_up, ffn_conv_w=ffn_conv_w, ffn_conv_b=ffn_conv_b, ffn_w_down=ffn_w_down, loss_target=loss_target, m_c_ctx=m_c_ctx, m_w_mod=m_w_mod, m_b_mod=m_b_mod, m_g_mix=m_g_mix, m_g_ffn=m_g_ffn, m_w_in=m_w_in, m_w_out=m_w_out, m_mla_q_a_norm=m_mla_q_a_norm, m_mla_w_qb=m_mla_w_qb, m_mla_kv_a_norm=m_mla_kv_a_norm, m_mla_w_kvb=m_mla_w_kvb, m_mla_q_nope_norm=m_mla_q_nope_norm, m_mla_q_rope_norm=m_mla_q_rope_norm, m_mla_k_nope_norm=m_mla_k_nope_norm, m_mla_k_rope_norm=m_mla_k_rope_norm, m_pool_w=m_pool_w, m_pool_scale=m_pool_scale, m_swa_q_norm=m_swa_q_norm, m_swa_k_norm=m_swa_k_norm, m_swa_sink=m_swa_sink, m_na_q_norm=m_na_q_norm, m_na_k_norm=m_na_k_norm, m_na_rpb=m_na_rpb, m_ffn_w_up=m_ffn_w_up, m_ffn_conv_w=m_ffn_conv_w, m_ffn_conv_b=m_ffn_conv_b, m_ffn_w_down=m_ffn_w_down, v_c_ctx=v_c_ctx, v_w_mod=v_w_mod, v_b_mod=v_b_mod, v_g_mix=v_g_mix, v_g_ffn=v_g_ffn, v_w_in=v_w_in, v_w_out=v_w_out, v_mla_q_a_norm=v_mla_q_a_norm, v_mla_w_qb=v_mla_w_qb, v_mla_kv_a_norm=v_mla_kv_a_norm, v_mla_w_kvb=v_mla_w_kvb, v_mla_q_nope_norm=v_mla_q_nope_norm, v_mla_q_rope_norm=v_mla_q_rope_norm, v_mla_k_nope_norm=v_mla_k_nope_norm, v_mla_k_rope_norm=v_mla_k_rope_norm, v_pool_w=v_pool_w, v_pool_scale=v_pool_scale, v_swa_q_norm=v_swa_q_norm, v_swa_k_norm=v_swa_k_norm, v_swa_sink=v_swa_sink, v_na_q_norm=v_na_q_norm, v_na_k_norm=v_na_k_norm, v_na_rpb=v_na_rpb, v_ffn_w_up=v_ffn_w_up, v_ffn_conv_w=v_ffn_conv_w, v_ffn_conv_b=v_ffn_conv_b, v_ffn_w_down=v_ffn_w_down)
    weights = {n: given[n] for n in TWIN_WEIGHTS}
    shared = {n: given[n] for n in SHARED_INPUTS}
    per_example = {n: given[n] for n in ['x', 'c', 'ctx']}
    grad_fn = _jax.value_and_grad(_loss, argnums=(0, 1))

    def one_microbatch(ex, loss_target):
        ex = dict(ex)
        diff = ex.pop(TWIN_DIFF_INPUT)
        return grad_fn(weights, diff, {**shared, **ex}, loss_target)

    if N_MICROBATCH == 1:
        loss, (grad_w, grad_x) = one_microbatch(per_example, given["loss_target"])
    else:
        def body(carry, xs):
            loss_sum, grad_sum = carry
            l_k, (gw_k, gx_k) = one_microbatch(xs[0], xs[1])
            with _jax.named_scope("update"):
                return (loss_sum + l_k, _jax.tree.map(_jnp.add, grad_sum, gw_k)), gx_k

        init = (_jnp.zeros((), _jnp.float32), _jax.tree.map(_jnp.zeros_like, weights))
        (loss, grad_w), grad_x = _jax.lax.scan(body, init, (per_example, given["loss_target"]))
    with _jax.named_scope("update"):
        delta_w, new_m, new_v = {}, {}, {}
        for n in TWIN_WEIGHTS:
            delta_w[n], new_m[n], new_v[n] = _adamw(weights[n], grad_w[n], given["m_" + n], given["v_" + n])
    return (loss, grad_x, *[grad_w[n] for n in TWIN_WEIGHTS], *[delta_w[n] for n in TWIN_WEIGHTS],
            *[new_m[n] for n in TWIN_WEIGHTS], *[new_v[n] for n in TWIN_WEIGHTS])
```

```python
import functools

import jax
import jax.numpy as jnp
from jax import lax
from jax.experimental import pallas as pl
from jax.experimental.pallas import tpu as pltpu

F32 = jnp.float32
BF16 = jnp.bfloat16
N_DEV = 8
MESH_ID = pl.DeviceIdType.MESH

GRID_W = 64
ROPE_BASE = 10000.0
NORM_EPS = 1e-6
NEG_INF = -1e30

MLA_HEADS = 4
MLA_NOPE = 128
MLA_ROPE = 64
MLA_V = 128
MLA_Q_LORA = 512
MLA_KV_LORA = 256
MLA_SCALE = (MLA_NOPE + MLA_ROPE) ** -0.5
POOL_WINDOWS = (2, 4, 8, 16)
POOL_GROUP = 128
POOL_WIDTH = POOL_GROUP * len(POOL_WINDOWS)
SWA_HEADS = 8
SWA_KV_HEADS = 2
SWA_HEAD_DIM = 64
SWA_WINDOW = 128
SWA_BLOCK = 128
NA_HEADS = 8
NA_HEAD_DIM = 64
NA_KH = 8
NA_KW = 16
NA_QC = 16
NA_KC = NA_QC + NA_KW

A_COLS = MLA_Q_LORA + MLA_KV_LORA + MLA_ROPE
B_COLS = POOL_WIDTH
C_COLS = (SWA_HEADS + 2 * SWA_KV_HEADS) * SWA_HEAD_DIM
D_COLS = 3 * NA_HEADS * NA_HEAD_DIM
IN_COLS = A_COLS + B_COLS + C_COLS + D_COLS
IN_COLS_PAD = 3840
IN_SPLITS = (A_COLS, A_COLS + B_COLS, A_COLS + B_COLS + C_COLS)

ADAM_LR = 0.001
ADAM_B1 = 0.9
ADAM_B2 = 0.999
ADAM_EPS = 1e-08
ADAM_WD = 0.01
ADAM_STEP = 10

VMEM_LIMIT = 48 << 20

ARG_NAMES = ['x', 'c', 'ctx', 'c_ctx', 'w_mod', 'b_mod', 'g_mix', 'g_ffn', 'w_in', 'w_out', 'mla_q_a_norm', 'mla_w_qb', 'mla_kv_a_norm', 'mla_w_kvb', 'mla_q_nope_norm', 'mla_q_rope_norm', 'mla_k_nope_norm', 'mla_k_rope_norm', 'pool_w', 'pool_scale', 'swa_q_norm', 'swa_k_norm', 'swa_sink', 'na_q_norm', 'na_k_norm', 'na_rpb', 'ffn_w_up', 'ffn_conv_w', 'ffn_conv_b', 'ffn_w_down', 'loss_target']
WEIGHTS = ['c_ctx', 'w_mod', 'b_mod', 'g_mix', 'g_ffn', 'w_in', 'w_out', 'mla_q_a_norm', 'mla_w_qb', 'mla_kv_a_norm', 'mla_w_kvb', 'mla_q_nope_norm', 'mla_q_rope_norm', 'mla_k_nope_norm', 'mla_k_rope_norm', 'pool_w', 'pool_scale', 'swa_q_norm', 'swa_k_norm', 'swa_sink', 'na_q_norm', 'na_k_norm', 'na_rpb', 'ffn_w_up', 'ffn_conv_w', 'ffn_conv_b', 'ffn_w_down']
BIG = {'w_in': 2, 'w_out': 1, 'mla_w_qb': 2, 'mla_w_kvb': 2, 'ffn_w_up': 2, 'ffn_w_down': 1}
SMALL = ['c_ctx', 'b_mod', 'g_mix', 'g_ffn', 'mla_q_a_norm', 'mla_kv_a_norm', 'mla_q_nope_norm', 'mla_q_rope_norm', 'mla_k_nope_norm', 'mla_k_rope_norm', 'pool_w', 'pool_scale', 'swa_q_norm', 'swa_k_norm', 'swa_sink', 'na_q_norm', 'na_k_norm', 'na_rpb', 'ffn_conv_b']


def _pcall(body, **kw):
    return pl.pallas_call(body, **kw)


def _my_index():
    return 4 * lax.axis_index("x") + 2 * lax.axis_index("y") + lax.axis_index("c")


def _all_gather(block, name):
    def body(x_ref, out_ref, send_sems, recv_sems, local_sem):
        x, y, c = lax.axis_index("x"), lax.axis_index("y"), lax.axis_index("c")
        me, sibling = (x, y, c), (x, y, 1 - c)
        chips = [(1 - x, y), (x, 1 - y), (1 - x, 1 - y)]

        def slot(px, py, pc):
            return out_ref.at[4 * px + 2 * py + pc]

        def copy(k, blk, to, src=None):
            return pltpu.make_async_remote_copy(
                src_ref=slot(*blk) if src is None else src, dst_ref=slot(*blk),
                send_sem=send_sems.at[k], recv_sem=recv_sems.at[k], device_id=to, device_id_type=MESH_ID)

        mine = pltpu.make_async_copy(x_ref, slot(*me), local_sem)
        mine.start()
        first = [copy(0, me, sibling, src=x_ref)]
        first += [copy(1 + j, me, (*chip, c), src=x_ref) for j, chip in enumerate(chips)]
        for cp in first:
            cp.start()
        passed = [copy(4 + j, (*chip, c), sibling) for j, chip in enumerate(chips)]
        for j, chip in enumerate(chips):
            copy(1 + j, (*chip, c), me).wait_recv()
            passed[j].start()
        copy(0, sibling, me).wait_recv()
        for j, chip in enumerate(chips):
            copy(4 + j, (*chip, 1 - c), me).wait_recv()
        for cp in first + passed:
            cp.wait_send()
        mine.wait()

    return _pcall(
        body, name=name,
        out_shape=jax.ShapeDtypeStruct((N_DEV,) + block.shape, block.dtype),
        in_specs=[pl.BlockSpec(memory_space=pl.ANY)],
        out_specs=pl.BlockSpec(memory_space=pl.ANY),
        scratch_shapes=[pltpu.SemaphoreType.DMA((7,)), pltpu.SemaphoreType.DMA((7,)), pltpu.SemaphoreType.DMA(())],
    )(block)


def _all_to_all(parts, name):
    def body(t_ref, out_ref, send_sems, recv_sems, local_sem):
        x, y, c = lax.axis_index("x"), lax.axis_index("y"), lax.axis_index("c")
        me = 4 * x + 2 * y + c
        mine = pltpu.make_async_copy(t_ref.at[me], out_ref.at[me], local_sem)
        mine.start()

        def peer(k):
            px = 1 - x if k & 4 else x
            py = 1 - y if k & 2 else y
            pc = 1 - c if k & 1 else c
            return px, py, pc

        def copy(k):
            px, py, pc = peer(k)
            return pltpu.make_async_remote_copy(
                src_ref=t_ref.at[4 * px + 2 * py + pc], dst_ref=out_ref.at[me],
                send_sem=send_sems.at[k - 1], recv_sem=recv_sems.at[k - 1], device_id=(px, py, pc), device_id_type=MESH_ID)

        def landed(k):
            px, py, pc = peer(k)
            p = 4 * px + 2 * py + pc
            return pltpu.make_async_remote_copy(
                src_ref=t_ref.at[p], dst_ref=out_ref.at[p],
                send_sem=send_sems.at[k - 1], recv_sem=recv_sems.at[k - 1], device_id=(px, py, pc), device_id_type=MESH_ID)

        sends = [copy(k) for k in range(1, N_DEV)]
        for cp in sends:
            cp.start()
        for k in range(1, N_DEV):
            landed(k).wait_recv()
        for cp in sends:
            cp.wait_send()
        mine.wait()

    return _pcall(
        body, name=name,
        out_shape=jax.ShapeDtypeStruct(parts.shape, parts.dtype),
        in_specs=[pl.BlockSpec(memory_space=pl.ANY)],
        out_specs=pl.BlockSpec(memory_space=pl.ANY),
        scratch_shapes=[pltpu.SemaphoreType.DMA((7,)), pltpu.SemaphoreType.DMA((7,)), pltpu.SemaphoreType.DMA(())],
    )(parts)


_LANE_TILES = (1024, 768, 512, 384, 256, 128)
_ROW_TILES = (1088, 1024, 512, 256, 128)


def _pick(dim, cands):
    for t in cands:
        if dim % t == 0:
            return t
    return dim


def _mm(a, b, mode, out_dtype, name):
    if mode == 'nn':
        (M, K), (_, N) = a.shape, b.shape
        tm, tn, tk = _pick(M, _ROW_TILES), _pick(N, _LANE_TILES), _pick(K, _LANE_TILES)
        a_spec = pl.BlockSpec((tm, tk), lambda i, j, k: (i, k))
        b_spec = pl.BlockSpec((tk, tn), lambda i, j, k: (k, j))
        dn = (((1,), (0,)), ((), ()))
    elif mode == 'nt':
        (M, K), (N, _) = a.shape, b.shape
        tm, tn, tk = _pick(M, _ROW_TILES), _pick(N, _LANE_TILES), _pick(K, _LANE_TILES)
        a_spec = pl.BlockSpec((tm, tk), lambda i, j, k: (i, k))
        b_spec = pl.BlockSpec((tn, tk), lambda i, j, k: (j, k))
        dn = (((1,), (1,)), ((), ()))
    else:
        (K, M), (_, N) = a.shape, b.shape
        tm, tn, tk = _pick(M, _LANE_TILES), _pick(N, _LANE_TILES), _pick(K, _ROW_TILES)
        a_spec = pl.BlockSpec((tk, tm), lambda i, j, k: (k, i))
        b_spec = pl.BlockSpec((tk, tn), lambda i, j, k: (k, j))
        dn = (((0,), (0,)), ((), ()))
    nk = K // tk

    def body(a_ref, b_ref, o_ref, acc):
        @pl.when(pl.program_id(2) == 0)
        def _():
            acc[...] = jnp.zeros_like(acc)

        acc[...] += lax.dot_general(a_ref[...], b_ref[...], dn, preferred_element_type=F32)

        @pl.when(pl.program_id(2) == nk - 1)
        def _():
            o_ref[...] = acc[...].astype(o_ref.dtype)

    return _pcall(
        body, name=name, grid=(M // tm, N // tn, nk), in_specs=[a_spec, b_spec],
        out_specs=pl.BlockSpec((tm, tn), lambda i, j, k: (i, j)),
        out_shape=jax.ShapeDtypeStruct((M, N), out_dtype),
        scratch_shapes=[pltpu.VMEM((tm, tn), F32)],
        compiler_params=pltpu.CompilerParams(
            dimension_semantics=("parallel", "parallel", "arbitrary"), vmem_limit_bytes=VMEM_LIMIT),
    )(a, b)


@jax.custom_vjp
def pmm(a, w):
    return _mm(a.astype(BF16), w, 'nn', F32, 'mm_nn')


def _pmm_fwd(a, w):
    a16 = a.astype(BF16)
    return _mm(a16, w, 'nn', F32, 'mm_nn'), (a16, w)


def _pmm_bwd(res, dy):
    a16, w = res
    dy16 = dy.astype(BF16)
    da = _mm(dy16, w, 'nt', F32, 'mm_nt')
    dw = _mm(a16, dy16, 'tn', BF16, 'mm_tn')
    return da, dw


pmm.defvjp(_pmm_fwd, _pmm_bwd)


def _attn_fwd_call(q, k, v, scale):
    H, nq, dq = q.shape
    nk, dv = v.shape[1], v.shape[2]
    tq = _pick(nq, (256, 128))

    def body(q_ref, k_ref, v_ref, o_ref, lse_ref):
        s = lax.dot_general(q_ref[0], k_ref[0], (((1,), (1,)), ((), ())), preferred_element_type=F32) * scale
        m = jnp.max(s, axis=1, keepdims=True)
        p = jnp.exp(s - m)
        l = jnp.sum(p, axis=1, keepdims=True)
        pn = (p / l).astype(BF16)
        o_ref[...] = jnp.dot(pn, v_ref[0], preferred_element_type=F32)
        lse_ref[0] = m + jnp.log(l)

    return _pcall(
        body, name='mla_attn_fwd', grid=(H, nq // tq),
        in_specs=[pl.BlockSpec((1, tq, dq), lambda h, i: (h, i, 0)),
                  pl.BlockSpec((1, nk, dq), lambda h, i: (h, 0, 0)),
                  pl.BlockSpec((1, nk, dv), lambda h, i: (h, 0, 0))],
        out_specs=[pl.BlockSpec((tq, dv), lambda h, i: (i, h)),
                   pl.BlockSpec((1, tq, 1), lambda h, i: (h, i, 0))],
        out_shape=[jax.ShapeDtypeStruct((nq, H * dv), F32), jax.ShapeDtypeStruct((H, nq, 1), F32)],
        compiler_params=pltpu.CompilerParams(
            dimension_semantics=("parallel", "parallel"), vmem_limit_bytes=VMEM_LIMIT),
    )(q, k, v)


def _attn_bwd_call(q, k, v, o, lse, do, scale):
    H, nq, dq = q.shape
    nk, dv = v.shape[1], v.shape[2]
    tq = _pick(nq, (128,))

    def body(q_ref, k_ref, v_ref, o_ref, lse_ref, do_ref, dq_ref, dk_ref, dv_ref):
        @pl.when(pl.program_id(1) == 0)
        def _():
            dk_ref[...] = jnp.zeros_like(dk_ref)
            dv_ref[...] = jnp.zeros_like(dv_ref)

        q16, k16, v16 = q_ref[0], k_ref[0], v_ref[0]
        do = do_ref[...]
        do16 = do.astype(BF16)
        s = lax.dot_general(q16, k16, (((1,), (1,)), ((), ())), preferred_element_type=F32) * scale
        p = jnp.exp(s - lse_ref[0])
        dv_ref[0] += lax.dot_general(p.astype(BF16), do16, (((0,), (0,)), ((), ())), preferred_element_type=F32)
        dp = lax.dot_general(do16, v16, (((1,), (1,)), ((), ())), preferred_element_type=F32)
        delta = jnp.sum(do * o_ref[...], axis=1, keepdims=True)
        ds16 = (p * (dp - delta) * scale).astype(BF16)
        dq_ref[0] = jnp.dot(ds16, k16, preferred_element_type=F32)
        dk_ref[0] += lax.dot_general(ds16, q16, (((0,), (0,)), ((), ())), preferred_element_type=F32)

    return _pcall(
        body, name='mla_attn_bwd', grid=(H, nq // tq),
        in_specs=[pl.BlockSpec((1, tq, dq), lambda h, i: (h, i, 0)),
                  pl.BlockSpec((1, nk, dq), lambda h, i: (h, 0, 0)),
                  pl.BlockSpec((1, nk, dv), lambda h, i: (h, 0, 0)),
                  pl.BlockSpec((tq, dv), lambda h, i: (i, h)),
                  pl.BlockSpec((1, tq, 1), lambda h, i: (h, i, 0)),
                  pl.BlockSpec((tq, dv), lambda h, i: (i, h))],
        out_specs=[pl.BlockSpec((1, tq, dq), lambda h, i: (h, i, 0)),
                   pl.BlockSpec((1, nk, dq), lambda h, i: (h, 0, 0)),
                   pl.BlockSpec((1, nk, dv), lambda h, i: (h, 0, 0))],
        out_shape=[jax.ShapeDtypeStruct((H, nq, dq), F32), jax.ShapeDtypeStruct((H, nk, dq), F32),
                   jax.ShapeDtypeStruct((H, nk, dv), F32)],
        compiler_params=pltpu.CompilerParams(
            dimension_semantics=("parallel", "arbitrary"), vmem_limit_bytes=VMEM_LIMIT),
    )(q, k, v, o, lse, do)


@functools.partial(jax.custom_vjp, nondiff_argnums=(3,))
def attention(q, k, v, scale):
    return _attn_fwd_call(q.astype(BF16), k.astype(BF16), v.astype(BF16), scale)[0]


def _attention_fwd(q, k, v, scale):
    q16, k16, v16 = q.astype(BF16), k.astype(BF16), v.astype(BF16)
    o, lse = _attn_fwd_call(q16, k16, v16, scale)
    return o, (q16, k16, v16, o, lse)


def _attention_bwd(scale, res, do):
    q16, k16, v16, o, lse = res
    return tuple(_attn_bwd_call(q16, k16, v16, o, lse, do, scale))


attention.defvjp(_attention_fwd, _attention_bwd)


def _win_geometry(kind, n):
    if kind == 'na':
        rows = n // GRID_W
        kh = min(NA_KH, rows)

        def start(i):
            return jnp.clip(i - kh // 2, 0, rows - kh) * GRID_W

        def bidx(i):
            return jnp.clip(i - kh // 2, 0, rows - kh) - i + (NA_KH - 1)

        return GRID_W, kh * GRID_W, start, bidx
    nb = n // SWA_BLOCK

    def start(i):
        return i * SWA_BLOCK

    def bidx(i):
        return jnp.where(i == 0, 0, jnp.where(i == nb - 1, 2, 1))

    return SWA_BLOCK, 3 * SWA_BLOCK, start, bidx


def _dot_nt(a, b):
    return lax.dot_general(a, b, (((1,), (1,)), ((), ())), preferred_element_type=F32)


def _dot_tn(a, b):
    return lax.dot_general(a, b, (((0,), (0,)), ((), ())), preferred_element_type=F32)


def _win_specs(q, k, kc, bias):
    hq, n, d = q.shape
    grp = hq // k.shape[0]
    hb = bias.shape[0]
    return [
        pl.BlockSpec((1, n, d), lambda h: (h, 0, 0)),
        pl.BlockSpec((1,) + k.shape[1:], lambda h: (h // grp, 0, 0)),
        pl.BlockSpec((1,) + k.shape[1:], lambda h: (h // grp, 0, 0)),
        pl.BlockSpec((1,) + kc.shape[1:], lambda h: (h // grp, 0, 0)),
        pl.BlockSpec((1,) + kc.shape[1:], lambda h: (h // grp, 0, 0)),
        pl.BlockSpec((1,) + bias.shape[1:], (lambda h: (h, 0, 0, 0)) if hb > 1 else (lambda h: (0, 0, 0, 0))),
        pl.BlockSpec(memory_space=pltpu.SMEM),
    ]


def _win_fwd_call(q, k, v, kc, vc, bias, sink, kind):
    hq, n, d = q.shape
    scale = d ** -0.5
    qb, wk, start, bidx = _win_geometry(kind, n)
    has_sink = kind == 'swa'

    def body(q_ref, k_ref, v_ref, kc_ref, vc_ref, b_ref, sink_ref, o_ref, lse_ref):
        kc16, vc16 = kc_ref[0], vc_ref[0]
        snk = sink_ref[pl.program_id(0)]

        def step(i, carry):
            qs = pl.multiple_of(i * qb, qb)
            ks = pl.multiple_of(start(i), GRID_W)
            q16 = q_ref[0, pl.ds(qs, qb), :]
            s1 = _dot_nt(q16, k_ref[0, pl.ds(ks, wk), :]) * scale + b_ref[0, bidx(i)]
            s2 = _dot_nt(q16, kc16) * scale
            m = jnp.maximum(jnp.max(s1, axis=1, keepdims=True), jnp.max(s2, axis=1, keepdims=True))
            if has_sink:
                m = jnp.maximum(m, snk)
            p1 = jnp.exp(s1 - m)
            p2 = jnp.exp(s2 - m)
            l = jnp.sum(p1, axis=1, keepdims=True) + jnp.sum(p2, axis=1, keepdims=True)
            if has_sink:
                l = l + jnp.exp(snk - m)
            o = (jnp.dot((p1 / l).astype(BF16), v_ref[0, pl.ds(ks, wk), :], preferred_element_type=F32)
                 + jnp.dot((p2 / l).astype(BF16), vc16, preferred_element_type=F32))
            o_ref[0, pl.ds(qs, qb), :] = o
            lse_ref[0, pl.ds(qs, qb), :] = m + jnp.log(l)
            return carry

        lax.fori_loop(0, n // qb, step, 0)

    return _pcall(
        body, name=kind + '_attn_fwd', grid=(hq,), in_specs=_win_specs(q, k, kc, bias),
        out_specs=[pl.BlockSpec((1, n, d), lambda h: (h, 0, 0)), pl.BlockSpec((1, n, 1), lambda h: (h, 0, 0))],
        out_shape=[jax.ShapeDtypeStruct((hq, n, d), F32), jax.ShapeDtypeStruct((hq, n, 1), F32)],
        compiler_params=pltpu.CompilerParams(dimension_semantics=("parallel",), vmem_limit_bytes=VMEM_LIMIT),
    )(q, k, v, kc, vc, bias, sink)


def _win_bwd_call(q, k, v, kc, vc, bias, sink, o, lse, do, kind):
    hq, n, d = q.shape
    scale = d ** -0.5
    qb, wk, start, bidx = _win_geometry(kind, n)
    has_sink = kind == 'swa'
    bias_grad = kind == 'na'

    def body(q_ref, k_ref, v_ref, kc_ref, vc_ref, b_ref, sink_ref, o_ref, lse_ref, do_ref,
             dq_ref, dk_ref, dv_ref, dkc_ref, dvc_ref, db_ref, dsink_ref):
        kc16, vc16 = kc_ref[0], vc_ref[0]
        snk = sink_ref[pl.program_id(0)]
        dk_ref[...] = jnp.zeros_like(dk_ref)
        dv_ref[...] = jnp.zeros_like(dv_ref)
        dkc_ref[...] = jnp.zeros_like(dkc_ref)
        dvc_ref[...] = jnp.zeros_like(dvc_ref)
        db_ref[...] = jnp.zeros_like(db_ref)

        def step(i, dsink):
            qs = pl.multiple_of(i * qb, qb)
            ks = pl.multiple_of(start(i), GRID_W)
            q16 = q_ref[0, pl.ds(qs, qb), :]
            k16 = k_ref[0, pl.ds(ks, wk), :]
            v16 = v_ref[0, pl.ds(ks, wk), :]
            lse = lse_ref[0, pl.ds(qs, qb), :]
            do = do_ref[0, pl.ds(qs, qb), :]
            do16 = do.astype(BF16)
            p1 = jnp.exp(_dot_nt(q16, k16) * scale + b_ref[0, bidx(i)] - lse)
            p2 = jnp.exp(_dot_nt(q16, kc16) * scale - lse)
            delta = jnp.sum(do * o_ref[0, pl.ds(qs, qb), :], axis=1, keepdims=True)
            ds1 = p1 * (_dot_nt(do16, v16) - delta)
            ds2 = p2 * (_dot_nt(do16, vc16) - delta)
            if bias_grad:
                db_ref[0, bidx(i)] += ds1
            ds1 = (ds1 * scale).astype(BF16)
            ds2 = (ds2 * scale).astype(BF16)
            dq_ref[0, pl.ds(qs, qb), :] = (jnp.dot(ds1, k16, preferred_element_type=F32)
                                          + jnp.dot(ds2, kc16, preferred_element_type=F32))
            dk_ref[0, pl.ds(ks, wk), :] += _dot_tn(ds1, q16)
            dv_ref[0, pl.ds(ks, wk), :] += _dot_tn(p1.astype(BF16), do16)
            dkc_ref[0] += _dot_tn(ds2, q16)
            dvc_ref[0] += _dot_tn(p2.astype(BF16), do16)
            if has_sink:
                dsink = dsink - jnp.sum(jnp.exp(snk - lse) * delta)
            return dsink

        dsink = lax.fori_loop(0, n // qb, step, jnp.zeros((), F32))
        dsink_ref[...] = jnp.full(dsink_ref.shape, dsink, F32)

    per_head = lambda shape: pl.BlockSpec((1,) + shape[1:], lambda h: (h,) + (0,) * (len(shape) - 1))
    kq = (hq,) + k.shape[1:]
    cq = (hq,) + kc.shape[1:]
    bq = (hq,) + bias.shape[1:]
    in_specs = _win_specs(q, k, kc, bias) + [per_head(o.shape), per_head(lse.shape), per_head(do.shape)]
    out_shapes = [q.shape, kq, kq, cq, cq, bq, (hq, 8, 128)]
    return _pcall(
        body, name=kind + '_attn_bwd', grid=(hq,), in_specs=in_specs,
        out_specs=[per_head(s) for s in out_shapes],
        out_shape=[jax.ShapeDtypeStruct(s, F32) for s in out_shapes],
        compiler_params=pltpu.CompilerParams(dimension_semantics=("parallel",), vmem_limit_bytes=VMEM_LIMIT),
    )(q, k, v, kc, vc, bias, sink, o, lse, do)


@functools.partial(jax.custom_vjp, nondiff_argnums=(7,))
def win_attention(q, k, v, kc, vc, bias, sink, kind):
    b16 = lambda t: t.astype(BF16)
    return _win_fwd_call(b16(q), b16(k), b16(v), b16(kc), b16(vc), bias, sink, kind)[0]


def _win_attention_fwd(q, k, v, kc, vc, bias, sink, kind):
    res = tuple(t.astype(BF16) for t in (q, k, v, kc, vc)) + (bias, sink)
    o, lse = _win_fwd_call(*res, kind)
    return o, res + (o, lse)


def _win_attention_bwd(kind, res, do):
    q, k, v, kc, vc, bias, sink, o, lse = res
    dq, dk, dv, dkc, dvc, db, dsink = _win_bwd_call(q, k, v, kc, vc, bias, sink, o, lse, do, kind)
    hkv = k.shape[0]
    fold = lambda t: t.reshape((hkv, -1) + t.shape[1:]).sum(axis=1)
    if bias.shape[0] == 1:
        db = jnp.zeros_like(bias)
    return dq, fold(dk), fold(dv), fold(dkc), fold(dvc), db, dsink[:, 0, 0]


win_attention.defvjp(_win_attention_fwd, _win_attention_bwd)


def _loss_head(y, target):
    n, d = y.shape
    tr = _pick(n, (512, 256, 128))
    nb = n // tr

    def body(y_ref, t_ref, dy_ref, part_ref):
        err = y_ref[...] - t_ref[...]
        dy_ref[...] = err * (1.0 / d)
        part_ref[...] = jnp.full(part_ref.shape, jnp.sum(err * err), F32)

    dy, part = _pcall(
        body, name='loss_head', grid=(nb,),
        in_specs=[pl.BlockSpec((tr, d), lambda i: (i, 0)), pl.BlockSpec((tr, d), lambda i: (i, 0))],
        out_specs=[pl.BlockSpec((tr, d), lambda i: (i, 0)), pl.BlockSpec((1, 8, 128), lambda i: (i, 0, 0))],
        out_shape=[jax.ShapeDtypeStruct((n, d), F32), jax.ShapeDtypeStruct((nb, 8, 128), F32)],
        compiler_params=pltpu.CompilerParams(dimension_semantics=("parallel",), vmem_limit_bytes=VMEM_LIMIT),
    )(y, target)
    return 0.5 * jnp.sum(part[:, 0, 0]) / d, dy


def _sum_parts(parts, name):
    P, R, C = parts.shape
    tr = _pick(R, (256, 128, 64, 32, 16, 8))

    def body(p_ref, o_ref):
        acc = p_ref[0].astype(F32)
        for i in range(1, P):
            acc = acc + p_ref[i].astype(F32)
        o_ref[...] = acc

    return _pcall(
        body, name=name, grid=(R // tr,),
        in_specs=[pl.BlockSpec((P, tr, C), lambda i: (0, i, 0))],
        out_specs=pl.BlockSpec((tr, C), lambda i: (i, 0)),
        out_shape=jax.ShapeDtypeStruct((R, C), F32),
        compiler_params=pltpu.CompilerParams(dimension_semantics=("parallel",), vmem_limit_bytes=VMEM_LIMIT),
    )(parts)


def _adamw(parts, w, m, v, name):
    P, R, C = parts.shape
    tr = _pick(R, (128, 64, 32, 16, 8))
    c1 = 1.0 / (1.0 - ADAM_B1 ** ADAM_STEP)
    c2 = 1.0 / (1.0 - ADAM_B2 ** ADAM_STEP)

    def body(p_ref, w_ref, m_ref, v_ref, g_out, d_out, m_out, v_out):
        g = p_ref[0].astype(F32)
        for i in range(1, P):
            g = g + p_ref[i].astype(F32)
        m_new = ADAM_B1 * m_ref[...] + (1.0 - ADAM_B1) * g
        v_new = ADAM_B2 * v_ref[...] + (1.0 - ADAM_B2) * (g * g)
        g_out[...] = g
        m_out[...] = m_new
        v_out[...] = v_new
        d_out[...] = -ADAM_LR * ((m_new * c1) / (jnp.sqrt(v_new * c2) + ADAM_EPS) + ADAM_WD * w_ref[...])

    blk = pl.BlockSpec((tr, C), lambda i: (i, 0))
    return _pcall(
        body, name=name, grid=(R // tr,),
        in_specs=[pl.BlockSpec((P, tr, C), lambda i: (0, i, 0)), blk, blk, blk],
        out_specs=[blk, blk, blk, blk],
        out_shape=[jax.ShapeDtypeStruct((R, C), F32)] * 4,
        compiler_params=pltpu.CompilerParams(dimension_semantics=("parallel",), vmem_limit_bytes=VMEM_LIMIT),
    )(parts, w, m, v)


def rms_norm(x, g):
    return x * lax.rsqrt(jnp.mean(x * x, axis=-1, keepdims=True) + NORM_EPS) * g


def modulate(h, shift, scale):
    return h * (1.0 + scale) + shift


def axial_angles(n, d_rot):
    t = jnp.arange(n)
    row = (t // GRID_W).astype(F32)
    col = (t % GRID_W).astype(F32)
    d_axis = d_rot // 2
    inv_freq = ROPE_BASE ** (-jnp.arange(0, d_axis, 2, dtype=F32) / d_axis)
    return (row[:, None] * inv_freq, col[:, None] * inv_freq)


def rope_segment(x, ang):
    cos = jnp.cos(ang)[:, None, :]
    sin = jnp.sin(ang)[:, None, :]
    x1, x2 = jnp.split(x, 2, axis=-1)
    return jnp.concatenate([x1 * cos - x2 * sin, x2 * cos + x1 * sin], axis=-1)


def axial_rope(x, ang):
    half = x.shape[-1] // 2
    return jnp.concatenate([rope_segment(x[..., :half], ang[0]), rope_segment(x[..., half:], ang[1])], axis=-1)


def rope_latent(t, n, ang):
    return jnp.concatenate([axial_rope(t[:n], ang), t[n:]], axis=0)


def mla_attend_ctx(q_nope, q_rope, k_nope, k_rope, v):
    s = (jnp.einsum('qhd,khd->hqk', q_nope, k_nope, preferred_element_type=F32)
         + jnp.einsum('qhr,kr->hqk', q_rope, k_rope, preferred_element_type=F32))
    p = jax.nn.softmax(s * MLA_SCALE, axis=-1)
    return jnp.einsum('hqk,khd->qhd', p, v)


def pool_mixer(u, w_pool, scale):
    n = u.shape[0]
    csum = jnp.pad(jnp.cumsum(u, axis=0), ((1, 0), (0, 0)))
    t = jnp.arange(n)
    diffs = []
    for g, w in enumerate(POOL_WINDOWS):
        sl = slice(g * POOL_GROUP, (g + 1) * POOL_GROUP)
        lo = jnp.clip(t - w // 2, 0, n)
        hi = jnp.clip(t + w // 2, 0, n)
        cs = csum[:, sl]
        mean = (cs[hi] - cs[lo]) / (hi - lo).astype(F32)[:, None]
        diffs.append(mean - u[:, sl])
    d = jnp.stack(diffs, axis=1)
    y = jnp.einsum('ngc,gcd->ngd', d, w_pool).reshape(n, POOL_WIDTH)
    return y * scale


def swa_latent(q, k, v, k_ctx, v_ctx, sink):
    n, hq, d = q.shape
    blk = SWA_BLOCK
    a = jnp.arange(blk)[:, None]
    j = jnp.arange(3 * blk)[None, :]
    near = jnp.abs(j - blk - a) <= SWA_WINDOW
    tiles = jnp.stack([near & (j >= blk), near, near & (j < 2 * blk)])
    bias = jnp.where(tiles, 0.0, NEG_INF).astype(F32)[None]
    heads = lambda t: t.transpose(1, 0, 2)
    pad = lambda t: jnp.pad(heads(t), ((0, 0), (blk, blk), (0, 0)))
    o = win_attention(heads(q), pad(k), pad(v), heads(k_ctx), heads(v_ctx), bias, sink, 'swa')
    return o.transpose(1, 0, 2).reshape(n, hq * d)


def ctx_attention(q, k, v, sink):
    nq, hq, d = q.shape
    hkv = k.shape[1]
    grp = hq // hkv
    nk = k.shape[0]
    qg = q.reshape(nq, hkv, grp, d)
    s = jnp.einsum('qhgd,khd->hgqk', qg, k, preferred_element_type=F32) * (d ** -0.5)
    if sink is not None:
        s_sink = jnp.broadcast_to(sink.reshape(hkv, grp)[:, :, None, None], s.shape[:-1] + (1,))
        s = jnp.concatenate([s, s_sink], axis=-1)
    p = jax.nn.softmax(s, axis=-1)[..., :nk]
    o = jnp.einsum('hgqk,khd->qhgd', p, v)
    return o.reshape(nq, hq * d)


def na_bias_tiles(rpb, n):
    rows = n // GRID_W
    kh = min(NA_KH, rows)
    qc = jnp.arange(GRID_W)[:, None]
    kc = jnp.arange(GRID_W)[None, :]
    dc = jnp.clip(kc - qc, 1 - NA_KW, NA_KW - 1) + (NA_KW - 1)
    onehot = (dc[None] == jnp.arange(2 * NA_KW - 1)[:, None, None]).astype(F32)
    toeplitz = jnp.einsum('hdt,tqk->hdqk', rpb, onehot, precision=lax.Precision.HIGHEST)
    q_col0 = jnp.clip(qc - NA_KW // 2, 0, GRID_W - NA_KW)
    valid = (kc >= q_col0) & (kc < q_col0 + NA_KW)
    masked = jnp.where(valid, toeplitz, NEG_INF)
    return jnp.stack([jnp.concatenate([masked[:, off + j] for j in range(kh)], axis=-1) for off in range(NA_KH)], axis=1)


def na_latent(q, k, v, k_ctx, v_ctx, rpb):
    n, h, d = q.shape
    heads = lambda t: t.transpose(1, 0, 2)
    o = win_attention(heads(q), heads(k), heads(v), heads(k_ctx), heads(v_ctx), na_bias_tiles(rpb, n),
                      jnp.zeros((h,), F32), 'na')
    return o.transpose(1, 0, 2).reshape(n, h * d)


def conv_gate(a, conv_w, conv_b):
    ap = jnp.pad(a, ((1, 1), (0, 0)))
    a = ap[:-2] * conv_w[0] + ap[1:-1] * conv_w[1] + ap[2:] * conv_w[2] + conv_b
    gate, val = jnp.split(a, 2, axis=-1)
    return jax.nn.silu(gate) * val


def _forward(x, mod_x, mod_c, wb, ws, ctx):
    n = x.shape[0]
    depth = mod_x.shape[0]
    ang_mla = axial_angles(n, MLA_ROPE)
    ang_swa = axial_angles(n, SWA_HEAD_DIM)
    for l in range(depth):
        update_ctx = l < depth - 1
        sh_m, sc_m, gt_m, sh_f, sc_f, gt_f = jnp.split(mod_x[l], 6)
        csh_m, csc_m, cgt_m, csh_f, csc_f, cgt_f = jnp.split(mod_c[l], 6)

        h_all = jnp.concatenate([modulate(rms_norm(x, ws['g_mix'][l]), sh_m, sc_m),
                                 modulate(rms_norm(ctx, ws['g_mix'][l]), csh_m, csc_m)], axis=0)
        p_all = pmm(h_all, wb['w_in'][l])[:, :IN_COLS]
        mla_p, pool_p, swa_p, na_p = jnp.split(p_all, IN_SPLITS, axis=-1)
        T = p_all.shape[0]

        cq, ckv, kr = jnp.split(mla_p, [MLA_Q_LORA, MLA_Q_LORA + MLA_KV_LORA], axis=-1)
        q = pmm(rms_norm(cq, ws['mla_q_a_norm'][l]), wb['mla_w_qb'][l]).reshape(T, MLA_HEADS, MLA_NOPE + MLA_ROPE)
        kv = pmm(rms_norm(ckv, ws['mla_kv_a_norm'][l]), wb['mla_w_kvb'][l]).reshape(T, MLA_HEADS, MLA_NOPE + MLA_V)
        q_nope = rms_norm(q[..., :MLA_NOPE], ws['mla_q_nope_norm'][l])
        q_rope = rope_latent(rms_norm(q[..., MLA_NOPE:], ws['mla_q_rope_norm'][l]), n, ang_mla)
        k_nope = rms_norm(kv[..., :MLA_NOPE], ws['mla_k_nope_norm'][l])
        v_mla = kv[..., MLA_NOPE:]
        k_rope = rope_latent(rms_norm(kr, ws['mla_k_rope_norm'][l])[:, None, :], n, ang_mla)
        q_cat = jnp.concatenate([q_nope, q_rope], axis=-1).transpose(1, 0, 2)
        k_cat = jnp.concatenate([k_nope, jnp.broadcast_to(k_rope, (T, MLA_HEADS, MLA_ROPE))], axis=-1).transpose(1, 0, 2)
        out_a = attention(q_cat[:, :n], k_cat, v_mla.transpose(1, 0, 2), MLA_SCALE)

        out_b = pool_mixer(pool_p[:n], ws['pool_w'][l], ws['pool_scale'][l])

        sq, sk, sv = jnp.split(swa_p, [SWA_HEADS * SWA_HEAD_DIM, (SWA_HEADS + SWA_KV_HEADS) * SWA_HEAD_DIM], axis=-1)
        sq = rope_latent(rms_norm(sq.reshape(T, SWA_HEADS, SWA_HEAD_DIM), ws['swa_q_norm'][l]), n, ang_swa)
        sk = rope_latent(rms_norm(sk.reshape(T, SWA_KV_HEADS, SWA_HEAD_DIM), ws['swa_k_norm'][l]), n, ang_swa)
        sv = sv.reshape(T, SWA_KV_HEADS, SWA_HEAD_DIM)
        out_c = swa_latent(sq[:n], sk[:n], sv[:n], sk[n:], sv[n:], ws['swa_sink'][l])

        nq_, nk_, nv_ = jnp.split(na_p, 3, axis=-1)
        nq_ = rms_norm(nq_.reshape(T, NA_HEADS, NA_HEAD_DIM), ws['na_q_norm'][l])
        nk_ = rms_norm(nk_.reshape(T, NA_HEADS, NA_HEAD_DIM), ws['na_k_norm'][l])
        nv_ = nv_.reshape(T, NA_HEADS, NA_HEAD_DIM)
        out_d = na_latent(nq_[:n], nk_[:n], nv_[:n], nk_[n:], nv_[n:], ws['na_rpb'][l])

        mix_x = jnp.concatenate([out_a, out_b, out_c, out_d], axis=-1)
        if update_ctx:
            L = T - n
            mix_c = jnp.concatenate([
                mla_attend_ctx(q_nope[n:], q_rope[n:], k_nope[n:], k_rope[n:, 0], v_mla[n:]).reshape(L, MLA_HEADS * MLA_V),
                pool_mixer(pool_p[n:], ws['pool_w'][l], ws['pool_scale'][l]),
                ctx_attention(sq[n:], sk[n:], sv[n:], ws['swa_sink'][l]),
                ctx_attention(nq_[n:], nk_[n:], nv_[n:], None),
            ], axis=-1)
            o_all = pmm(jnp.concatenate([mix_x, mix_c], axis=0), wb['w_out'][l])
            x = x + gt_m * o_all[:n]
            ctx = ctx + cgt_m * o_all[n:]
            h2 = jnp.concatenate([modulate(rms_norm(x, ws['g_ffn'][l]), sh_f, sc_f),
                                  modulate(rms_norm(ctx, ws['g_ffn'][l]), csh_f, csc_f)], axis=0)
            a = pmm(h2, wb['ffn_w_up'][l])
            u = jnp.concatenate([conv_gate(a[:n], ws['ffn_conv_w'][l], ws['ffn_conv_b'][l]),
                                 conv_gate(a[n:], ws['ffn_conv_w'][l], ws['ffn_conv_b'][l])], axis=0)
            f_all = pmm(u, wb['ffn_w_down'][l])
            x = x + gt_f * f_all[:n]
            ctx = ctx + cgt_f * f_all[n:]
        else:
            x = x + gt_m * pmm(mix_x, wb['w_out'][l])
            h2 = modulate(rms_norm(x, ws['g_ffn'][l]), sh_f, sc_f)
            u = conv_gate(pmm(h2, wb['ffn_w_up'][l]), ws['ffn_conv_w'][l], ws['ffn_conv_b'][l])
            x = x + gt_f * pmm(u, wb['ffn_w_down'][l])
    return x


def _gather_weight(w, axis, name):
    g = _all_gather(w.astype(BF16), name)
    depth, r, c = w.shape
    if axis == 1:
        return g.transpose(1, 0, 2, 3).reshape(depth, N_DEV * r, c)
    return g.transpose(1, 2, 0, 3).reshape(depth, r, N_DEV * c)


def _scatter_layout(g, axis):
    depth, R, C = g.shape
    if axis == 1:
        return g.reshape(depth, N_DEV, R // N_DEV, C).transpose(1, 0, 2, 3)
    return g.reshape(depth, R, N_DEV, C // N_DEV).transpose(2, 0, 1, 3)


def _pack_rows(vecs):
    flat = jnp.concatenate([v.reshape(-1).astype(F32) for v in vecs])
    pad = (-flat.shape[0]) % 1024
    return jnp.pad(flat, (0, pad)).reshape(-1, 128)


def _unpack(flat, shapes):
    out, off = [], 0
    for s in shapes:
        size = 1
        for d in s:
            size *= d
        out.append(flat[off:off + size].reshape(s))
        off += size
    return out


def _silu_grad(z):
    s = jax.nn.sigmoid(z)
    return s * (1.0 + z * (1.0 - s))


def kernel(x, c, ctx, c_ctx, w_mod, b_mod, g_mix, g_ffn, w_in, w_out, mla_q_a_norm, mla_w_qb, mla_kv_a_norm, mla_w_kvb, mla_q_nope_norm, mla_q_rope_norm, mla_k_nope_norm, mla_k_rope_norm, pool_w, pool_scale, swa_q_norm, swa_k_norm, swa_sink, na_q_norm, na_k_norm, na_rpb, ffn_w_up, ffn_conv_w, ffn_conv_b, ffn_w_down, loss_target, m_c_ctx, m_w_mod, m_b_mod, m_g_mix, m_g_ffn, m_w_in, m_w_out, m_mla_q_a_norm, m_mla_w_qb, m_mla_kv_a_norm, m_mla_w_kvb, m_mla_q_nope_norm, m_mla_q_rope_norm, m_mla_k_nope_norm, m_mla_k_rope_norm, m_pool_w, m_pool_scale, m_swa_q_norm, m_swa_k_norm, m_swa_sink, m_na_q_norm, m_na_k_norm, m_na_rpb, m_ffn_w_up, m_ffn_conv_w, m_ffn_conv_b, m_ffn_w_down, v_c_ctx, v_w_mod, v_b_mod, v_g_mix, v_g_ffn, v_w_in, v_w_out, v_mla_q_a_norm, v_mla_w_qb, v_mla_kv_a_norm, v_mla_w_kvb, v_mla_q_nope_norm, v_mla_q_rope_norm, v_mla_k_nope_norm, v_mla_k_rope_norm, v_pool_w, v_pool_scale, v_swa_q_norm, v_swa_k_norm, v_swa_sink, v_na_q_norm, v_na_k_norm, v_na_rpb, v_ffn_w_up, v_ffn_conv_w, v_ffn_conv_b, v_ffn_w_down):
    return _step(x, c, ctx, c_ctx, w_mod, b_mod, g_mix, g_ffn, w_in, w_out, mla_q_a_norm, mla_w_qb, mla_kv_a_norm, mla_w_kvb, mla_q_nope_norm, mla_q_rope_norm, mla_k_nope_norm, mla_k_rope_norm, pool_w, pool_scale, swa_q_norm, swa_k_norm, swa_sink, na_q_norm, na_k_norm, na_rpb, ffn_w_up, ffn_conv_w, ffn_conv_b, ffn_w_down, loss_target, m_c_ctx, m_w_mod, m_b_mod, m_g_mix, m_g_ffn, m_w_in, m_w_out, m_mla_q_a_norm, m_mla_w_qb, m_mla_kv_a_norm, m_mla_w_kvb, m_mla_q_nope_norm, m_mla_q_rope_norm, m_mla_k_nope_norm, m_mla_k_rope_norm, m_pool_w, m_pool_scale, m_swa_q_norm, m_swa_k_norm, m_swa_sink, m_na_q_norm, m_na_k_norm, m_na_rpb, m_ffn_w_up, m_ffn_conv_w, m_ffn_conv_b, m_ffn_w_down, v_c_ctx, v_w_mod, v_b_mod, v_g_mix, v_g_ffn, v_w_in, v_w_out, v_mla_q_a_norm, v_mla_w_qb, v_mla_kv_a_norm, v_mla_w_kvb, v_mla_q_nope_norm, v_mla_q_rope_norm, v_mla_k_nope_norm, v_mla_k_rope_norm, v_pool_w, v_pool_scale, v_swa_q_norm, v_swa_k_norm, v_swa_sink, v_na_q_norm, v_na_k_norm, v_na_rpb, v_ffn_w_up, v_ffn_conv_w, v_ffn_conv_b, v_ffn_w_down)


def _step(*args):
    n_in = len(ARG_NAMES)
    n_w = len(WEIGHTS)
    given = dict(zip(ARG_NAMES, args[:n_in]))
    mom = dict(zip(WEIGHTS, args[n_in:n_in + n_w]))
    var = dict(zip(WEIGHTS, args[n_in + n_w:n_in + 2 * n_w]))
    me = _my_index()

    x = given['x'][0]
    ctx = given['ctx'][0]
    target = given['loss_target'][0]
    n, D = x.shape
    depth = given['w_mod'].shape[0]
    mod_cols = given['w_mod'].shape[2]
    conv_cols = given['ffn_conv_w'].shape[2]

    wb = {name: _gather_weight(given[name], axis, 'ag_' + name) for name, axis in BIG.items()}
    wb['w_in'] = jnp.pad(wb['w_in'], ((0, 0), (0, 0), (0, IN_COLS_PAD - IN_COLS)))
    misc = _all_gather(_pack_rows([given['c'], given['ffn_conv_w']]), 'ag_cond')
    misc = misc.reshape(N_DEV, -1)
    c_all = misc[:, :D]
    conv_w = misc[:, D:D + depth * 3 * conv_cols].reshape(N_DEV, depth, 3, conv_cols)
    conv_w = conv_w.transpose(1, 2, 0, 3).reshape(depth, 3, N_DEV * conv_cols)

    cond = jnp.concatenate([c_all, given['c_ctx'][None], jnp.zeros((16 - N_DEV - 1, D), F32)], axis=0)
    s16 = jax.nn.silu(cond).astype(BF16)
    wm16 = given['w_mod'].astype(BF16)
    b_loc = lax.dynamic_slice_in_dim(given['b_mod'], me * mod_cols, mod_cols, axis=1)
    mod_part = jnp.stack([_mm(s16, wm16[l], 'nn', F32, 'mod_fwd') + b_loc[l] for l in range(depth)])
    mod_all = _all_gather(mod_part, 'ag_mod').transpose(1, 2, 0, 3).reshape(depth, 16, N_DEV * mod_cols)
    mod_x = lax.dynamic_index_in_dim(mod_all, me, axis=1, keepdims=False)
    mod_c = mod_all[:, N_DEV]

    ws = {name: given[name] for name in SMALL if name not in ('c_ctx', 'b_mod')}
    ws['ffn_conv_w'] = conv_w
    y, vjp = jax.vjp(lambda *d: _forward(*d, ctx), x, mod_x, mod_c, wb, ws)
    loss_local, dy = _loss_head(y, target)
    g_x, g_mod_x, g_mod_c, g_wb, g_ws = vjp(dy)

    g_wb['w_in'] = g_wb['w_in'][:, :, :IN_COLS]
    parts = {name: _all_to_all(_scatter_layout(g_wb[name], axis), 'a2a_' + name) for name, axis in BIG.items()}

    g_mod = jnp.zeros((depth, 16, N_DEV * mod_cols), F32)
    g_mod = lax.dynamic_update_slice_in_dim(g_mod, g_mod_x[:, None, :], me, axis=1)
    g_mod = g_mod.at[:, N_DEV].set(g_mod_c)
    g_mod_parts = g_mod.reshape(depth, 16, N_DEV, mod_cols).transpose(2, 0, 1, 3).reshape(N_DEV, -1)
    g_conv_parts = g_ws['ffn_conv_w'].reshape(depth, 3, N_DEV, conv_cols).transpose(2, 0, 1, 3).reshape(N_DEV, -1)
    n_mod = depth * 16 * mod_cols
    n_conv = depth * 3 * conv_cols
    f32_parts = jnp.concatenate([g_mod_parts, g_conv_parts], axis=1)
    f32_pad = (-f32_parts.shape[1]) % 1024
    f32_parts = jnp.pad(f32_parts, ((0, 0), (0, f32_pad))).reshape(N_DEV, -1, 128)
    f32_parts = _all_to_all(f32_parts, 'a2a_f32')
    f32_sum = _sum_parts(f32_parts, 'sum_f32').reshape(-1)
    g_mod_loc = f32_sum[:n_mod].reshape(depth, 16, mod_cols)
    g_conv_loc = f32_sum[n_mod:n_mod + n_conv].reshape(depth * 3, conv_cols)

    g_mod16 = g_mod_loc.astype(BF16)
    g_w_mod = jnp.stack([_mm(s16, g_mod16[l], 'tn', F32, 'mod_dw') for l in range(depth)])
    d_silu = sum(_mm(g_mod16[l], wm16[l], 'nt', F32, 'mod_dc') for l in range(depth))
    g_c_ctx_part = d_silu[N_DEV] * _silu_grad(given['c_ctx'])

    small_grads = {name: g_ws[name] for name in SMALL if name not in ('c_ctx', 'b_mod')}
    small_grads['c_ctx'] = g_c_ctx_part
    small_grads['b_mod'] = g_mod_x + g_mod_c
    small_shapes = [given[name].shape for name in SMALL]
    n_small = sum(int(given[name].size) for name in SMALL)
    packed = _pack_rows([small_grads[name] for name in SMALL] + [loss_local.reshape(1)])
    small_parts = _all_gather(packed, 'ag_small')

    out_g, out_d, out_m, out_v = {}, {}, {}, {}

    def put(name, res, shape):
        out_g[name], out_d[name], out_m[name], out_v[name] = (r.reshape(shape) for r in res)

    for name in BIG:
        shape = given[name].shape
        flat = (shape[0] * shape[1], shape[2])
        res = _adamw(parts[name].reshape((N_DEV,) + flat), given[name].reshape(flat), mom[name].reshape(flat),
                     var[name].reshape(flat), 'adamw_' + name)
        put(name, res, shape)
    shape = given['w_mod'].shape
    flat = (shape[0] * shape[1], shape[2])
    res = _adamw(g_w_mod.reshape((1,) + flat), given['w_mod'].reshape(flat), mom['w_mod'].reshape(flat),
                 var['w_mod'].reshape(flat), 'adamw_w_mod')
    put('w_mod', res, shape)
    shape = given['ffn_conv_w'].shape
    flat = (shape[0] * shape[1], shape[2])
    res = _adamw(g_conv_loc.reshape((1,) + flat), given['ffn_conv_w'].reshape(flat), mom['ffn_conv_w'].reshape(flat),
                 var['ffn_conv_w'].reshape(flat), 'adamw_conv_w')
    put('ffn_conv_w', res, shape)

    zero1 = jnp.zeros((1,), F32)
    res = _adamw(small_parts, _pack_rows([given[k] for k in SMALL] + [zero1]), _pack_rows([mom[k] for k in SMALL] + [zero1]),
                 _pack_rows([var[k] for k in SMALL] + [zero1 + 1.0]), 'adamw_small')
    flats = [r.reshape(-1) for r in res]
    for name, g_, d_, m_, v_ in zip(SMALL, *[_unpack(f, small_shapes) for f in flats]):
        out_g[name], out_d[name], out_m[name], out_v[name] = g_, d_, m_, v_
    loss = flats[0][n_small]

    return (loss, g_x[None], *[out_g[k] for k in WEIGHTS], *[out_d[k] for k in WEIGHTS],
            *[out_m[k] for k in WEIGHTS], *[out_v[k] for k in WEIGHTS])
```

```python
import functools

import jax
import jax.numpy as jnp
from jax import lax
from jax.experimental import pallas as pl
from jax.experimental.pallas import tpu as pltpu

F32 = jnp.float32
BF16 = jnp.bfloat16
N_DEV = 8
MESH_ID = pl.DeviceIdType.MESH

GRID_W = 64
ROPE_BASE = 10000.0
NORM_EPS = 1e-6
NEG_INF = -1e30

MLA_HEADS = 4
MLA_NOPE = 128
MLA_ROPE = 64
MLA_V = 128
MLA_Q_LORA = 512
MLA_KV_LORA = 256
MLA_SCALE = (MLA_NOPE + MLA_ROPE) ** -0.5
POOL_WINDOWS = (2, 4, 8, 16)
POOL_GROUP = 128
POOL_WIDTH = POOL_GROUP * len(POOL_WINDOWS)
SWA_HEADS = 8
SWA_KV_HEADS = 2
SWA_HEAD_DIM = 64
SWA_WINDOW = 128
SWA_BLOCK = 128
NA_HEADS = 8
NA_HEAD_DIM = 64
NA_KH = 8
NA_KW = 16
NA_QC = 16
NA_KC = NA_QC + NA_KW

A_COLS = MLA_Q_LORA + MLA_KV_LORA + MLA_ROPE
B_COLS = POOL_WIDTH
C_COLS = (SWA_HEADS + 2 * SWA_KV_HEADS) * SWA_HEAD_DIM
D_COLS = 3 * NA_HEADS * NA_HEAD_DIM
IN_COLS = A_COLS + B_COLS + C_COLS + D_COLS
IN_COLS_PAD = 3840
IN_SPLITS = (A_COLS, A_COLS + B_COLS, A_COLS + B_COLS + C_COLS)

ADAM_LR = 0.001
ADAM_B1 = 0.9
ADAM_B2 = 0.999
ADAM_EPS = 1e-08
ADAM_WD = 0.01
ADAM_STEP = 10

VMEM_LIMIT = 48 << 20

ARG_NAMES = ['x', 'c', 'ctx', 'c_ctx', 'w_mod', 'b_mod', 'g_mix', 'g_ffn', 'w_in', 'w_out', 'mla_q_a_norm', 'mla_w_qb', 'mla_kv_a_norm', 'mla_w_kvb', 'mla_q_nope_norm', 'mla_q_rope_norm', 'mla_k_nope_norm', 'mla_k_rope_norm', 'pool_w', 'pool_scale', 'swa_q_norm', 'swa_k_norm', 'swa_sink', 'na_q_norm', 'na_k_norm', 'na_rpb', 'ffn_w_up', 'ffn_conv_w', 'ffn_conv_b', 'ffn_w_down', 'loss_target']
WEIGHTS = ['c_ctx', 'w_mod', 'b_mod', 'g_mix', 'g_ffn', 'w_in', 'w_out', 'mla_q_a_norm', 'mla_w_qb', 'mla_kv_a_norm', 'mla_w_kvb', 'mla_q_nope_norm', 'mla_q_rope_norm', 'mla_k_nope_norm', 'mla_k_rope_norm', 'pool_w', 'pool_scale', 'swa_q_norm', 'swa_k_norm', 'swa_sink', 'na_q_norm', 'na_k_norm', 'na_rpb', 'ffn_w_up', 'ffn_conv_w', 'ffn_conv_b', 'ffn_w_down']
BIG = {'w_in': 2, 'w_out': 1, 'mla_w_qb': 2, 'mla_w_kvb': 2, 'ffn_w_up': 2, 'ffn_w_down': 1}
SMALL = ['c_ctx', 'b_mod', 'g_mix', 'g_ffn', 'mla_q_a_norm', 'mla_kv_a_norm', 'mla_q_nope_norm', 'mla_q_rope_norm', 'mla_k_nope_norm', 'mla_k_rope_norm', 'pool_w', 'pool_scale', 'swa_q_norm', 'swa_k_norm', 'swa_sink', 'na_q_norm', 'na_k_norm', 'na_rpb', 'ffn_conv_b']


def _pcall(body, **kw):
    return pl.pallas_call(body, **kw)


def _my_index():
    return 4 * lax.axis_index("x") + 2 * lax.axis_index("y") + lax.axis_index("c")


_COMM_SCRATCH = [pltpu.SemaphoreType.DMA((7,)), pltpu.SemaphoreType.DMA((7,)), pltpu.SemaphoreType.DMA(())]
_ANY = pl.BlockSpec(memory_space=pl.ANY)


def _gather_copies(x_ref, out_ref, send_sems, recv_sems, local_sem):
    x, y, c = lax.axis_index("x"), lax.axis_index("y"), lax.axis_index("c")
    me, sibling = (x, y, c), (x, y, 1 - c)
    chips = [(1 - x, y), (x, 1 - y), (1 - x, 1 - y)]

    def slot(px, py, pc):
        return out_ref.at[4 * px + 2 * py + pc]

    def copy(k, blk, to, src=None):
        return pltpu.make_async_remote_copy(
            src_ref=slot(*blk) if src is None else src, dst_ref=slot(*blk),
            send_sem=send_sems.at[k], recv_sem=recv_sems.at[k], device_id=to, device_id_type=MESH_ID)

    mine = pltpu.make_async_copy(x_ref, slot(*me), local_sem)
    first = [copy(0, me, sibling, src=x_ref)] + [copy(1 + j, me, (*chip, c), src=x_ref) for j, chip in enumerate(chips)]

    def start():
        mine.start()
        for cp in first:
            cp.start()

    def finish():
        passed = [copy(4 + j, (*chip, c), sibling) for j, chip in enumerate(chips)]
        for j, chip in enumerate(chips):
            copy(1 + j, (*chip, c), me).wait_recv()
            passed[j].start()
        copy(0, sibling, me).wait_recv()
        for j, chip in enumerate(chips):
            copy(4 + j, (*chip, 1 - c), me).wait_recv()
        for cp in first + passed:
            cp.wait_send()
        mine.wait()

    return start, finish


def _exchange_copies(t_ref, out_ref, send_sems, recv_sems, local_sem):
    x, y, c = lax.axis_index("x"), lax.axis_index("y"), lax.axis_index("c")
    me = 4 * x + 2 * y + c

    def peer(k):
        return (1 - x if k & 4 else x), (1 - y if k & 2 else y), (1 - c if k & 1 else c)

    def copy(k, landed):
        px, py, pc = peer(k)
        p = 4 * px + 2 * py + pc
        return pltpu.make_async_remote_copy(
            src_ref=t_ref.at[p], dst_ref=out_ref.at[p if landed else me],
            send_sem=send_sems.at[k - 1], recv_sem=recv_sems.at[k - 1], device_id=(px, py, pc), device_id_type=MESH_ID)

    mine = pltpu.make_async_copy(t_ref.at[me], out_ref.at[me], local_sem)
    sends = [copy(k, False) for k in range(1, N_DEV)]

    def start():
        mine.start()
        for cp in sends:
            cp.start()

    def finish():
        for k in range(1, N_DEV):
            copy(k, True).wait_recv()
        for cp in sends:
            cp.wait_send()
        mine.wait()

    return start, finish


_COMM = {'gather': _gather_copies, 'exchange': _exchange_copies}


def _comm_out_shape(kind, operand):
    shape = (N_DEV,) + operand.shape if kind == 'gather' else operand.shape
    return jax.ShapeDtypeStruct(shape, operand.dtype)


def _comm_call(kind, operand, name):
    def body(x_ref, out_ref, send_sems, recv_sems, local_sem):
        start, finish = _COMM[kind](x_ref, out_ref, send_sems, recv_sems, local_sem)
        start()
        finish()

    return _pcall(body, name=name, out_shape=_comm_out_shape(kind, operand), in_specs=[_ANY], out_specs=_ANY,
                  scratch_shapes=_COMM_SCRATCH)(operand)


def _all_gather(block, name):
    return _comm_call('gather', block, name)


def _all_to_all(parts, name):
    return _comm_call('exchange', parts, name)


_LANE_TILES = (1024, 768, 512, 384, 256, 128)
_ROW_TILES = (1088, 1024, 512, 256, 128)


def _pick(dim, cands):
    for t in cands:
        if dim % t == 0:
            return t
    return dim


def _mm(a, b, mode, out_dtype, name, comm=None):
    if mode == 'nn':
        (M, K), (_, N) = a.shape, b.shape
        tm, tn, tk = _pick(M, _ROW_TILES), _pick(N, _LANE_TILES), _pick(K, _LANE_TILES)
        a_spec = pl.BlockSpec((tm, tk), lambda i, j, k: (i, k))
        b_spec = pl.BlockSpec((tk, tn), lambda i, j, k: (k, j))
        dn = (((1,), (0,)), ((), ()))
    elif mode == 'nt':
        (M, K), (N, _) = a.shape, b.shape
        tm, tn, tk = _pick(M, _ROW_TILES), _pick(N, _LANE_TILES), _pick(K, _LANE_TILES)
        a_spec = pl.BlockSpec((tm, tk), lambda i, j, k: (i, k))
        b_spec = pl.BlockSpec((tn, tk), lambda i, j, k: (j, k))
        dn = (((1,), (1,)), ((), ()))
    else:
        (K, M), (_, N) = a.shape, b.shape
        tm, tn, tk = _pick(M, _LANE_TILES), _pick(N, _LANE_TILES), _pick(K, _ROW_TILES)
        a_spec = pl.BlockSpec((tk, tm), lambda i, j, k: (k, i))
        b_spec = pl.BlockSpec((tk, tn), lambda i, j, k: (k, j))
        dn = (((0,), (0,)), ((), ()))
    grid = (M // tm, N // tn, K // tk)

    def matmul_step(a_ref, b_ref, o_ref, acc):
        @pl.when(pl.program_id(2) == 0)
        def _():
            acc[...] = jnp.zeros_like(acc)

        acc[...] += lax.dot_general(a_ref[...], b_ref[...], dn, preferred_element_type=F32)

        @pl.when(pl.program_id(2) == grid[2] - 1)
        def _():
            o_ref[...] = acc[...].astype(o_ref.dtype)

    o_spec = pl.BlockSpec((tm, tn), lambda i, j, k: (i, j))
    o_shape = jax.ShapeDtypeStruct((M, N), out_dtype)
    acc = pltpu.VMEM((tm, tn), F32)
    if comm is None:
        return _pcall(
            matmul_step, name=name, grid=grid, in_specs=[a_spec, b_spec], out_specs=o_spec, out_shape=o_shape,
            scratch_shapes=[acc],
            compiler_params=pltpu.CompilerParams(
                dimension_semantics=("parallel", "parallel", "arbitrary"), vmem_limit_bytes=VMEM_LIMIT),
        )(a, b)

    kind, operand = comm

    def body(a_ref, b_ref, x_ref, o_ref, out_ref, acc_ref, send_sems, recv_sems, local_sem):
        start, finish = _COMM[kind](x_ref, out_ref, send_sems, recv_sems, local_sem)
        step = (pl.program_id(0) * grid[1] + pl.program_id(1)) * grid[2] + pl.program_id(2)

        @pl.when(step == 0)
        def _():
            start()

        matmul_step(a_ref, b_ref, o_ref, acc_ref)

        @pl.when(step == grid[0] * grid[1] * grid[2] - 1)
        def _():
            finish()

    return _pcall(
        body, name=name, grid=grid, in_specs=[a_spec, b_spec, _ANY], out_specs=[o_spec, _ANY],
        out_shape=[o_shape, _comm_out_shape(kind, operand)], scratch_shapes=[acc] + _COMM_SCRATCH,
        compiler_params=pltpu.CompilerParams(
            dimension_semantics=("arbitrary", "arbitrary", "arbitrary"), vmem_limit_bytes=VMEM_LIMIT),
    )(a, b, operand)


def _proj_fwd(a16, w, next_shard, name):
    if next_shard is None:
        return _mm(a16, w, 'nn', F32, name + '_nn'), None
    return _mm(a16, w, 'nn', F32, name + '_nn_gather', comm=('gather', next_shard.astype(BF16)))


def _proj_bwd(a16, w, dy, d_gathered, name):
    dy16 = dy.astype(BF16)
    if d_gathered is None:
        dw, d_shard = _mm(a16, dy16, 'tn', BF16, name + '_tn'), None
    else:
        dw, parts = _mm(a16, dy16, 'tn', BF16, name + '_tn_exchange', comm=('exchange', d_gathered))
        d_shard = _sum_parts(parts.reshape(N_DEV, -1, parts.shape[-1]), name + '_sum').reshape(parts.shape[1:])
    return _mm(dy16, w, 'nt', F32, name + '_nt'), dw, d_shard


@functools.partial(jax.custom_vjp, nondiff_argnums=(2,))
def pmm(a, w, name):
    return _proj_fwd(a.astype(BF16), w, None, name)[0]


def _pmm_fwd(a, w, name):
    a16 = a.astype(BF16)
    return _proj_fwd(a16, w, None, name)[0], (a16, w)


def _pmm_bwd(name, res, dy):
    return _proj_bwd(*res, dy, None, name)[:2]


pmm.defvjp(_pmm_fwd, _pmm_bwd)


@functools.partial(jax.custom_vjp, nondiff_argnums=(3,))
def pmm_carry(a, w, next_shard, name):
    return _proj_fwd(a.astype(BF16), w, next_shard, name)


def _pmm_carry_fwd(a, w, next_shard, name):
    a16 = a.astype(BF16)
    return _proj_fwd(a16, w, next_shard, name), (a16, w)


def _pmm_carry_bwd(name, res, cts):
    return _proj_bwd(*res, cts[0], cts[1], name)


pmm_carry.defvjp(_pmm_carry_fwd, _pmm_carry_bwd)


FFN_TILE = 128


def _neighbours(x, n):
    T = x.shape[0]
    t = lax.broadcasted_iota(jnp.int32, x.shape, 0)
    prev = jnp.where((t == 0) | (t == n), 0.0, pltpu.roll(x, 1, 0))
    nxt = jnp.where((t == n - 1) | (t == T - 1), 0.0, pltpu.roll(x, T - 1, 0))
    return prev, nxt


def _gate_fwd_call(a, cw, cb, n):
    T, C2 = a.shape
    tc = FFN_TILE

    def body(a_ref, cw_ref, cb_ref, u_ref):
        x = a_ref[...]
        prev, nxt = _neighbours(x, n)
        z = prev * cw_ref[0:1, :] + x * cw_ref[1:2, :] + nxt * cw_ref[2:3, :] + cb_ref[...]
        g, v = z[:, :tc], z[:, tc:]
        u_ref[...] = (g * jax.nn.sigmoid(g) * v).astype(u_ref.dtype)

    return _pcall(
        body, name='ffn_gate_fwd', grid=(C2 // (2 * tc),),
        in_specs=[pl.BlockSpec((T, 2 * tc), lambda j: (0, j)), pl.BlockSpec((3, 2 * tc), lambda j: (0, j)),
                  pl.BlockSpec((1, 2 * tc), lambda j: (0, j))],
        out_specs=pl.BlockSpec((T, tc), lambda j: (0, j)),
        out_shape=jax.ShapeDtypeStruct((T, C2 // 2), BF16),
        compiler_params=pltpu.CompilerParams(dimension_semantics=("parallel",), vmem_limit_bytes=VMEM_LIMIT),
    )(a, cw, cb)


def _gate_bwd_call(a, cw, cb, du, n):
    T, C2 = a.shape
    tc = FFN_TILE

    def body(a_ref, cw_ref, cb_ref, du_ref, da_ref, dw_ref):
        x = a_ref[...]
        w0, w1, w2 = cw_ref[0:1, :], cw_ref[1:2, :], cw_ref[2:3, :]
        prev, nxt = _neighbours(x, n)
        z = prev * w0 + x * w1 + nxt * w2 + cb_ref[...]
        g, v = z[:, :tc], z[:, tc:]
        sg = jax.nn.sigmoid(g)
        du = du_ref[...]
        dz = jnp.concatenate([du * v * (sg * (1.0 + g * (1.0 - sg))), du * (g * sg)], axis=1)
        t = lax.broadcasted_iota(jnp.int32, dz.shape, 0)
        from_next = pltpu.roll(jnp.where((t == 0) | (t == n), 0.0, dz), T - 1, 0)
        from_prev = pltpu.roll(jnp.where((t == n - 1) | (t == T - 1), 0.0, dz), 1, 0)
        da_ref[...] = dz * w1 + from_next * w0 + from_prev * w2
        dw_ref[0:1, :] = jnp.sum(dz * prev, axis=0, keepdims=True)
        dw_ref[1:2, :] = jnp.sum(dz * x, axis=0, keepdims=True)
        dw_ref[2:3, :] = jnp.sum(dz * nxt, axis=0, keepdims=True)
        dw_ref[3:4, :] = jnp.sum(dz, axis=0, keepdims=True)
        dw_ref[4:8, :] = jnp.zeros((4, 2 * tc), F32)

    return _pcall(
        body, name='ffn_gate_bwd', grid=(C2 // (2 * tc),),
        in_specs=[pl.BlockSpec((T, 2 * tc), lambda j: (0, j)), pl.BlockSpec((3, 2 * tc), lambda j: (0, j)),
                  pl.BlockSpec((1, 2 * tc), lambda j: (0, j)), pl.BlockSpec((T, tc), lambda j: (0, j))],
        out_specs=[pl.BlockSpec((T, 2 * tc), lambda j: (0, j)), pl.BlockSpec((8, 2 * tc), lambda j: (0, j))],
        out_shape=[jax.ShapeDtypeStruct((T, C2), F32), jax.ShapeDtypeStruct((8, C2), F32)],
        compiler_params=pltpu.CompilerParams(dimension_semantics=("parallel",), vmem_limit_bytes=VMEM_LIMIT),
    )(a, cw, cb, du)


def _ffn_tail_fwd_impl(a, cw, cb, w, next_shard, n):
    u16 = _gate_fwd_call(a, cw, cb.reshape(1, -1), n)
    y, gathered = _proj_fwd(u16, w, next_shard, 'ffn_down')
    return y, gathered, (a, cw, cb, u16, w)


def _ffn_tail_bwd_impl(res, dy, d_gathered, n):
    a, cw, cb, u16, w = res
    du, dw, d_shard = _proj_bwd(u16, w, dy, d_gathered, 'ffn_down')
    da, dcw = _gate_bwd_call(a, cw, cb.reshape(1, -1), du, n)
    return da, dcw[:3], dcw[3], dw, d_shard


@functools.partial(jax.custom_vjp, nondiff_argnums=(4,))
def ffn_tail(a, cw, cb, w, n):
    return _ffn_tail_fwd_impl(a, cw, cb, w, None, n)[0]


def _ffn_tail_fwd(a, cw, cb, w, n):
    y, _, res = _ffn_tail_fwd_impl(a, cw, cb, w, None, n)
    return y, res


def _ffn_tail_bwd(n, res, dy):
    return _ffn_tail_bwd_impl(res, dy, None, n)[:4]


ffn_tail.defvjp(_ffn_tail_fwd, _ffn_tail_bwd)


@functools.partial(jax.custom_vjp, nondiff_argnums=(5,))
def ffn_tail_carry(a, cw, cb, w, next_shard, n):
    y, gathered, _ = _ffn_tail_fwd_impl(a, cw, cb, w, next_shard, n)
    return y, gathered


def _ffn_tail_carry_fwd(a, cw, cb, w, next_shard, n):
    y, gathered, res = _ffn_tail_fwd_impl(a, cw, cb, w, next_shard, n)
    return (y, gathered), res


def _ffn_tail_carry_bwd(n, res, cts):
    return _ffn_tail_bwd_impl(res, cts[0], cts[1], n)


ffn_tail_carry.defvjp(_ffn_tail_carry_fwd, _ffn_tail_carry_bwd)


def _attn_fwd_call(q, k, v, scale):
    H, nq, dq = q.shape
    nk, dv = v.shape[1], v.shape[2]
    tq = _pick(nq, (256, 128))

    def body(q_ref, k_ref, v_ref, o_ref, lse_ref):
        s = lax.dot_general(q_ref[0], k_ref[0], (((1,), (1,)), ((), ())), preferred_element_type=F32) * scale
        m = jnp.max(s, axis=1, keepdims=True)
        p = jnp.exp(s - m)
        l = jnp.sum(p, axis=1, keepdims=True)
        pn = (p / l).astype(BF16)
        o_ref[...] = jnp.dot(pn, v_ref[0], preferred_element_type=F32)
        lse_ref[0] = m + jnp.log(l)

    return _pcall(
        body, name='mla_attn_fwd', grid=(H, nq // tq),
        in_specs=[pl.BlockSpec((1, tq, dq), lambda h, i: (h, i, 0)),
                  pl.BlockSpec((1, nk, dq), lambda h, i: (h, 0, 0)),
                  pl.BlockSpec((1, nk, dv), lambda h, i: (h, 0, 0))],
        out_specs=[pl.BlockSpec((tq, dv), lambda h, i: (i, h)),
                   pl.BlockSpec((1, tq, 1), lambda h, i: (h, i, 0))],
        out_shape=[jax.ShapeDtypeStruct((nq, H * dv), F32), jax.ShapeDtypeStruct((H, nq, 1), F32)],
        compiler_params=pltpu.CompilerParams(
            dimension_semantics=("parallel", "parallel"), vmem_limit_bytes=VMEM_LIMIT),
    )(q, k, v)


def _attn_bwd_call(q, k, v, o, lse, do, scale):
    H, nq, dq = q.shape
    nk, dv = v.shape[1], v.shape[2]
    tq = _pick(nq, (128,))

    def body(q_ref, k_ref, v_ref, o_ref, lse_ref, do_ref, dq_ref, dk_ref, dv_ref):
        @pl.when(pl.program_id(1) == 0)
        def _():
            dk_ref[...] = jnp.zeros_like(dk_ref)
            dv_ref[...] = jnp.zeros_like(dv_ref)

        q16, k16, v16 = q_ref[0], k_ref[0], v_ref[0]
        do = do_ref[...]
        do16 = do.astype(BF16)
        s = lax.dot_general(q16, k16, (((1,), (1,)), ((), ())), preferred_element_type=F32) * scale
        p = jnp.exp(s - lse_ref[0])
        dv_ref[0] += lax.dot_general(p.astype(BF16), do16, (((0,), (0,)), ((), ())), preferred_element_type=F32)
        dp = lax.dot_general(do16, v16, (((1,), (1,)), ((), ())), preferred_element_type=F32)
        delta = jnp.sum(do * o_ref[...], axis=1, keepdims=True)
        ds16 = (p * (dp - delta) * scale).astype(BF16)
        dq_ref[0] = jnp.dot(ds16, k16, preferred_element_type=F32)
        dk_ref[0] += lax.dot_general(ds16, q16, (((0,), (0,)), ((), ())), preferred_element_type=F32)

    return _pcall(
        body, name='mla_attn_bwd', grid=(H, nq // tq),
        in_specs=[pl.BlockSpec((1, tq, dq), lambda h, i: (h, i, 0)),
                  pl.BlockSpec((1, nk, dq), lambda h, i: (h, 0, 0)),
                  pl.BlockSpec((1, nk, dv), lambda h, i: (h, 0, 0)),
                  pl.BlockSpec((tq, dv), lambda h, i: (i, h)),
                  pl.BlockSpec((1, tq, 1), lambda h, i: (h, i, 0)),
                  pl.BlockSpec((tq, dv), lambda h, i: (i, h))],
        out_specs=[pl.BlockSpec((1, tq, dq), lambda h, i: (h, i, 0)),
                   pl.BlockSpec((1, nk, dq), lambda h, i: (h, 0, 0)),
                   pl.BlockSpec((1, nk, dv), lambda h, i: (h, 0, 0))],
        out_shape=[jax.ShapeDtypeStruct((H, nq, dq), F32), jax.ShapeDtypeStruct((H, nk, dq), F32),
                   jax.ShapeDtypeStruct((H, nk, dv), F32)],
        compiler_params=pltpu.CompilerParams(
            dimension_semantics=("parallel", "arbitrary"), vmem_limit_bytes=VMEM_LIMIT),
    )(q, k, v, o, lse, do)


@functools.partial(jax.custom_vjp, nondiff_argnums=(3,))
def attention(q, k, v, scale):
    return _attn_fwd_call(q.astype(BF16), k.astype(BF16), v.astype(BF16), scale)[0]


def _attention_fwd(q, k, v, scale):
    q16, k16, v16 = q.astype(BF16), k.astype(BF16), v.astype(BF16)
    o, lse = _attn_fwd_call(q16, k16, v16, scale)
    return o, (q16, k16, v16, o, lse)


def _attention_bwd(scale, res, do):
    q16, k16, v16, o, lse = res
    return tuple(_attn_bwd_call(q16, k16, v16, o, lse, do, scale))


attention.defvjp(_attention_fwd, _attention_bwd)


def _win_geometry(kind, n):
    if kind == 'na':
        rows = n // GRID_W
        kh = min(NA_KH, rows)

        def start(i):
            return jnp.clip(i - kh // 2, 0, rows - kh) * GRID_W

        def bidx(i):
            return jnp.clip(i - kh // 2, 0, rows - kh) - i + (NA_KH - 1)

        return GRID_W, kh * GRID_W, start, bidx
    nb = n // SWA_BLOCK

    def start(i):
        return i * SWA_BLOCK

    def bidx(i):
        return jnp.where(i == 0, 0, jnp.where(i == nb - 1, 2, 1))

    return SWA_BLOCK, 3 * SWA_BLOCK, start, bidx


def _dot_nt(a, b):
    return lax.dot_general(a, b, (((1,), (1,)), ((), ())), preferred_element_type=F32)


def _dot_tn(a, b):
    return lax.dot_general(a, b, (((0,), (0,)), ((), ())), preferred_element_type=F32)


def _win_specs(q, k, kc, bias):
    hq, n, d = q.shape
    grp = hq // k.shape[0]
    hb = bias.shape[0]
    return [
        pl.BlockSpec((1, n, d), lambda h: (h, 0, 0)),
        pl.BlockSpec((1,) + k.shape[1:], lambda h: (h // grp, 0, 0)),
        pl.BlockSpec((1,) + k.shape[1:], lambda h: (h // grp, 0, 0)),
        pl.BlockSpec((1,) + kc.shape[1:], lambda h: (h // grp, 0, 0)),
        pl.BlockSpec((1,) + kc.shape[1:], lambda h: (h // grp, 0, 0)),
        pl.BlockSpec((1,) + bias.shape[1:], (lambda h: (h, 0, 0, 0)) if hb > 1 else (lambda h: (0, 0, 0, 0))),
        pl.BlockSpec(memory_space=pltpu.SMEM),
    ]


def _win_fwd_call(q, k, v, kc, vc, bias, sink, kind):
    hq, n, d = q.shape
    scale = d ** -0.5
    qb, wk, start, bidx = _win_geometry(kind, n)
    has_sink = kind == 'swa'

    def body(q_ref, k_ref, v_ref, kc_ref, vc_ref, b_ref, sink_ref, o_ref, lse_ref):
        kc16, vc16 = kc_ref[0], vc_ref[0]
        snk = sink_ref[pl.program_id(0)]

        def step(i, carry):
            qs = pl.multiple_of(i * qb, qb)
            ks = pl.multiple_of(start(i), GRID_W)
            q16 = q_ref[0, pl.ds(qs, qb), :]
            s1 = _dot_nt(q16, k_ref[0, pl.ds(ks, wk), :]) * scale + b_ref[0, bidx(i)]
            s2 = _dot_nt(q16, kc16) * scale
            m = jnp.maximum(jnp.max(s1, axis=1, keepdims=True), jnp.max(s2, axis=1, keepdims=True))
            if has_sink:
                m = jnp.maximum(m, snk)
            p1 = jnp.exp(s1 - m)
            p2 = jnp.exp(s2 - m)
            l = jnp.sum(p1, axis=1, keepdims=True) + jnp.sum(p2, axis=1, keepdims=True)
            if has_sink:
                l = l + jnp.exp(snk - m)
            o = (jnp.dot((p1 / l).astype(BF16), v_ref[0, pl.ds(ks, wk), :], preferred_element_type=F32)
                 + jnp.dot((p2 / l).astype(BF16), vc16, preferred_element_type=F32))
            o_ref[0, pl.ds(qs, qb), :] = o
            lse_ref[0, pl.ds(qs, qb), :] = m + jnp.log(l)
            return carry

        lax.fori_loop(0, n // qb, step, 0)

    return _pcall(
        body, name=kind + '_attn_fwd', grid=(hq,), in_specs=_win_specs(q, k, kc, bias),
        out_specs=[pl.BlockSpec((1, n, d), lambda h: (h, 0, 0)), pl.BlockSpec((1, n, 1), lambda h: (h, 0, 0))],
        out_shape=[jax.ShapeDtypeStruct((hq, n, d), F32), jax.ShapeDtypeStruct((hq, n, 1), F32)],
        compiler_params=pltpu.CompilerParams(dimension_semantics=("parallel",), vmem_limit_bytes=VMEM_LIMIT),
    )(q, k, v, kc, vc, bias, sink)


def _win_bwd_call(q, k, v, kc, vc, bias, sink, o, lse, do, kind):
    hq, n, d = q.shape
    scale = d ** -0.5
    qb, wk, start, bidx = _win_geometry(kind, n)
    has_sink = kind == 'swa'
    bias_grad = kind == 'na'

    def body(q_ref, k_ref, v_ref, kc_ref, vc_ref, b_ref, sink_ref, o_ref, lse_ref, do_ref,
             dq_ref, dk_ref, dv_ref, dkc_ref, dvc_ref, db_ref, dsink_ref):
        kc16, vc16 = kc_ref[0], vc_ref[0]
        snk = sink_ref[pl.program_id(0)]
        dk_ref[...] = jnp.zeros_like(dk_ref)
        dv_ref[...] = jnp.zeros_like(dv_ref)
        dkc_ref[...] = jnp.zeros_like(dkc_ref)
        dvc_ref[...] = jnp.zeros_like(dvc_ref)
        db_ref[...] = jnp.zeros_like(db_ref)

        def step(i, dsink):
            qs = pl.multiple_of(i * qb, qb)
            ks = pl.multiple_of(start(i), GRID_W)
            q16 = q_ref[0, pl.ds(qs, qb), :]
            k16 = k_ref[0, pl.ds(ks, wk), :]
            v16 = v_ref[0, pl.ds(ks, wk), :]
            lse = lse_ref[0, pl.ds(qs, qb), :]
            do = do_ref[0, pl.ds(qs, qb), :]
            do16 = do.astype(BF16)
            p1 = jnp.exp(_dot_nt(q16, k16) * scale + b_ref[0, bidx(i)] - lse)
            p2 = jnp.exp(_dot_nt(q16, kc16) * scale - lse)
            delta = jnp.sum(do * o_ref[0, pl.ds(qs, qb), :], axis=1, keepdims=True)
            ds1 = p1 * (_dot_nt(do16, v16) - delta)
            ds2 = p2 * (_dot_nt(do16, vc16) - delta)
            if bias_grad:
                db_ref[0, bidx(i)] += ds1
            ds1 = (ds1 * scale).astype(BF16)
            ds2 = (ds2 * scale).astype(BF16)
            dq_ref[0, pl.ds(qs, qb), :] = (jnp.dot(ds1, k16, preferred_element_type=F32)
                                          + jnp.dot(ds2, kc16, preferred_element_type=F32))
            dk_ref[0, pl.ds(ks, wk), :] += _dot_tn(ds1, q16)
            dv_ref[0, pl.ds(ks, wk), :] += _dot_tn(p1.astype(BF16), do16)
            dkc_ref[0] += _dot_tn(ds2, q16)
            dvc_ref[0] += _dot_tn(p2.astype(BF16), do16)
            if has_sink:
                dsink = dsink - jnp.sum(jnp.exp(snk - lse) * delta)
            return dsink

        dsink = lax.fori_loop(0, n // qb, step, jnp.zeros((), F32))
        dsink_ref[...] = jnp.full(dsink_ref.shape, dsink, F32)

    per_head = lambda shape: pl.BlockSpec((1,) + shape[1:], lambda h: (h,) + (0,) * (len(shape) - 1))
    kq = (hq,) + k.shape[1:]
    cq = (hq,) + kc.shape[1:]
    bq = (hq,) + bias.shape[1:]
    in_specs = _win_specs(q, k, kc, bias) + [per_head(o.shape), per_head(lse.shape), per_head(do.shape)]
    out_shapes = [q.shape, kq, kq, cq, cq, bq, (hq, 8, 128)]
    return _pcall(
        body, name=kind + '_attn_bwd', grid=(hq,), in_specs=in_specs,
        out_specs=[per_head(s) for s in out_shapes],
        out_shape=[jax.ShapeDtypeStruct(s, F32) for s in out_shapes],
        compiler_params=pltpu.CompilerParams(dimension_semantics=("parallel",), vmem_limit_bytes=VMEM_LIMIT),
    )(q, k, v, kc, vc, bias, sink, o, lse, do)


@functools.partial(jax.custom_vjp, nondiff_argnums=(7,))
def win_attention(q, k, v, kc, vc, bias, sink, kind):
    b16 = lambda t: t.astype(BF16)
    return _win_fwd_call(b16(q), b16(k), b16(v), b16(kc), b16(vc), bias, sink, kind)[0]


def _win_attention_fwd(q, k, v, kc, vc, bias, sink, kind):
    res = tuple(t.astype(BF16) for t in (q, k, v, kc, vc)) + (bias, sink)
    o, lse = _win_fwd_call(*res, kind)
    return o, res + (o, lse)


def _win_attention_bwd(kind, res, do):
    q, k, v, kc, vc, bias, sink, o, lse = res
    dq, dk, dv, dkc, dvc, db, dsink = _win_bwd_call(q, k, v, kc, vc, bias, sink, o, lse, do, kind)
    hkv = k.shape[0]
    fold = lambda t: t.reshape((hkv, -1) + t.shape[1:]).sum(axis=1)
    if bias.shape[0] == 1:
        db = jnp.zeros_like(bias)
    return dq, fold(dk), fold(dv), fold(dkc), fold(dvc), db, dsink[:, 0, 0]


win_attention.defvjp(_win_attention_fwd, _win_attention_bwd)


def _loss_head(y, target):
    n, d = y.shape
    tr = _pick(n, (512, 256, 128))
    nb = n // tr

    def body(y_ref, t_ref, dy_ref, part_ref):
        err = y_ref[...] - t_ref[...]
        dy_ref[...] = err * (1.0 / d)
        part_ref[...] = jnp.full(part_ref.shape, jnp.sum(err * err), F32)

    dy, part = _pcall(
        body, name='loss_head', grid=(nb,),
        in_specs=[pl.BlockSpec((tr, d), lambda i: (i, 0)), pl.BlockSpec((tr, d), lambda i: (i, 0))],
        out_specs=[pl.BlockSpec((tr, d), lambda i: (i, 0)), pl.BlockSpec((1, 8, 128), lambda i: (i, 0, 0))],
        out_shape=[jax.ShapeDtypeStruct((n, d), F32), jax.ShapeDtypeStruct((nb, 8, 128), F32)],
        compiler_params=pltpu.CompilerParams(dimension_semantics=("parallel",), vmem_limit_bytes=VMEM_LIMIT),
    )(y, target)
    return 0.5 * jnp.sum(part[:, 0, 0]) / d, dy


def _sum_parts(parts, name):
    P, R, C = parts.shape
    tr = _pick(R, (256, 128, 64, 32, 16, 8))

    def body(p_ref, o_ref):
        acc = p_ref[0].astype(F32)
        for i in range(1, P):
            acc = acc + p_ref[i].astype(F32)
        o_ref[...] = acc

    return _pcall(
        body, name=name, grid=(R // tr,),
        in_specs=[pl.BlockSpec((P, tr, C), lambda i: (0, i, 0))],
        out_specs=pl.BlockSpec((tr, C), lambda i: (i, 0)),
        out_shape=jax.ShapeDtypeStruct((R, C), F32),
        compiler_params=pltpu.CompilerParams(dimension_semantics=("parallel",), vmem_limit_bytes=VMEM_LIMIT),
    )(parts)


def _adamw(parts, w, m, v, name):
    P, R, C = parts.shape
    tr = _pick(R, (128, 64, 32, 16, 8))
    c1 = 1.0 / (1.0 - ADAM_B1 ** ADAM_STEP)
    c2 = 1.0 / (1.0 - ADAM_B2 ** ADAM_STEP)

    def body(p_ref, w_ref, m_ref, v_ref, g_out, d_out, m_out, v_out):
        g = p_ref[0].astype(F32)
        for i in range(1, P):
            g = g + p_ref[i].astype(F32)
        m_new = ADAM_B1 * m_ref[...] + (1.0 - ADAM_B1) * g
        v_new = ADAM_B2 * v_ref[...] + (1.0 - ADAM_B2) * (g * g)
        g_out[...] = g
        m_out[...] = m_new
        v_out[...] = v_new
        d_out[...] = -ADAM_LR * ((m_new * c1) / (jnp.sqrt(v_new * c2) + ADAM_EPS) + ADAM_WD * w_ref[...])

    blk = pl.BlockSpec((tr, C), lambda i: (i, 0))
    return _pcall(
        body, name=name, grid=(R // tr,),
        in_specs=[pl.BlockSpec((P, tr, C), lambda i: (0, i, 0)), blk, blk, blk],
        out_specs=[blk, blk, blk, blk],
        out_shape=[jax.ShapeDtypeStruct((R, C), F32)] * 4,
        compiler_params=pltpu.CompilerParams(dimension_semantics=("parallel",), vmem_limit_bytes=VMEM_LIMIT),
    )(parts, w, m, v)


def rms_norm(x, g):
    return x * lax.rsqrt(jnp.mean(x * x, axis=-1, keepdims=True) + NORM_EPS) * g


def modulate(h, shift, scale):
    return h * (1.0 + scale) + shift


def axial_angles(n, d_rot):
    t = jnp.arange(n)
    row = (t // GRID_W).astype(F32)
    col = (t % GRID_W).astype(F32)
    d_axis = d_rot // 2
    inv_freq = ROPE_BASE ** (-jnp.arange(0, d_axis, 2, dtype=F32) / d_axis)
    return (row[:, None] * inv_freq, col[:, None] * inv_freq)


def rope_segment(x, ang):
    cos = jnp.cos(ang)[:, None, :]
    sin = jnp.sin(ang)[:, None, :]
    x1, x2 = jnp.split(x, 2, axis=-1)
    return jnp.concatenate([x1 * cos - x2 * sin, x2 * cos + x1 * sin], axis=-1)


def axial_rope(x, ang):
    half = x.shape[-1] // 2
    return jnp.concatenate([rope_segment(x[..., :half], ang[0]), rope_segment(x[..., half:], ang[1])], axis=-1)


def rope_latent(t, n, ang):
    return jnp.concatenate([axial_rope(t[:n], ang), t[n:]], axis=0)


def mla_attend_ctx(q_nope, q_rope, k_nope, k_rope, v):
    s = (jnp.einsum('qhd,khd->hqk', q_nope, k_nope, preferred_element_type=F32)
         + jnp.einsum('qhr,kr->hqk', q_rope, k_rope, preferred_element_type=F32))
    p = jax.nn.softmax(s * MLA_SCALE, axis=-1)
    return jnp.einsum('hqk,khd->qhd', p, v)


def pool_mixer(u, w_pool, scale):
    n = u.shape[0]
    csum = jnp.pad(jnp.cumsum(u, axis=0), ((1, 0), (0, 0)))
    t = jnp.arange(n)
    diffs = []
    for g, w in enumerate(POOL_WINDOWS):
        sl = slice(g * POOL_GROUP, (g + 1) * POOL_GROUP)
        lo = jnp.clip(t - w // 2, 0, n)
        hi = jnp.clip(t + w // 2, 0, n)
        cs = csum[:, sl]
        mean = (cs[hi] - cs[lo]) / (hi - lo).astype(F32)[:, None]
        diffs.append(mean - u[:, sl])
    d = jnp.stack(diffs, axis=1)
    y = jnp.einsum('ngc,gcd->ngd', d, w_pool).reshape(n, POOL_WIDTH)
    return y * scale


def swa_latent(q, k, v, k_ctx, v_ctx, sink):
    n, hq, d = q.shape
    blk = SWA_BLOCK
    a = jnp.arange(blk)[:, None]
    j = jnp.arange(3 * blk)[None, :]
    near = jnp.abs(j - blk - a) <= SWA_WINDOW
    tiles = jnp.stack([near & (j >= blk), near, near & (j < 2 * blk)])
    bias = jnp.where(tiles, 0.0, NEG_INF).astype(F32)[None]
    heads = lambda t: t.transpose(1, 0, 2)
    pad = lambda t: jnp.pad(heads(t), ((0, 0), (blk, blk), (0, 0)))
    o = win_attention(heads(q), pad(k), pad(v), heads(k_ctx), heads(v_ctx), bias, sink, 'swa')
    return o.transpose(1, 0, 2).reshape(n, hq * d)


def ctx_attention(q, k, v, sink):
    nq, hq, d = q.shape
    hkv = k.shape[1]
    grp = hq // hkv
    nk = k.shape[0]
    qg = q.reshape(nq, hkv, grp, d)
    s = jnp.einsum('qhgd,khd->hgqk', qg, k, preferred_element_type=F32) * (d ** -0.5)
    if sink is not None:
        s_sink = jnp.broadcast_to(sink.reshape(hkv, grp)[:, :, None, None], s.shape[:-1] + (1,))
        s = jnp.concatenate([s, s_sink], axis=-1)
    p = jax.nn.softmax(s, axis=-1)[..., :nk]
    o = jnp.einsum('hgqk,khd->qhgd', p, v)
    return o.reshape(nq, hq * d)


def na_bias_tiles(rpb, n):
    rows = n // GRID_W
    kh = min(NA_KH, rows)
    qc = jnp.arange(GRID_W)[:, None]
    kc = jnp.arange(GRID_W)[None, :]
    dc = jnp.clip(kc - qc, 1 - NA_KW, NA_KW - 1) + (NA_KW - 1)
    onehot = (dc[None] == jnp.arange(2 * NA_KW - 1)[:, None, None]).astype(F32)
    toeplitz = jnp.einsum('hdt,tqk->hdqk', rpb, onehot, precision=lax.Precision.HIGHEST)
    q_col0 = jnp.clip(qc - NA_KW // 2, 0, GRID_W - NA_KW)
    valid = (kc >= q_col0) & (kc < q_col0 + NA_KW)
    masked = jnp.where(valid, toeplitz, NEG_INF)
    return jnp.stack([jnp.concatenate([masked[:, off + j] for j in range(kh)], axis=-1) for off in range(NA_KH)], axis=1)


def na_latent(q, k, v, k_ctx, v_ctx, rpb):
    n, h, d = q.shape
    heads = lambda t: t.transpose(1, 0, 2)
    o = win_attention(heads(q), heads(k), heads(v), heads(k_ctx), heads(v_ctx), na_bias_tiles(rpb, n),
                      jnp.zeros((h,), F32), 'na')
    return o.transpose(1, 0, 2).reshape(n, h * d)


def _interleave(t):
    lead, f = t.shape[:-1], t.shape[-1] // 2
    return t.reshape(lead + (2, f // FFN_TILE, FFN_TILE)).swapaxes(-3, -2).reshape(lead + (2 * f,))


def _assemble(g, name):
    _, r, c = g.shape
    w = g.reshape(N_DEV * r, c) if BIG[name] == 1 else g.transpose(1, 0, 2).reshape(r, N_DEV * c)
    if name == 'w_in':
        w = jnp.pad(w, ((0, 0), (0, IN_COLS_PAD - IN_COLS)))
    if name == 'ffn_w_up':
        w = _interleave(w)
    return w


def _forward(x, mod_x, mod_c, gathered0, shards, ws, ctx):
    n = x.shape[0]
    depth = mod_x.shape[0]
    ang_mla = axial_angles(n, MLA_ROPE)
    ang_swa = axial_angles(n, SWA_HEAD_DIM)
    gathered = gathered0
    for l in range(depth):
        update_ctx = l < depth - 1
        w = {name: _assemble(gathered[name], name) for name in BIG}
        gathered = {}

        def proj(a, name):
            if not update_ctx:
                return pmm(a, w[name], name)
            y, gathered[name] = pmm_carry(a, w[name], shards[name][l + 1], name)
            return y

        sh_m, sc_m, gt_m, sh_f, sc_f, gt_f = jnp.split(mod_x[l], 6)
        csh_m, csc_m, cgt_m, csh_f, csc_f, cgt_f = jnp.split(mod_c[l], 6)

        h_all = jnp.concatenate([modulate(rms_norm(x, ws['g_mix'][l]), sh_m, sc_m),
                                 modulate(rms_norm(ctx, ws['g_mix'][l]), csh_m, csc_m)], axis=0)
        p_all = proj(h_all, 'w_in')[:, :IN_COLS]
        mla_p, pool_p, swa_p, na_p = jnp.split(p_all, IN_SPLITS, axis=-1)
        T = p_all.shape[0]

        cq, ckv, kr = jnp.split(mla_p, [MLA_Q_LORA, MLA_Q_LORA + MLA_KV_LORA], axis=-1)
        q = proj(rms_norm(cq, ws['mla_q_a_norm'][l]), 'mla_w_qb').reshape(T, MLA_HEADS, MLA_NOPE + MLA_ROPE)
        kv = proj(rms_norm(ckv, ws['mla_kv_a_norm'][l]), 'mla_w_kvb').reshape(T, MLA_HEADS, MLA_NOPE + MLA_V)
        q_nope = rms_norm(q[..., :MLA_NOPE], ws['mla_q_nope_norm'][l])
        q_rope = rope_latent(rms_norm(q[..., MLA_NOPE:], ws['mla_q_rope_norm'][l]), n, ang_mla)
        k_nope = rms_norm(kv[..., :MLA_NOPE], ws['mla_k_nope_norm'][l])
        v_mla = kv[..., MLA_NOPE:]
        k_rope = rope_latent(rms_norm(kr, ws['mla_k_rope_norm'][l])[:, None, :], n, ang_mla)
        q_cat = jnp.concatenate([q_nope, q_rope], axis=-1).transpose(1, 0, 2)
        k_cat = jnp.concatenate([k_nope, jnp.broadcast_to(k_rope, (T, MLA_HEADS, MLA_ROPE))], axis=-1).transpose(1, 0, 2)
        out_a = attention(q_cat[:, :n], k_cat, v_mla.transpose(1, 0, 2), MLA_SCALE)

        out_b = pool_mixer(pool_p[:n], ws['pool_w'][l], ws['pool_scale'][l])

        sq, sk, sv = jnp.split(swa_p, [SWA_HEADS * SWA_HEAD_DIM, (SWA_HEADS + SWA_KV_HEADS) * SWA_HEAD_DIM], axis=-1)
        sq = rope_latent(rms_norm(sq.reshape(T, SWA_HEADS, SWA_HEAD_DIM), ws['swa_q_norm'][l]), n, ang_swa)
        sk = rope_latent(rms_norm(sk.reshape(T, SWA_KV_HEADS, SWA_HEAD_DIM), ws['swa_k_norm'][l]), n, ang_swa)
        sv = sv.reshape(T, SWA_KV_HEADS, SWA_HEAD_DIM)
        out_c = swa_latent(sq[:n], sk[:n], sv[:n], sk[n:], sv[n:], ws['swa_sink'][l])

        nq_, nk_, nv_ = jnp.split(na_p, 3, axis=-1)
        nq_ = rms_norm(nq_.reshape(T, NA_HEADS, NA_HEAD_DIM), ws['na_q_norm'][l])
        nk_ = rms_norm(nk_.reshape(T, NA_HEADS, NA_HEAD_DIM), ws['na_k_norm'][l])
        nv_ = nv_.reshape(T, NA_HEADS, NA_HEAD_DIM)
        out_d = na_latent(nq_[:n], nk_[:n], nv_[:n], nk_[n:], nv_[n:], ws['na_rpb'][l])

        mix_x = jnp.concatenate([out_a, out_b, out_c, out_d], axis=-1)
        conv_w = _interleave(ws['ffn_conv_w'][l])
        conv_b = _interleave(ws['ffn_conv_b'][l])
        if update_ctx:
            L = T - n
            mix_c = jnp.concatenate([
                mla_attend_ctx(q_nope[n:], q_rope[n:], k_nope[n:], k_rope[n:, 0], v_mla[n:]).reshape(L, MLA_HEADS * MLA_V),
                pool_mixer(pool_p[n:], ws['pool_w'][l], ws['pool_scale'][l]),
                ctx_attention(sq[n:], sk[n:], sv[n:], ws['swa_sink'][l]),
                ctx_attention(nq_[n:], nk_[n:], nv_[n:], None),
            ], axis=-1)
            o_all = proj(jnp.concatenate([mix_x, mix_c], axis=0), 'w_out')
            x = x + gt_m * o_all[:n]
            ctx = ctx + cgt_m * o_all[n:]
            h2 = jnp.concatenate([modulate(rms_norm(x, ws['g_ffn'][l]), sh_f, sc_f),
                                  modulate(rms_norm(ctx, ws['g_ffn'][l]), csh_f, csc_f)], axis=0)
            f_all, gathered['ffn_w_down'] = ffn_tail_carry(
                proj(h2, 'ffn_w_up'), conv_w, conv_b, w['ffn_w_down'], shards['ffn_w_down'][l + 1], n)
            x = x + gt_f * f_all[:n]
            ctx = ctx + cgt_f * f_all[n:]
        else:
            x = x + gt_m * proj(mix_x, 'w_out')
            h2 = modulate(rms_norm(x, ws['g_ffn'][l]), sh_f, sc_f)
            x = x + gt_f * ffn_tail(proj(h2, 'ffn_w_up'), conv_w, conv_b, w['ffn_w_down'], n)
    return x


def _pack_rows(vecs):
    flat = jnp.concatenate([v.reshape(-1).astype(F32) for v in vecs])
    pad = (-flat.shape[0]) % 1024
    return jnp.pad(flat, (0, pad)).reshape(-1, 128)


def _unpack(flat, shapes):
    out, off = [], 0
    for s in shapes:
        size = 1
        for d in s:
            size *= d
        out.append(flat[off:off + size].reshape(s))
        off += size
    return out


def _silu_grad(z):
    s = jax.nn.sigmoid(z)
    return s * (1.0 + z * (1.0 - s))


def kernel(x, c, ctx, c_ctx, w_mod, b_mod, g_mix, g_ffn, w_in, w_out, mla_q_a_norm, mla_w_qb, mla_kv_a_norm, mla_w_kvb, mla_q_nope_norm, mla_q_rope_norm, mla_k_nope_norm, mla_k_rope_norm, pool_w, pool_scale, swa_q_norm, swa_k_norm, swa_sink, na_q_norm, na_k_norm, na_rpb, ffn_w_up, ffn_conv_w, ffn_conv_b, ffn_w_down, loss_target, m_c_ctx, m_w_mod, m_b_mod, m_g_mix, m_g_ffn, m_w_in, m_w_out, m_mla_q_a_norm, m_mla_w_qb, m_mla_kv_a_norm, m_mla_w_kvb, m_mla_q_nope_norm, m_mla_q_rope_norm, m_mla_k_nope_norm, m_mla_k_rope_norm, m_pool_w, m_pool_scale, m_swa_q_norm, m_swa_k_norm, m_swa_sink, m_na_q_norm, m_na_k_norm, m_na_rpb, m_ffn_w_up, m_ffn_conv_w, m_ffn_conv_b, m_ffn_w_down, v_c_ctx, v_w_mod, v_b_mod, v_g_mix, v_g_ffn, v_w_in, v_w_out, v_mla_q_a_norm, v_mla_w_qb, v_mla_kv_a_norm, v_mla_w_kvb, v_mla_q_nope_norm, v_mla_q_rope_norm, v_mla_k_nope_norm, v_mla_k_rope_norm, v_pool_w, v_pool_scale, v_swa_q_norm, v_swa_k_norm, v_swa_sink, v_na_q_norm, v_na_k_norm, v_na_rpb, v_ffn_w_up, v_ffn_conv_w, v_ffn_conv_b, v_ffn_w_down):
    return _step(x, c, ctx, c_ctx, w_mod, b_mod, g_mix, g_ffn, w_in, w_out, mla_q_a_norm, mla_w_qb, mla_kv_a_norm, mla_w_kvb, mla_q_nope_norm, mla_q_rope_norm, mla_k_nope_norm, mla_k_rope_norm, pool_w, pool_scale, swa_q_norm, swa_k_norm, swa_sink, na_q_norm, na_k_norm, na_rpb, ffn_w_up, ffn_conv_w, ffn_conv_b, ffn_w_down, loss_target, m_c_ctx, m_w_mod, m_b_mod, m_g_mix, m_g_ffn, m_w_in, m_w_out, m_mla_q_a_norm, m_mla_w_qb, m_mla_kv_a_norm, m_mla_w_kvb, m_mla_q_nope_norm, m_mla_q_rope_norm, m_mla_k_nope_norm, m_mla_k_rope_norm, m_pool_w, m_pool_scale, m_swa_q_norm, m_swa_k_norm, m_swa_sink, m_na_q_norm, m_na_k_norm, m_na_rpb, m_ffn_w_up, m_ffn_conv_w, m_ffn_conv_b, m_ffn_w_down, v_c_ctx, v_w_mod, v_b_mod, v_g_mix, v_g_ffn, v_w_in, v_w_out, v_mla_q_a_norm, v_mla_w_qb, v_mla_kv_a_norm, v_mla_w_kvb, v_mla_q_nope_norm, v_mla_q_rope_norm, v_mla_k_nope_norm, v_mla_k_rope_norm, v_pool_w, v_pool_scale, v_swa_q_norm, v_swa_k_norm, v_swa_sink, v_na_q_norm, v_na_k_norm, v_na_rpb, v_ffn_w_up, v_ffn_conv_w, v_ffn_conv_b, v_ffn_w_down)


def _step(*args):
    n_in = len(ARG_NAMES)
    n_w = len(WEIGHTS)
    given = dict(zip(ARG_NAMES, args[:n_in]))
    mom = dict(zip(WEIGHTS, args[n_in:n_in + n_w]))
    var = dict(zip(WEIGHTS, args[n_in + n_w:n_in + 2 * n_w]))
    me = _my_index()

    x = given['x'][0]
    ctx = given['ctx'][0]
    target = given['loss_target'][0]
    n, D = x.shape
    depth = given['w_mod'].shape[0]
    mod_cols = given['w_mod'].shape[2]
    conv_cols = given['ffn_conv_w'].shape[2]

    gathered0 = {name: _all_gather(given[name][0].astype(BF16), 'ag_' + name) for name in BIG}
    misc = _all_gather(_pack_rows([given['c'], given['ffn_conv_w']]), 'ag_cond')
    misc = misc.reshape(N_DEV, -1)
    c_all = misc[:, :D]
    conv_w = misc[:, D:D + depth * 3 * conv_cols].reshape(N_DEV, depth, 3, conv_cols)
    conv_w = conv_w.transpose(1, 2, 0, 3).reshape(depth, 3, N_DEV * conv_cols)

    cond = jnp.concatenate([c_all, given['c_ctx'][None], jnp.zeros((16 - N_DEV - 1, D), F32)], axis=0)
    s16 = jax.nn.silu(cond).astype(BF16)
    wm16 = given['w_mod'].astype(BF16)
    b_loc = lax.dynamic_slice_in_dim(given['b_mod'], me * mod_cols, mod_cols, axis=1)
    mod_part = jnp.stack([_mm(s16, wm16[l], 'nn', F32, 'mod_fwd') + b_loc[l] for l in range(depth)])
    mod_all = _all_gather(mod_part, 'ag_mod').transpose(1, 2, 0, 3).reshape(depth, 16, N_DEV * mod_cols)
    mod_x = lax.dynamic_index_in_dim(mod_all, me, axis=1, keepdims=False)
    mod_c = mod_all[:, N_DEV]

    ws = {name: given[name] for name in SMALL if name not in ('c_ctx', 'b_mod')}
    ws['ffn_conv_w'] = conv_w
    shards = {name: (None,) + tuple(given[name][l] for l in range(1, depth)) for name in BIG}
    y, vjp = jax.vjp(lambda *d: _forward(*d, ctx), x, mod_x, mod_c, gathered0, shards, ws)
    loss_local, dy = _loss_head(y, target)
    g_x, g_mod_x, g_mod_c, g_gathered0, g_shards, g_ws = vjp(dy)

    g_big = {}
    for name in BIG:
        parts = _all_to_all(g_gathered0[name], 'a2a_' + name)
        g0 = _sum_parts(parts, 'sum_' + name)
        g_big[name] = jnp.stack((g0,) + tuple(g_shards[name][1:]))

    g_mod = jnp.zeros((depth, 16, N_DEV * mod_cols), F32)
    g_mod = lax.dynamic_update_slice_in_dim(g_mod, g_mod_x[:, None, :], me, axis=1)
    g_mod = g_mod.at[:, N_DEV].set(g_mod_c)
    g_mod_parts = g_mod.reshape(depth, 16, N_DEV, mod_cols).transpose(2, 0, 1, 3).reshape(N_DEV, -1)
    g_conv_parts = g_ws['ffn_conv_w'].reshape(depth, 3, N_DEV, conv_cols).transpose(2, 0, 1, 3).reshape(N_DEV, -1)
    n_mod = depth * 16 * mod_cols
    n_conv = depth * 3 * conv_cols
    f32_parts = jnp.concatenate([g_mod_parts, g_conv_parts], axis=1)
    f32_pad = (-f32_parts.shape[1]) % 1024
    f32_parts = jnp.pad(f32_parts, ((0, 0), (0, f32_pad))).reshape(N_DEV, -1, 128)
    f32_parts = _all_to_all(f32_parts, 'a2a_f32')
    f32_sum = _sum_parts(f32_parts, 'sum_f32').reshape(-1)
    g_mod_loc = f32_sum[:n_mod].reshape(depth, 16, mod_cols)
    g_conv_loc = f32_sum[n_mod:n_mod + n_conv].reshape(depth * 3, conv_cols)

    g_mod16 = g_mod_loc.astype(BF16)
    g_w_mod = jnp.stack([_mm(s16, g_mod16[l], 'tn', F32, 'mod_dw') for l in range(depth)])
    d_silu = sum(_mm(g_mod16[l], wm16[l], 'nt', F32, 'mod_dc') for l in range(depth))
    g_c_ctx_part = d_silu[N_DEV] * _silu_grad(given['c_ctx'])

    small_grads = {name: g_ws[name] for name in SMALL if name not in ('c_ctx', 'b_mod')}
    small_grads['c_ctx'] = g_c_ctx_part
    small_grads['b_mod'] = g_mod_x + g_mod_c
    small_shapes = [given[name].shape for name in SMALL]
    n_small = sum(int(given[name].size) for name in SMALL)
    packed = _pack_rows([small_grads[name] for name in SMALL] + [loss_local.reshape(1)])
    small_parts = _all_gather(packed, 'ag_small')

    out_g, out_d, out_m, out_v = {}, {}, {}, {}

    def update(name, parts):
        shape = given[name].shape
        flat = (shape[0] * shape[1], shape[2])
        res = _adamw(parts.reshape((-1,) + flat), given[name].reshape(flat), mom[name].reshape(flat),
                     var[name].reshape(flat), 'adamw_' + name)
        out_g[name], out_d[name], out_m[name], out_v[name] = (r.reshape(shape) for r in res)

    for name in BIG:
        update(name, g_big[name])
    update('w_mod', g_w_mod)
    update('ffn_conv_w', g_conv_loc)

    zero1 = jnp.zeros((1,), F32)
    res = _adamw(small_parts, _pack_rows([given[k] for k in SMALL] + [zero1]), _pack_rows([mom[k] for k in SMALL] + [zero1]),
                 _pack_rows([var[k] for k in SMALL] + [zero1 + 1.0]), 'adamw_small')
    flats = [r.reshape(-1) for r in res]
    for name, g_, d_, m_, v_ in zip(SMALL, *[_unpack(f, small_shapes) for f in flats]):
        out_g[name], out_d[name], out_m[name], out_v[name] = g_, d_, m_, v_
    loss = flats[0][n_small]

    return (loss, g_x[None], *[out_g[k] for k in WEIGHTS], *[out_d[k] for k in WEIGHTS],
            *[out_m[k] for k in WEIGHTS], *[out_v[k] for k in WEIGHTS])
```

```python
import functools

import jax
import jax.numpy as jnp
from jax import lax
from jax.experimental import pallas as pl
from jax.experimental.pallas import tpu as pltpu

F32 = jnp.float32
BF16 = jnp.bfloat16
N_DEV = 8
MESH_ID = pl.DeviceIdType.MESH

GRID_W = 64
ROPE_BASE = 10000.0
NORM_EPS = 1e-6
NEG_INF = -1e30

MLA_HEADS = 4
MLA_NOPE = 128
MLA_ROPE = 64
MLA_V = 128
MLA_Q_LORA = 512
MLA_KV_LORA = 256
MLA_SCALE = (MLA_NOPE + MLA_ROPE) ** -0.5
POOL_WINDOWS = (2, 4, 8, 16)
POOL_GROUP = 128
POOL_WIDTH = POOL_GROUP * len(POOL_WINDOWS)
SWA_HEADS = 8
SWA_KV_HEADS = 2
SWA_HEAD_DIM = 64
SWA_WINDOW = 128
SWA_BLOCK = 128
NA_HEADS = 8
NA_HEAD_DIM = 64
NA_KH = 8
NA_KW = 16
NA_QC = 16
NA_KC = NA_QC + NA_KW

A_COLS = MLA_Q_LORA + MLA_KV_LORA + MLA_ROPE
B_COLS = POOL_WIDTH
C_COLS = (SWA_HEADS + 2 * SWA_KV_HEADS) * SWA_HEAD_DIM
D_COLS = 3 * NA_HEADS * NA_HEAD_DIM
IN_COLS = A_COLS + B_COLS + C_COLS + D_COLS
IN_COLS_PAD = 3840
IN_SPLITS = (A_COLS, A_COLS + B_COLS, A_COLS + B_COLS + C_COLS)

ADAM_LR = 0.001
ADAM_B1 = 0.9
ADAM_B2 = 0.999
ADAM_EPS = 1e-08
ADAM_WD = 0.01
ADAM_STEP = 10

VMEM_LIMIT = 48 << 20

ARG_NAMES = ['x', 'c', 'ctx', 'c_ctx', 'w_mod', 'b_mod', 'g_mix', 'g_ffn', 'w_in', 'w_out', 'mla_q_a_norm', 'mla_w_qb', 'mla_kv_a_norm', 'mla_w_kvb', 'mla_q_nope_norm', 'mla_q_rope_norm', 'mla_k_nope_norm', 'mla_k_rope_norm', 'pool_w', 'pool_scale', 'swa_q_norm', 'swa_k_norm', 'swa_sink', 'na_q_norm', 'na_k_norm', 'na_rpb', 'ffn_w_up', 'ffn_conv_w', 'ffn_conv_b', 'ffn_w_down', 'loss_target']
WEIGHTS = ['c_ctx', 'w_mod', 'b_mod', 'g_mix', 'g_ffn', 'w_in', 'w_out', 'mla_q_a_norm', 'mla_w_qb', 'mla_kv_a_norm', 'mla_w_kvb', 'mla_q_nope_norm', 'mla_q_rope_norm', 'mla_k_nope_norm', 'mla_k_rope_norm', 'pool_w', 'pool_scale', 'swa_q_norm', 'swa_k_norm', 'swa_sink', 'na_q_norm', 'na_k_norm', 'na_rpb', 'ffn_w_up', 'ffn_conv_w', 'ffn_conv_b', 'ffn_w_down']
BIG = {'w_in': 2, 'w_out': 1, 'mla_w_qb': 2, 'mla_w_kvb': 2, 'ffn_w_up': 2, 'ffn_w_down': 1}
SMALL = ['c_ctx', 'b_mod', 'g_mix', 'g_ffn', 'mla_q_a_norm', 'mla_kv_a_norm', 'mla_q_nope_norm', 'mla_q_rope_norm', 'mla_k_nope_norm', 'mla_k_rope_norm', 'pool_w', 'pool_scale', 'swa_q_norm', 'swa_k_norm', 'swa_sink', 'na_q_norm', 'na_k_norm', 'na_rpb', 'ffn_conv_b']


def _pcall(body, **kw):
    return pl.pallas_call(body, **kw)


def _my_index():
    return 4 * lax.axis_index("x") + 2 * lax.axis_index("y") + lax.axis_index("c")


_COMM_SCRATCH = [pltpu.SemaphoreType.DMA((7,)), pltpu.SemaphoreType.DMA((7,)), pltpu.SemaphoreType.DMA(())]
_ANY = pl.BlockSpec(memory_space=pl.ANY)


def _gather_copies(x_ref, out_ref, send_sems, recv_sems, local_sem):
    x, y, c = lax.axis_index("x"), lax.axis_index("y"), lax.axis_index("c")
    me, sibling = (x, y, c), (x, y, 1 - c)
    chips = [(1 - x, y), (x, 1 - y), (1 - x, 1 - y)]

    def slot(px, py, pc):
        return out_ref.at[4 * px + 2 * py + pc]

    def copy(k, blk, to, src=None):
        return pltpu.make_async_remote_copy(
            src_ref=slot(*blk) if src is None else src, dst_ref=slot(*blk),
            send_sem=send_sems.at[k], recv_sem=recv_sems.at[k], device_id=to, device_id_type=MESH_ID)

    mine = pltpu.make_async_copy(x_ref, slot(*me), local_sem)
    first = [copy(0, me, sibling, src=x_ref)] + [copy(1 + j, me, (*chip, c), src=x_ref) for j, chip in enumerate(chips)]

    def start():
        mine.start()
        for cp in first:
            cp.start()

    def finish():
        passed = [copy(4 + j, (*chip, c), sibling) for j, chip in enumerate(chips)]
        for j, chip in enumerate(chips):
            copy(1 + j, (*chip, c), me).wait_recv()
            passed[j].start()
        copy(0, sibling, me).wait_recv()
        for j, chip in enumerate(chips):
            copy(4 + j, (*chip, 1 - c), me).wait_recv()
        for cp in first + passed:
            cp.wait_send()
        mine.wait()

    return start, finish


def _exchange_copies(t_ref, out_ref, send_sems, recv_sems, local_sem):
    x, y, c = lax.axis_index("x"), lax.axis_index("y"), lax.axis_index("c")
    me = 4 * x + 2 * y + c

    def peer(k):
        return (1 - x if k & 4 else x), (1 - y if k & 2 else y), (1 - c if k & 1 else c)

    def copy(k, landed):
        px, py, pc = peer(k)
        p = 4 * px + 2 * py + pc
        return pltpu.make_async_remote_copy(
            src_ref=t_ref.at[p], dst_ref=out_ref.at[p if landed else me],
            send_sem=send_sems.at[k - 1], recv_sem=recv_sems.at[k - 1], device_id=(px, py, pc), device_id_type=MESH_ID)

    mine = pltpu.make_async_copy(t_ref.at[me], out_ref.at[me], local_sem)
    sends = [copy(k, False) for k in range(1, N_DEV)]

    def start():
        mine.start()
        for cp in sends:
            cp.start()

    def finish():
        for k in range(1, N_DEV):
            copy(k, True).wait_recv()
        for cp in sends:
            cp.wait_send()
        mine.wait()

    return start, finish


_COMM = {'gather': _gather_copies, 'exchange': _exchange_copies}


def _comm_out_shape(kind, operand):
    shape = (N_DEV,) + operand.shape if kind == 'gather' else operand.shape
    return jax.ShapeDtypeStruct(shape, operand.dtype)


def _comm_call(kind, operand, name):
    def body(x_ref, out_ref, send_sems, recv_sems, local_sem):
        start, finish = _COMM[kind](x_ref, out_ref, send_sems, recv_sems, local_sem)
        start()
        finish()

    return _pcall(body, name=name, out_shape=_comm_out_shape(kind, operand), in_specs=[_ANY], out_specs=_ANY,
                  scratch_shapes=_COMM_SCRATCH)(operand)


def _all_gather(block, name):
    return _comm_call('gather', block, name)


def _all_to_all(parts, name):
    return _comm_call('exchange', parts, name)


_LANE_TILES = (1024, 768, 512, 384, 256, 128)
_ROW_TILES = (1088, 1024, 512, 256, 128)


def _pick(dim, cands):
    for t in cands:
        if dim % t == 0:
            return t
    return dim


def _mm(a, b, mode, out_dtype, name, comm=None):
    if mode == 'nn':
        (M, K), (_, N) = a.shape, b.shape
        tm, tn, tk = _pick(M, _ROW_TILES), _pick(N, _LANE_TILES), _pick(K, _LANE_TILES)
        a_spec = pl.BlockSpec((tm, tk), lambda i, j, k: (i, k))
        b_spec = pl.BlockSpec((tk, tn), lambda i, j, k: (k, j))
        dn = (((1,), (0,)), ((), ()))
    elif mode == 'nt':
        (M, K), (N, _) = a.shape, b.shape
        tm, tn, tk = _pick(M, _ROW_TILES), _pick(N, _LANE_TILES), _pick(K, _LANE_TILES)
        a_spec = pl.BlockSpec((tm, tk), lambda i, j, k: (i, k))
        b_spec = pl.BlockSpec((tn, tk), lambda i, j, k: (j, k))
        dn = (((1,), (1,)), ((), ()))
    else:
        (K, M), (_, N) = a.shape, b.shape
        tm, tn, tk = _pick(M, _LANE_TILES), _pick(N, _LANE_TILES), _pick(K, _ROW_TILES)
        a_spec = pl.BlockSpec((tk, tm), lambda i, j, k: (k, i))
        b_spec = pl.BlockSpec((tk, tn), lambda i, j, k: (k, j))
        dn = (((0,), (0,)), ((), ()))
    grid = (M // tm, N // tn, K // tk)

    def matmul_step(a_ref, b_ref, o_ref, acc):
        @pl.when(pl.program_id(2) == 0)
        def _():
            acc[...] = jnp.zeros_like(acc)

        acc[...] += lax.dot_general(a_ref[...], b_ref[...], dn, preferred_element_type=F32)

        @pl.when(pl.program_id(2) == grid[2] - 1)
        def _():
            o_ref[...] = acc[...].astype(o_ref.dtype)

    o_spec = pl.BlockSpec((tm, tn), lambda i, j, k: (i, j))
    o_shape = jax.ShapeDtypeStruct((M, N), out_dtype)
    acc = pltpu.VMEM((tm, tn), F32)
    if comm is None:
        return _pcall(
            matmul_step, name=name, grid=grid, in_specs=[a_spec, b_spec], out_specs=o_spec, out_shape=o_shape,
            scratch_shapes=[acc],
            compiler_params=pltpu.CompilerParams(
                dimension_semantics=("parallel", "parallel", "arbitrary"), vmem_limit_bytes=VMEM_LIMIT),
        )(a, b)

    kind, operand = comm

    def body(a_ref, b_ref, x_ref, o_ref, out_ref, acc_ref, send_sems, recv_sems, local_sem):
        start, finish = _COMM[kind](x_ref, out_ref, send_sems, recv_sems, local_sem)
        step = (pl.program_id(0) * grid[1] + pl.program_id(1)) * grid[2] + pl.program_id(2)

        @pl.when(step == 0)
        def _():
            start()

        matmul_step(a_ref, b_ref, o_ref, acc_ref)

        @pl.when(step == grid[0] * grid[1] * grid[2] - 1)
        def _():
            finish()

    return _pcall(
        body, name=name, grid=grid, in_specs=[a_spec, b_spec, _ANY], out_specs=[o_spec, _ANY],
        out_shape=[o_shape, _comm_out_shape(kind, operand)], scratch_shapes=[acc] + _COMM_SCRATCH,
        compiler_params=pltpu.CompilerParams(
            dimension_semantics=("arbitrary", "arbitrary", "arbitrary"), vmem_limit_bytes=VMEM_LIMIT),
    )(a, b, operand)


def _proj_fwd(a16, w, next_shard, name):
    if next_shard is None:
        return _mm(a16, w, 'nn', F32, name + '_nn'), None
    return _mm(a16, w, 'nn', F32, name + '_nn_gather', comm=('gather', next_shard.astype(BF16)))


def _proj_bwd(a16, w, dy, d_gathered, name):
    dy16 = dy.astype(BF16)
    if d_gathered is None:
        return _mm(dy16, w, 'nt', F32, name + '_nt'), _mm(a16, dy16, 'tn', BF16, name + '_tn'), None
    half = d_gathered.shape[1] // 2
    dw, top = _mm(a16, dy16, 'tn', BF16, name + '_tn_exchange', comm=('exchange', d_gathered[:, :half]))
    da, low = _mm(dy16, w, 'nt', F32, name + '_nt_exchange', comm=('exchange', d_gathered[:, half:]))
    d_shard = jnp.concatenate([_sum_parts(top, name + '_sum_top'), _sum_parts(low, name + '_sum_low')], axis=0)
    return da, dw, d_shard


@functools.partial(jax.custom_vjp, nondiff_argnums=(3,))
def pmm(a, w, next_shard, name):
    return _proj_fwd(a.astype(BF16), w, next_shard, name)


def _pmm_fwd(a, w, next_shard, name):
    a16 = a.astype(BF16)
    return _proj_fwd(a16, w, next_shard, name), (a16, w)


def _pmm_bwd(name, res, cts):
    return _proj_bwd(*res, cts[0], cts[1], name)


pmm.defvjp(_pmm_fwd, _pmm_bwd)


FFN_TILE = 128


def _neighbours(x, n):
    T = x.shape[0]
    t = lax.broadcasted_iota(jnp.int32, x.shape, 0)
    prev = jnp.where((t == 0) | (t == n), 0.0, pltpu.roll(x, 1, 0))
    nxt = jnp.where((t == n - 1) | (t == T - 1), 0.0, pltpu.roll(x, T - 1, 0))
    return prev, nxt


def _gate_specs(rows, f):
    tiles = f // FFN_TILE
    return [pl.BlockSpec((rows, FFN_TILE), lambda j: (0, j)), pl.BlockSpec((rows, FFN_TILE), lambda j: (0, j + tiles))]


def _conv3(x, cw_ref, cb_ref, n):
    prev, nxt = _neighbours(x, n)
    return prev * cw_ref[0:1, :] + x * cw_ref[1:2, :] + nxt * cw_ref[2:3, :] + cb_ref[...], prev, nxt


def _gate_fwd_call(a, cw, cb, n):
    T, f = a.shape[0], a.shape[1] // 2

    def body(ag_ref, av_ref, wg_ref, wv_ref, bg_ref, bv_ref, u_ref):
        g = _conv3(ag_ref[...], wg_ref, bg_ref, n)[0]
        v = _conv3(av_ref[...], wv_ref, bv_ref, n)[0]
        u_ref[...] = (g * jax.nn.sigmoid(g) * v).astype(u_ref.dtype)

    return _pcall(
        body, name='ffn_gate_fwd', grid=(f // FFN_TILE,),
        in_specs=_gate_specs(T, f) + _gate_specs(3, f) + _gate_specs(1, f),
        out_specs=pl.BlockSpec((T, FFN_TILE), lambda j: (0, j)),
        out_shape=jax.ShapeDtypeStruct((T, f), BF16),
        compiler_params=pltpu.CompilerParams(dimension_semantics=("parallel",), vmem_limit_bytes=VMEM_LIMIT),
    )(a, a, cw, cw, cb, cb)


def _gate_bwd_call(a, cw, cb, du, n):
    T, f = a.shape[0], a.shape[1] // 2

    def half(dz, x, prev, nxt, w_ref, da_ref, dw_ref):
        t = lax.broadcasted_iota(jnp.int32, dz.shape, 0)
        from_next = pltpu.roll(jnp.where((t == 0) | (t == n), 0.0, dz), T - 1, 0)
        from_prev = pltpu.roll(jnp.where((t == n - 1) | (t == T - 1), 0.0, dz), 1, 0)
        da_ref[...] = (dz * w_ref[1:2, :] + from_next * w_ref[0:1, :] + from_prev * w_ref[2:3, :]).astype(da_ref.dtype)
        dw_ref[0:1, :] = jnp.sum(dz * prev, axis=0, keepdims=True)
        dw_ref[1:2, :] = jnp.sum(dz * x, axis=0, keepdims=True)
        dw_ref[2:3, :] = jnp.sum(dz * nxt, axis=0, keepdims=True)
        dw_ref[3:4, :] = jnp.sum(dz, axis=0, keepdims=True)
        dw_ref[4:8, :] = jnp.zeros((4, FFN_TILE), F32)

    def body(ag_ref, av_ref, wg_ref, wv_ref, bg_ref, bv_ref, du_ref, dag_ref, dav_ref, dwg_ref, dwv_ref):
        xg, xv = ag_ref[...], av_ref[...]
        g, g_prev, g_next = _conv3(xg, wg_ref, bg_ref, n)
        v, v_prev, v_next = _conv3(xv, wv_ref, bv_ref, n)
        sg = jax.nn.sigmoid(g)
        du = du_ref[...]
        half(du * v * (sg * (1.0 + g * (1.0 - sg))), xg, g_prev, g_next, wg_ref, dag_ref, dwg_ref)
        half(du * (g * sg), xv, v_prev, v_next, wv_ref, dav_ref, dwv_ref)

    tile = lambda rows: pl.BlockSpec((rows, FFN_TILE), lambda j: (0, j))
    return _pcall(
        body, name='ffn_gate_bwd', grid=(f // FFN_TILE,),
        in_specs=_gate_specs(T, f) + _gate_specs(3, f) + _gate_specs(1, f) + [tile(T)],
        out_specs=[tile(T), tile(T), tile(8), tile(8)],
        out_shape=[jax.ShapeDtypeStruct((T, f), BF16), jax.ShapeDtypeStruct((T, f), BF16),
                   jax.ShapeDtypeStruct((8, f), F32), jax.ShapeDtypeStruct((8, f), F32)],
        compiler_params=pltpu.CompilerParams(dimension_semantics=("parallel",), vmem_limit_bytes=VMEM_LIMIT),
    )(a, a, cw, cw, cb, cb, du)


@functools.partial(jax.custom_vjp, nondiff_argnums=(7,))
def conv_ffn(h, w_up, cw, cb, w_down, next_up, next_down, n):
    return _conv_ffn_fwd(h, w_up, cw, cb, w_down, next_up, next_down, n)[0]


def _conv_ffn_fwd(h, w_up, cw, cb, w_down, next_up, next_down, n):
    h16 = h.astype(BF16)
    a, g_up = _proj_fwd(h16, w_up, next_up, 'ffn_up')
    u16 = _gate_fwd_call(a, cw, cb.reshape(1, -1), n)
    y, g_down = _proj_fwd(u16, w_down, next_down, 'ffn_down')
    return (y, g_up, g_down), (h16, w_up, cw, cb, w_down, a, u16)


def _conv_ffn_bwd(n, res, cts):
    h16, w_up, cw, cb, w_down, a, u16 = res
    dy, d_g_up, d_g_down = cts
    du, dw_down, d_next_down = _proj_bwd(u16, w_down, dy, d_g_down, 'ffn_down')
    dag, dav, dwg, dwv = _gate_bwd_call(a, cw, cb.reshape(1, -1), du, n)
    dcw = jnp.concatenate([dwg, dwv], axis=1)
    dh, dw_up, d_next_up = _proj_bwd(h16, w_up, jnp.concatenate([dag, dav], axis=1), d_g_up, 'ffn_up')
    return dh, dw_up, dcw[:3], dcw[3], dw_down, d_next_up, d_next_down


conv_ffn.defvjp(_conv_ffn_fwd, _conv_ffn_bwd)


def _attn_fwd_call(q, k, v, scale):
    H, nq, dq = q.shape
    nk, dv = v.shape[1], v.shape[2]
    tq = _pick(nq, (256, 128))

    def body(q_ref, k_ref, v_ref, o_ref, lse_ref):
        s = lax.dot_general(q_ref[0], k_ref[0], (((1,), (1,)), ((), ())), preferred_element_type=F32) * scale
        m = jnp.max(s, axis=1, keepdims=True)
        p = jnp.exp(s - m)
        l = jnp.sum(p, axis=1, keepdims=True)
        pn = (p * (1.0 / l)).astype(BF16)
        o_ref[...] = jnp.dot(pn, v_ref[0], preferred_element_type=F32)
        lse_ref[0] = m + jnp.log(l)

    return _pcall(
        body, name='mla_attn_fwd', grid=(H, nq // tq),
        in_specs=[pl.BlockSpec((1, tq, dq), lambda h, i: (h, i, 0)),
                  pl.BlockSpec((1, nk, dq), lambda h, i: (h, 0, 0)),
                  pl.BlockSpec((1, nk, dv), lambda h, i: (h, 0, 0))],
        out_specs=[pl.BlockSpec((tq, dv), lambda h, i: (i, h)),
                   pl.BlockSpec((1, tq, 1), lambda h, i: (h, i, 0))],
        out_shape=[jax.ShapeDtypeStruct((nq, H * dv), F32), jax.ShapeDtypeStruct((H, nq, 1), F32)],
        compiler_params=pltpu.CompilerParams(
            dimension_semantics=("parallel", "parallel"), vmem_limit_bytes=VMEM_LIMIT),
    )(q, k, v)


def _attn_bwd_call(q, k, v, o, lse, do, scale):
    H, nq, dq = q.shape
    nk, dv = v.shape[1], v.shape[2]
    tq = _pick(nq, (128,))

    def body(q_ref, k_ref, v_ref, o_ref, lse_ref, do_ref, dq_ref, dk_ref, dv_ref):
        @pl.when(pl.program_id(1) == 0)
        def _():
            dk_ref[...] = jnp.zeros_like(dk_ref)
            dv_ref[...] = jnp.zeros_like(dv_ref)

        q16, k16, v16 = q_ref[0], k_ref[0], v_ref[0]
        do = do_ref[...]
        do16 = do.astype(BF16)
        s = lax.dot_general(q16, k16, (((1,), (1,)), ((), ())), preferred_element_type=F32) * scale
        p = jnp.exp(s - lse_ref[0])
        dv_ref[0] += lax.dot_general(p.astype(BF16), do16, (((0,), (0,)), ((), ())), preferred_element_type=F32)
        dp = lax.dot_general(do16, v16, (((1,), (1,)), ((), ())), preferred_element_type=F32)
        delta = jnp.sum(do * o_ref[...], axis=1, keepdims=True)
        ds16 = (p * (dp - delta) * scale).astype(BF16)
        dq_ref[0] = jnp.dot(ds16, k16, preferred_element_type=F32)
        dk_ref[0] += lax.dot_general(ds16, q16, (((0,), (0,)), ((), ())), preferred_element_type=F32)

    return _pcall(
        body, name='mla_attn_bwd', grid=(H, nq // tq),
        in_specs=[pl.BlockSpec((1, tq, dq), lambda h, i: (h, i, 0)),
                  pl.BlockSpec((1, nk, dq), lambda h, i: (h, 0, 0)),
                  pl.BlockSpec((1, nk, dv), lambda h, i: (h, 0, 0)),
                  pl.BlockSpec((tq, dv), lambda h, i: (i, h)),
                  pl.BlockSpec((1, tq, 1), lambda h, i: (h, i, 0)),
                  pl.BlockSpec((tq, dv), lambda h, i: (i, h))],
        out_specs=[pl.BlockSpec((1, tq, dq), lambda h, i: (h, i, 0)),
                   pl.BlockSpec((1, nk, dq), lambda h, i: (h, 0, 0)),
                   pl.BlockSpec((1, nk, dv), lambda h, i: (h, 0, 0))],
        out_shape=[jax.ShapeDtypeStruct((H, nq, dq), F32), jax.ShapeDtypeStruct((H, nk, dq), F32),
                   jax.ShapeDtypeStruct((H, nk, dv), F32)],
        compiler_params=pltpu.CompilerParams(
            dimension_semantics=("parallel", "arbitrary"), vmem_limit_bytes=VMEM_LIMIT),
    )(q, k, v, o, lse, do)


@functools.partial(jax.custom_vjp, nondiff_argnums=(3,))
def attention(q, k, v, scale):
    return _attn_fwd_call(q.astype(BF16), k.astype(BF16), v.astype(BF16), scale)[0]


def _attention_fwd(q, k, v, scale):
    q16, k16, v16 = q.astype(BF16), k.astype(BF16), v.astype(BF16)
    o, lse = _attn_fwd_call(q16, k16, v16, scale)
    return o, (q16, k16, v16, o, lse)


def _attention_bwd(scale, res, do):
    q16, k16, v16, o, lse = res
    return tuple(_attn_bwd_call(q16, k16, v16, o, lse, do, scale))


attention.defvjp(_attention_fwd, _attention_bwd)


def _win_geometry(kind, n):
    if kind == 'na':
        rows = n // GRID_W
        kh = min(NA_KH, rows)

        def start(i):
            return jnp.clip(i - kh // 2, 0, rows - kh) * GRID_W

        def bidx(i):
            return jnp.clip(i - kh // 2, 0, rows - kh) - i + (NA_KH - 1)

        return GRID_W, kh * GRID_W, start, bidx
    nb = n // SWA_BLOCK

    def start(i):
        return i * SWA_BLOCK

    def bidx(i):
        return jnp.where(i == 0, 0, jnp.where(i == nb - 1, 2, 1))

    return SWA_BLOCK, 3 * SWA_BLOCK, start, bidx


def _dot_nt(a, b):
    return lax.dot_general(a, b, (((1,), (1,)), ((), ())), preferred_element_type=F32)


def _dot_tn(a, b):
    return lax.dot_general(a, b, (((0,), (0,)), ((), ())), preferred_element_type=F32)


def _win_specs(q, k, kc, bias):
    hq, n, d = q.shape
    grp = hq // k.shape[0]
    hb = bias.shape[0]
    return [
        pl.BlockSpec((1, n, d), lambda h: (h, 0, 0)),
        pl.BlockSpec((1,) + k.shape[1:], lambda h: (h // grp, 0, 0)),
        pl.BlockSpec((1,) + k.shape[1:], lambda h: (h // grp, 0, 0)),
        pl.BlockSpec((1,) + kc.shape[1:], lambda h: (h // grp, 0, 0)),
        pl.BlockSpec((1,) + kc.shape[1:], lambda h: (h // grp, 0, 0)),
        pl.BlockSpec((1,) + bias.shape[1:], (lambda h: (h, 0, 0, 0)) if hb > 1 else (lambda h: (0, 0, 0, 0))),
        pl.BlockSpec(memory_space=pltpu.SMEM),
    ]


def _win_fwd_call(q, k, v, kc, vc, bias, sink, kind):
    hq, n, d = q.shape
    scale = d ** -0.5
    qb, wk, start, bidx = _win_geometry(kind, n)
    has_sink = kind == 'swa'

    def body(q_ref, k_ref, v_ref, kc_ref, vc_ref, b_ref, sink_ref, o_ref, lse_ref):
        kc16, vc16 = kc_ref[0], vc_ref[0]
        snk = sink_ref[pl.program_id(0)]

        def step(i, carry):
            qs = pl.multiple_of(i * qb, qb)
            ks = pl.multiple_of(start(i), GRID_W)
            q16 = q_ref[0, pl.ds(qs, qb), :]
            s1 = _dot_nt(q16, k_ref[0, pl.ds(ks, wk), :]) * scale + b_ref[0, bidx(i)]
            s2 = _dot_nt(q16, kc16) * scale
            m = jnp.maximum(jnp.max(s1, axis=1, keepdims=True), jnp.max(s2, axis=1, keepdims=True))
            if has_sink:
                m = jnp.maximum(m, snk)
            p1 = jnp.exp(s1 - m)
            p2 = jnp.exp(s2 - m)
            l = jnp.sum(p1, axis=1, keepdims=True) + jnp.sum(p2, axis=1, keepdims=True)
            if has_sink:
                l = l + jnp.exp(snk - m)
            inv = 1.0 / l
            o = (jnp.dot((p1 * inv).astype(BF16), v_ref[0, pl.ds(ks, wk), :], preferred_element_type=F32)
                 + jnp.dot((p2 * inv).astype(BF16), vc16, preferred_element_type=F32))
            o_ref[0, pl.ds(qs, qb), :] = o
            lse_ref[0, pl.ds(qs, qb), :] = m + jnp.log(l)
            return carry

        lax.fori_loop(0, n // qb, step, 0, unroll=4)

    return _pcall(
        body, name=kind + '_attn_fwd', grid=(hq,), in_specs=_win_specs(q, k, kc, bias),
        out_specs=[pl.BlockSpec((1, n, d), lambda h: (h, 0, 0)), pl.BlockSpec((1, n, 1), lambda h: (h, 0, 0))],
        out_shape=[jax.ShapeDtypeStruct((hq, n, d), F32), jax.ShapeDtypeStruct((hq, n, 1), F32)],
        compiler_params=pltpu.CompilerParams(dimension_semantics=("parallel",), vmem_limit_bytes=VMEM_LIMIT),
    )(q, k, v, kc, vc, bias, sink)


def _win_bwd_call(q, k, v, kc, vc, bias, sink, o, lse, do, kind):
    hq, n, d = q.shape
    scale = d ** -0.5
    qb, wk, start, bidx = _win_geometry(kind, n)
    has_sink = kind == 'swa'
    bias_grad = kind == 'na'

    def body(q_ref, k_ref, v_ref, kc_ref, vc_ref, b_ref, sink_ref, o_ref, lse_ref, do_ref,
             dq_ref, dk_ref, dv_ref, dkc_ref, dvc_ref, db_ref, dsink_ref):
        kc16, vc16 = kc_ref[0], vc_ref[0]
        snk = sink_ref[pl.program_id(0)]
        dk_ref[...] = jnp.zeros_like(dk_ref)
        dv_ref[...] = jnp.zeros_like(dv_ref)
        dkc_ref[...] = jnp.zeros_like(dkc_ref)
        dvc_ref[...] = jnp.zeros_like(dvc_ref)
        db_ref[...] = jnp.zeros_like(db_ref)

        def step(i, dsink):
            qs = pl.multiple_of(i * qb, qb)
            ks = pl.multiple_of(start(i), GRID_W)
            q16 = q_ref[0, pl.ds(qs, qb), :]
            k16 = k_ref[0, pl.ds(ks, wk), :]
            v16 = v_ref[0, pl.ds(ks, wk), :]
            lse = lse_ref[0, pl.ds(qs, qb), :]
            do = do_ref[0, pl.ds(qs, qb), :]
            do16 = do.astype(BF16)
            p1 = jnp.exp(_dot_nt(q16, k16) * scale + b_ref[0, bidx(i)] - lse)
            p2 = jnp.exp(_dot_nt(q16, kc16) * scale - lse)
            delta = jnp.sum(do * o_ref[0, pl.ds(qs, qb), :], axis=1, keepdims=True)
            ds1 = p1 * (_dot_nt(do16, v16) - delta)
            ds2 = p2 * (_dot_nt(do16, vc16) - delta)
            if bias_grad:
                db_ref[0, bidx(i)] += ds1
            ds1 = (ds1 * scale).astype(BF16)
            ds2 = (ds2 * scale).astype(BF16)
            dq_ref[0, pl.ds(qs, qb), :] = (jnp.dot(ds1, k16, preferred_element_type=F32)
                                          + jnp.dot(ds2, kc16, preferred_element_type=F32))
            dk_ref[0, pl.ds(ks, wk), :] += _dot_tn(ds1, q16)
            dv_ref[0, pl.ds(ks, wk), :] += _dot_tn(p1.astype(BF16), do16)
            dkc_ref[0] += _dot_tn(ds2, q16)
            dvc_ref[0] += _dot_tn(p2.astype(BF16), do16)
            if has_sink:
                dsink = dsink - jnp.sum(jnp.exp(snk - lse) * delta)
            return dsink

        dsink = lax.fori_loop(0, n // qb, step, jnp.zeros((), F32), unroll=2)
        dsink_ref[...] = jnp.full(dsink_ref.shape, dsink, F32)

    per_head = lambda shape: pl.BlockSpec((1,) + shape[1:], lambda h: (h,) + (0,) * (len(shape) - 1))
    kq = (hq,) + k.shape[1:]
    cq = (hq,) + kc.shape[1:]
    bq = (hq,) + bias.shape[1:]
    in_specs = _win_specs(q, k, kc, bias) + [per_head(o.shape), per_head(lse.shape), per_head(do.shape)]
    out_shapes = [q.shape, kq, kq, cq, cq, bq, (hq, 8, 128)]
    return _pcall(
        body, name=kind + '_attn_bwd', grid=(hq,), in_specs=in_specs,
        out_specs=[per_head(s) for s in out_shapes],
        out_shape=[jax.ShapeDtypeStruct(s, F32) for s in out_shapes],
        compiler_params=pltpu.CompilerParams(dimension_semantics=("parallel",), vmem_limit_bytes=VMEM_LIMIT),
    )(q, k, v, kc, vc, bias, sink, o, lse, do)


@functools.partial(jax.custom_vjp, nondiff_argnums=(7,))
def win_attention(q, k, v, kc, vc, bias, sink, kind):
    b16 = lambda t: t.astype(BF16)
    return _win_fwd_call(b16(q), b16(k), b16(v), b16(kc), b16(vc), bias, sink, kind)[0]


def _win_attention_fwd(q, k, v, kc, vc, bias, sink, kind):
    res = tuple(t.astype(BF16) for t in (q, k, v, kc, vc)) + (bias, sink)
    o, lse = _win_fwd_call(*res, kind)
    return o, res + (o, lse)


def _win_attention_bwd(kind, res, do):
    q, k, v, kc, vc, bias, sink, o, lse = res
    dq, dk, dv, dkc, dvc, db, dsink = _win_bwd_call(q, k, v, kc, vc, bias, sink, o, lse, do, kind)
    hkv = k.shape[0]
    fold = lambda t: t.reshape((hkv, -1) + t.shape[1:]).sum(axis=1)
    if bias.shape[0] == 1:
        db = jnp.zeros_like(bias)
    return dq, fold(dk), fold(dv), fold(dkc), fold(dvc), db, dsink[:, 0, 0]


win_attention.defvjp(_win_attention_fwd, _win_attention_bwd)


def _loss_head(y, target):
    n, d = y.shape
    tr = _pick(n, (512, 256, 128))
    nb = n // tr

    def body(y_ref, t_ref, dy_ref, part_ref):
        err = y_ref[...] - t_ref[...]
        dy_ref[...] = err * (1.0 / d)
        part_ref[...] = jnp.full(part_ref.shape, jnp.sum(err * err), F32)

    dy, part = _pcall(
        body, name='loss_head', grid=(nb,),
        in_specs=[pl.BlockSpec((tr, d), lambda i: (i, 0)), pl.BlockSpec((tr, d), lambda i: (i, 0))],
        out_specs=[pl.BlockSpec((tr, d), lambda i: (i, 0)), pl.BlockSpec((1, 8, 128), lambda i: (i, 0, 0))],
        out_shape=[jax.ShapeDtypeStruct((n, d), F32), jax.ShapeDtypeStruct((nb, 8, 128), F32)],
        compiler_params=pltpu.CompilerParams(dimension_semantics=("parallel",), vmem_limit_bytes=VMEM_LIMIT),
    )(y, target)
    return 0.5 * jnp.sum(part[:, 0, 0]) / d, dy


def _sum_parts(parts, name):
    P, R, C = parts.shape
    tr = _pick(R, (256, 128, 64, 32, 16, 8))

    def body(p_ref, o_ref):
        acc = p_ref[0].astype(F32)
        for i in range(1, P):
            acc = acc + p_ref[i].astype(F32)
        o_ref[...] = acc

    return _pcall(
        body, name=name, grid=(R // tr,),
        in_specs=[pl.BlockSpec((P, tr, C), lambda i: (0, i, 0))],
        out_specs=pl.BlockSpec((tr, C), lambda i: (i, 0)),
        out_shape=jax.ShapeDtypeStruct((R, C), F32),
        compiler_params=pltpu.CompilerParams(dimension_semantics=("parallel",), vmem_limit_bytes=VMEM_LIMIT),
    )(parts)


def _adamw(parts, w, m, v, name):
    P, R, C = parts.shape
    tr = _pick(R, (128, 64, 32, 16, 8))
    c1 = 1.0 / (1.0 - ADAM_B1 ** ADAM_STEP)
    c2 = 1.0 / (1.0 - ADAM_B2 ** ADAM_STEP)

    def body(p_ref, w_ref, m_ref, v_ref, g_out, d_out, m_out, v_out):
        g = p_ref[0].astype(F32)
        for i in range(1, P):
            g = g + p_ref[i].astype(F32)
        m_new = ADAM_B1 * m_ref[...] + (1.0 - ADAM_B1) * g
        v_new = ADAM_B2 * v_ref[...] + (1.0 - ADAM_B2) * (g * g)
        g_out[...] = g
        m_out[...] = m_new
        v_out[...] = v_new
        d_out[...] = -ADAM_LR * ((m_new * c1) / (jnp.sqrt(v_new * c2) + ADAM_EPS) + ADAM_WD * w_ref[...])

    blk = pl.BlockSpec((tr, C), lambda i: (i, 0))
    return _pcall(
        body, name=name, grid=(R // tr,),
        in_specs=[pl.BlockSpec((P, tr, C), lambda i: (0, i, 0)), blk, blk, blk],
        out_specs=[blk, blk, blk, blk],
        out_shape=[jax.ShapeDtypeStruct((R, C), F32)] * 4,
        compiler_params=pltpu.CompilerParams(dimension_semantics=("parallel",), vmem_limit_bytes=VMEM_LIMIT),
    )(parts, w, m, v)


def rms_norm(x, g):
    return x * lax.rsqrt(jnp.mean(x * x, axis=-1, keepdims=True) + NORM_EPS) * g


def modulate(h, shift, scale):
    return h * (1.0 + scale) + shift


def axial_angles(n, d_rot):
    t = jnp.arange(n)
    row = (t // GRID_W).astype(F32)
    col = (t % GRID_W).astype(F32)
    d_axis = d_rot // 2
    inv_freq = ROPE_BASE ** (-jnp.arange(0, d_axis, 2, dtype=F32) / d_axis)
    return (row[:, None] * inv_freq, col[:, None] * inv_freq)


def rope_segment(x, ang):
    cos = jnp.cos(ang)[:, None, :]
    sin = jnp.sin(ang)[:, None, :]
    x1, x2 = jnp.split(x, 2, axis=-1)
    return jnp.concatenate([x1 * cos - x2 * sin, x2 * cos + x1 * sin], axis=-1)


def axial_rope(x, ang):
    half = x.shape[-1] // 2
    return jnp.concatenate([rope_segment(x[..., :half], ang[0]), rope_segment(x[..., half:], ang[1])], axis=-1)


def rope_latent(t, n, ang):
    return jnp.concatenate([axial_rope(t[:n], ang), t[n:]], axis=0)


def mla_attend_ctx(q_nope, q_rope, k_nope, k_rope, v):
    s = (jnp.einsum('qhd,khd->hqk', q_nope, k_nope, preferred_element_type=F32)
         + jnp.einsum('qhr,kr->hqk', q_rope, k_rope, preferred_element_type=F32))
    p = jax.nn.softmax(s * MLA_SCALE, axis=-1)
    return jnp.einsum('hqk,khd->qhd', p, v)


def pool_mixer(u, w_pool, scale):
    n = u.shape[0]
    csum = jnp.pad(jnp.cumsum(u, axis=0), ((1, 0), (0, 0)))
    t = jnp.arange(n)
    diffs = []
    for g, w in enumerate(POOL_WINDOWS):
        sl = slice(g * POOL_GROUP, (g + 1) * POOL_GROUP)
        lo = jnp.clip(t - w // 2, 0, n)
        hi = jnp.clip(t + w // 2, 0, n)
        cs = csum[:, sl]
        mean = (cs[hi] - cs[lo]) / (hi - lo).astype(F32)[:, None]
        diffs.append(mean - u[:, sl])
    d = jnp.stack(diffs, axis=1)
    y = jnp.einsum('ngc,gcd->ngd', d, w_pool).reshape(n, POOL_WIDTH)
    return y * scale


def swa_latent(q, k, v, k_ctx, v_ctx, sink):
    n, hq, d = q.shape
    blk = SWA_BLOCK
    a = jnp.arange(blk)[:, None]
    j = jnp.arange(3 * blk)[None, :]
    near = jnp.abs(j - blk - a) <= SWA_WINDOW
    tiles = jnp.stack([near & (j >= blk), near, near & (j < 2 * blk)])
    bias = jnp.where(tiles, 0.0, NEG_INF).astype(F32)[None]
    heads = lambda t: t.transpose(1, 0, 2)
    pad = lambda t: jnp.pad(heads(t), ((0, 0), (blk, blk), (0, 0)))
    o = win_attention(heads(q), pad(k), pad(v), heads(k_ctx), heads(v_ctx), bias, sink, 'swa')
    return o.transpose(1, 0, 2).reshape(n, hq * d)


def ctx_attention(q, k, v, sink):
    nq, hq, d = q.shape
    hkv = k.shape[1]
    grp = hq // hkv
    nk = k.shape[0]
    qg = q.reshape(nq, hkv, grp, d)
    s = jnp.einsum('qhgd,khd->hgqk', qg, k, preferred_element_type=F32) * (d ** -0.5)
    if sink is not None:
        s_sink = jnp.broadcast_to(sink.reshape(hkv, grp)[:, :, None, None], s.shape[:-1] + (1,))
        s = jnp.concatenate([s, s_sink], axis=-1)
    p = jax.nn.softmax(s, axis=-1)[..., :nk]
    o = jnp.einsum('hgqk,khd->qhgd', p, v)
    return o.reshape(nq, hq * d)


def na_bias_tiles(rpb, n):
    rows = n // GRID_W
    kh = min(NA_KH, rows)
    qc = jnp.arange(GRID_W)[:, None]
    kc = jnp.arange(GRID_W)[None, :]
    dc = jnp.clip(kc - qc, 1 - NA_KW, NA_KW - 1) + (NA_KW - 1)
    onehot = (dc[None] == jnp.arange(2 * NA_KW - 1)[:, None, None]).astype(F32)
    toeplitz = jnp.einsum('hdt,tqk->hdqk', rpb, onehot, precision=lax.Precision.HIGHEST)
    q_col0 = jnp.clip(qc - NA_KW // 2, 0, GRID_W - NA_KW)
    valid = (kc >= q_col0) & (kc < q_col0 + NA_KW)
    masked = jnp.where(valid, toeplitz, NEG_INF)
    return jnp.stack([jnp.concatenate([masked[:, off + j] for j in range(kh)], axis=-1) for off in range(NA_KH)], axis=1)


def na_latent(q, k, v, k_ctx, v_ctx, rpb):
    n, h, d = q.shape
    heads = lambda t: t.transpose(1, 0, 2)
    o = win_attention(heads(q), heads(k), heads(v), heads(k_ctx), heads(v_ctx), na_bias_tiles(rpb, n),
                      jnp.zeros((h,), F32), 'na')
    return o.transpose(1, 0, 2).reshape(n, h * d)


def _assemble(g, name):
    _, r, c = g.shape
    w = g.reshape(N_DEV * r, c) if BIG[name] == 1 else g.transpose(1, 0, 2).reshape(r, N_DEV * c)
    if name == 'w_in':
        w = jnp.pad(w, ((0, 0), (0, IN_COLS_PAD - IN_COLS)))
    return w


def _forward(x, mod_x, mod_c, gathered0, shards, ws, ctx):
    n = x.shape[0]
    depth = mod_x.shape[0]
    ang_mla = axial_angles(n, MLA_ROPE)
    ang_swa = axial_angles(n, SWA_HEAD_DIM)
    gathered = gathered0
    for l in range(depth):
        update_ctx = l < depth - 1
        w = {name: _assemble(gathered[name], name) for name in BIG}
        gathered = {}

        nxt = {name: shards[name][l + 1] if update_ctx else None for name in BIG}

        def proj(a, name):
            y, gathered[name] = pmm(a, w[name], nxt[name], name)
            return y

        sh_m, sc_m, gt_m, sh_f, sc_f, gt_f = jnp.split(mod_x[l], 6)
        csh_m, csc_m, cgt_m, csh_f, csc_f, cgt_f = jnp.split(mod_c[l], 6)

        h_all = jnp.concatenate([modulate(rms_norm(x, ws['g_mix'][l]), sh_m, sc_m),
                                 modulate(rms_norm(ctx, ws['g_mix'][l]), csh_m, csc_m)], axis=0)
        p_all = proj(h_all, 'w_in')[:, :IN_COLS]
        mla_p, pool_p, swa_p, na_p = jnp.split(p_all, IN_SPLITS, axis=-1)
        T = p_all.shape[0]

        cq, ckv, kr = jnp.split(mla_p, [MLA_Q_LORA, MLA_Q_LORA + MLA_KV_LORA], axis=-1)
        q = proj(rms_norm(cq, ws['mla_q_a_norm'][l]), 'mla_w_qb').reshape(T, MLA_HEADS, MLA_NOPE + MLA_ROPE)
        kv = proj(rms_norm(ckv, ws['mla_kv_a_norm'][l]), 'mla_w_kvb').reshape(T, MLA_HEADS, MLA_NOPE + MLA_V)
        q_nope = rms_norm(q[..., :MLA_NOPE], ws['mla_q_nope_norm'][l])
        q_rope = rope_latent(rms_norm(q[..., MLA_NOPE:], ws['mla_q_rope_norm'][l]), n, ang_mla)
        k_nope = rms_norm(kv[..., :MLA_NOPE], ws['mla_k_nope_norm'][l])
        v_mla = kv[..., MLA_NOPE:]
        k_rope = rope_latent(rms_norm(kr, ws['mla_k_rope_norm'][l])[:, None, :], n, ang_mla)
        q_cat = jnp.concatenate([q_nope, q_rope], axis=-1).transpose(1, 0, 2)
        k_cat = jnp.concatenate([k_nope, jnp.broadcast_to(k_rope, (T, MLA_HEADS, MLA_ROPE))], axis=-1).transpose(1, 0, 2)
        out_a = attention(q_cat[:, :n], k_cat, v_mla.transpose(1, 0, 2), MLA_SCALE)

        out_b = pool_mixer(pool_p[:n], ws['pool_w'][l], ws['pool_scale'][l])

        sq, sk, sv = jnp.split(swa_p, [SWA_HEADS * SWA_HEAD_DIM, (SWA_HEADS + SWA_KV_HEADS) * SWA_HEAD_DIM], axis=-1)
        sq = rope_latent(rms_norm(sq.reshape(T, SWA_HEADS, SWA_HEAD_DIM), ws['swa_q_norm'][l]), n, ang_swa)
        sk = rope_latent(rms_norm(sk.reshape(T, SWA_KV_HEADS, SWA_HEAD_DIM), ws['swa_k_norm'][l]), n, ang_swa)
        sv = sv.reshape(T, SWA_KV_HEADS, SWA_HEAD_DIM)
        out_c = swa_latent(sq[:n], sk[:n], sv[:n], sk[n:], sv[n:], ws['swa_sink'][l])

        nq_, nk_, nv_ = jnp.split(na_p, 3, axis=-1)
        nq_ = rms_norm(nq_.reshape(T, NA_HEADS, NA_HEAD_DIM), ws['na_q_norm'][l])
        nk_ = rms_norm(nk_.reshape(T, NA_HEADS, NA_HEAD_DIM), ws['na_k_norm'][l])
        nv_ = nv_.reshape(T, NA_HEADS, NA_HEAD_DIM)
        out_d = na_latent(nq_[:n], nk_[:n], nv_[:n], nk_[n:], nv_[n:], ws['na_rpb'][l])

        mix_x = jnp.concatenate([out_a, out_b, out_c, out_d], axis=-1)

        def ffn(h):
            y, gathered['ffn_w_up'], gathered['ffn_w_down'] = conv_ffn(
                h, w['ffn_w_up'], ws['ffn_conv_w'][l], ws['ffn_conv_b'][l], w['ffn_w_down'],
                nxt['ffn_w_up'], nxt['ffn_w_down'], n)
            return y

        if update_ctx:
            L = T - n
            mix_c = jnp.concatenate([
                mla_attend_ctx(q_nope[n:], q_rope[n:], k_nope[n:], k_rope[n:, 0], v_mla[n:]).reshape(L, MLA_HEADS * MLA_V),
                pool_mixer(pool_p[n:], ws['pool_w'][l], ws['pool_scale'][l]),
                ctx_attention(sq[n:], sk[n:], sv[n:], ws['swa_sink'][l]),
                ctx_attention(nq_[n:], nk_[n:], nv_[n:], None),
            ], axis=-1)
            o_all = proj(jnp.concatenate([mix_x, mix_c], axis=0), 'w_out')
            x = x + gt_m * o_all[:n]
            ctx = ctx + cgt_m * o_all[n:]
            h2 = jnp.concatenate([modulate(rms_norm(x, ws['g_ffn'][l]), sh_f, sc_f),
                                  modulate(rms_norm(ctx, ws['g_ffn'][l]), csh_f, csc_f)], axis=0)
            f_all = ffn(h2)
            x = x + gt_f * f_all[:n]
            ctx = ctx + cgt_f * f_all[n:]
        else:
            x = x + gt_m * proj(mix_x, 'w_out')
            x = x + gt_f * ffn(modulate(rms_norm(x, ws['g_ffn'][l]), sh_f, sc_f))
    return x


def _pack_rows(vecs):
    flat = jnp.concatenate([v.reshape(-1).astype(F32) for v in vecs])
    pad = (-flat.shape[0]) % 1024
    return jnp.pad(flat, (0, pad)).reshape(-1, 128)


def _unpack(flat, shapes):
    out, off = [], 0
    for s in shapes:
        size = 1
        for d in s:
            size *= d
        out.append(flat[off:off + size].reshape(s))
        off += size
    return out


def _silu_grad(z):
    s = jax.nn.sigmoid(z)
    return s * (1.0 + z * (1.0 - s))


def kernel(x, c, ctx, c_ctx, w_mod, b_mod, g_mix, g_ffn, w_in, w_out, mla_q_a_norm, mla_w_qb, mla_kv_a_norm, mla_w_kvb, mla_q_nope_norm, mla_q_rope_norm, mla_k_nope_norm, mla_k_rope_norm, pool_w, pool_scale, swa_q_norm, swa_k_norm, swa_sink, na_q_norm, na_k_norm, na_rpb, ffn_w_up, ffn_conv_w, ffn_conv_b, ffn_w_down, loss_target, m_c_ctx, m_w_mod, m_b_mod, m_g_mix, m_g_ffn, m_w_in, m_w_out, m_mla_q_a_norm, m_mla_w_qb, m_mla_kv_a_norm, m_mla_w_kvb, m_mla_q_nope_norm, m_mla_q_rope_norm, m_mla_k_nope_norm, m_mla_k_rope_norm, m_pool_w, m_pool_scale, m_swa_q_norm, m_swa_k_norm, m_swa_sink, m_na_q_norm, m_na_k_norm, m_na_rpb, m_ffn_w_up, m_ffn_conv_w, m_ffn_conv_b, m_ffn_w_down, v_c_ctx, v_w_mod, v_b_mod, v_g_mix, v_g_ffn, v_w_in, v_w_out, v_mla_q_a_norm, v_mla_w_qb, v_mla_kv_a_norm, v_mla_w_kvb, v_mla_q_nope_norm, v_mla_q_rope_norm, v_mla_k_nope_norm, v_mla_k_rope_norm, v_pool_w, v_pool_scale, v_swa_q_norm, v_swa_k_norm, v_swa_sink, v_na_q_norm, v_na_k_norm, v_na_rpb, v_ffn_w_up, v_ffn_conv_w, v_ffn_conv_b, v_ffn_w_down):
    return _step(x, c, ctx, c_ctx, w_mod, b_mod, g_mix, g_ffn, w_in, w_out, mla_q_a_norm, mla_w_qb, mla_kv_a_norm, mla_w_kvb, mla_q_nope_norm, mla_q_rope_norm, mla_k_nope_norm, mla_k_rope_norm, pool_w, pool_scale, swa_q_norm, swa_k_norm, swa_sink, na_q_norm, na_k_norm, na_rpb, ffn_w_up, ffn_conv_w, ffn_conv_b, ffn_w_down, loss_target, m_c_ctx, m_w_mod, m_b_mod, m_g_mix, m_g_ffn, m_w_in, m_w_out, m_mla_q_a_norm, m_mla_w_qb, m_mla_kv_a_norm, m_mla_w_kvb, m_mla_q_nope_norm, m_mla_q_rope_norm, m_mla_k_nope_norm, m_mla_k_rope_norm, m_pool_w, m_pool_scale, m_swa_q_norm, m_swa_k_norm, m_swa_sink, m_na_q_norm, m_na_k_norm, m_na_rpb, m_ffn_w_up, m_ffn_conv_w, m_ffn_conv_b, m_ffn_w_down, v_c_ctx, v_w_mod, v_b_mod, v_g_mix, v_g_ffn, v_w_in, v_w_out, v_mla_q_a_norm, v_mla_w_qb, v_mla_kv_a_norm, v_mla_w_kvb, v_mla_q_nope_norm, v_mla_q_rope_norm, v_mla_k_nope_norm, v_mla_k_rope_norm, v_pool_w, v_pool_scale, v_swa_q_norm, v_swa_k_norm, v_swa_sink, v_na_q_norm, v_na_k_norm, v_na_rpb, v_ffn_w_up, v_ffn_conv_w, v_ffn_conv_b, v_ffn_w_down)


def _step(*args):
    n_in = len(ARG_NAMES)
    n_w = len(WEIGHTS)
    given = dict(zip(ARG_NAMES, args[:n_in]))
    mom = dict(zip(WEIGHTS, args[n_in:n_in + n_w]))
    var = dict(zip(WEIGHTS, args[n_in + n_w:n_in + 2 * n_w]))
    me = _my_index()

    x = given['x'][0]
    ctx = given['ctx'][0]
    target = given['loss_target'][0]
    n, D = x.shape
    depth = given['w_mod'].shape[0]
    mod_cols = given['w_mod'].shape[2]
    conv_cols = given['ffn_conv_w'].shape[2]

    gathered0 = {name: _all_gather(given[name][0].astype(BF16), 'ag_' + name) for name in BIG}
    misc = _all_gather(_pack_rows([given['c'], given['ffn_conv_w']]), 'ag_cond')
    misc = misc.reshape(N_DEV, -1)
    c_all = misc[:, :D]
    conv_w = misc[:, D:D + depth * 3 * conv_cols].reshape(N_DEV, depth, 3, conv_cols)
    conv_w = conv_w.transpose(1, 2, 0, 3).reshape(depth, 3, N_DEV * conv_cols)

    cond = jnp.concatenate([c_all, given['c_ctx'][None], jnp.zeros((16 - N_DEV - 1, D), F32)], axis=0)
    s16 = jax.nn.silu(cond).astype(BF16)
    wm16 = given['w_mod'].astype(BF16)
    b_loc = lax.dynamic_slice_in_dim(given['b_mod'], me * mod_cols, mod_cols, axis=1)
    mod_part = jnp.stack([_mm(s16, wm16[l], 'nn', F32, 'mod_fwd') + b_loc[l] for l in range(depth)])
    mod_all = _all_gather(mod_part, 'ag_mod').transpose(1, 2, 0, 3).reshape(depth, 16, N_DEV * mod_cols)
    mod_x = lax.dynamic_index_in_dim(mod_all, me, axis=1, keepdims=False)
    mod_c = mod_all[:, N_DEV]

    ws = {name: given[name] for name in SMALL if name not in ('c_ctx', 'b_mod')}
    ws['ffn_conv_w'] = conv_w
    shards = {name: (None,) + tuple(given[name][l] for l in range(1, depth)) for name in BIG}
    y, vjp = jax.vjp(lambda *d: _forward(*d, ctx), x, mod_x, mod_c, gathered0, shards, ws)
    loss_local, dy = _loss_head(y, target)
    g_x, g_mod_x, g_mod_c, g_gathered0, g_shards, g_ws = vjp(dy)

    g_big = {}
    for name in BIG:
        parts = _all_to_all(g_gathered0[name], 'a2a_' + name)
        g0 = _sum_parts(parts, 'sum_' + name)
        g_big[name] = jnp.stack((g0,) + tuple(g_shards[name][1:]))

    g_mod = jnp.zeros((depth, 16, N_DEV * mod_cols), F32)
    g_mod = lax.dynamic_update_slice_in_dim(g_mod, g_mod_x[:, None, :], me, axis=1)
    g_mod = g_mod.at[:, N_DEV].set(g_mod_c)
    g_mod_parts = g_mod.reshape(depth, 16, N_DEV, mod_cols).transpose(2, 0, 1, 3).reshape(N_DEV, -1)
    g_conv_parts = g_ws['ffn_conv_w'].reshape(depth, 3, N_DEV, conv_cols).transpose(2, 0, 1, 3).reshape(N_DEV, -1)
    n_mod = depth * 16 * mod_cols
    n_conv = depth * 3 * conv_cols
    f32_parts = jnp.concatenate([g_mod_parts, g_conv_parts], axis=1)
    f32_pad = (-f32_parts.shape[1]) % 1024
    f32_parts = jnp.pad(f32_parts, ((0, 0), (0, f32_pad))).reshape(N_DEV, -1, 128)
    f32_parts = _all_to_all(f32_parts, 'a2a_f32')
    f32_sum = _sum_parts(f32_parts, 'sum_f32').reshape(-1)
    g_mod_loc = f32_sum[:n_mod].reshape(depth, 16, mod_cols)
    g_conv_loc = f32_sum[n_mod:n_mod + n_conv].reshape(depth * 3, conv_cols)

    g_mod16 = g_mod_loc.astype(BF16)
    g_w_mod = jnp.stack([_mm(s16, g_mod16[l], 'tn', F32, 'mod_dw') for l in range(depth)])
    d_silu = sum(_mm(g_mod16[l], wm16[l], 'nt', F32, 'mod_dc') for l in range(depth))
    g_c_ctx_part = d_silu[N_DEV] * _silu_grad(given['c_ctx'])

    small_grads = {name: g_ws[name] for name in SMALL if name not in ('c_ctx', 'b_mod')}
    small_grads['c_ctx'] = g_c_ctx_part
    small_grads['b_mod'] = g_mod_x + g_mod_c
    small_shapes = [given[name].shape for name in SMALL]
    n_small = sum(int(given[name].size) for name in SMALL)
    packed = _pack_rows([small_grads[name] for name in SMALL] + [loss_local.reshape(1)])
    small_parts = _all_gather(packed, 'ag_small')

    out_g, out_d, out_m, out_v = {}, {}, {}, {}

    def update(name, parts):
        shape = given[name].shape
        flat = (shape[0] * shape[1], shape[2])
        res = _adamw(parts.reshape((-1,) + flat), given[name].reshape(flat), mom[name].reshape(flat),
                     var[name].reshape(flat), 'adamw_' + name)
        out_g[name], out_d[name], out_m[name], out_v[name] = (r.reshape(shape) for r in res)

    for name in BIG:
        update(name, g_big[name])
    update('w_mod', g_w_mod)
    update('ffn_conv_w', g_conv_loc)

    zero1 = jnp.zeros((1,), F32)
    res = _adamw(small_parts, _pack_rows([given[k] for k in SMALL] + [zero1]), _pack_rows([mom[k] for k in SMALL] + [zero1]),
                 _pack_rows([var[k] for k in SMALL] + [zero1 + 1.0]), 'adamw_small')
    flats = [r.reshape(-1) for r in res]
    for name, g_, d_, m_, v_ in zip(SMALL, *[_unpack(f, small_shapes) for f in flats]):
        out_g[name], out_d[name], out_m[name], out_v[name] = g_, d_, m_, v_
    loss = flats[0][n_small]

    return (loss, g_x[None], *[out_g[k] for k in WEIGHTS], *[out_d[k] for k in WEIGHTS],
            *[out_m[k] for k in WEIGHTS], *[out_v[k] for k in WEIGHTS])
```

```python
import functools

import jax
import jax.numpy as jnp
from jax import lax
from jax.experimental import pallas as pl
from jax.experimental.pallas import tpu as pltpu

F32 = jnp.float32
BF16 = jnp.bfloat16
N_DEV = 8
MESH_ID = pl.DeviceIdType.MESH

GRID_W = 64
ROPE_BASE = 10000.0
NORM_EPS = 1e-6
NEG_INF = -1e30

MLA_HEADS = 4
MLA_NOPE = 128
MLA_ROPE = 64
MLA_V = 128
MLA_Q_LORA = 512
MLA_KV_LORA = 256
MLA_SCALE = (MLA_NOPE + MLA_ROPE) ** -0.5
POOL_WINDOWS = (2, 4, 8, 16)
POOL_GROUP = 128
POOL_WIDTH = POOL_GROUP * len(POOL_WINDOWS)
SWA_HEADS = 8
SWA_KV_HEADS = 2
SWA_HEAD_DIM = 64
SWA_WINDOW = 128
SWA_BLOCK = 128
NA_HEADS = 8
NA_HEAD_DIM = 64
NA_KH = 8
NA_KW = 16
NA_QC = 16
NA_KC = NA_QC + NA_KW

A_COLS = MLA_Q_LORA + MLA_KV_LORA + MLA_ROPE
B_COLS = POOL_WIDTH
C_COLS = (SWA_HEADS + 2 * SWA_KV_HEADS) * SWA_HEAD_DIM
D_COLS = 3 * NA_HEADS * NA_HEAD_DIM
IN_COLS = A_COLS + B_COLS + C_COLS + D_COLS
IN_COLS_PAD = 3840
IN_SPLITS = (A_COLS, A_COLS + B_COLS, A_COLS + B_COLS + C_COLS)

ADAM_LR = 0.001
ADAM_B1 = 0.9
ADAM_B2 = 0.999
ADAM_EPS = 1e-08
ADAM_WD = 0.01
ADAM_STEP = 10

VMEM_LIMIT = 48 << 20

ARG_NAMES = ['x', 'c', 'ctx', 'c_ctx', 'w_mod', 'b_mod', 'g_mix', 'g_ffn', 'w_in', 'w_out', 'mla_q_a_norm', 'mla_w_qb', 'mla_kv_a_norm', 'mla_w_kvb', 'mla_q_nope_norm', 'mla_q_rope_norm', 'mla_k_nope_norm', 'mla_k_rope_norm', 'pool_w', 'pool_scale', 'swa_q_norm', 'swa_k_norm', 'swa_sink', 'na_q_norm', 'na_k_norm', 'na_rpb', 'ffn_w_up', 'ffn_conv_w', 'ffn_conv_b', 'ffn_w_down', 'loss_target']
WEIGHTS = ['c_ctx', 'w_mod', 'b_mod', 'g_mix', 'g_ffn', 'w_in', 'w_out', 'mla_q_a_norm', 'mla_w_qb', 'mla_kv_a_norm', 'mla_w_kvb', 'mla_q_nope_norm', 'mla_q_rope_norm', 'mla_k_nope_norm', 'mla_k_rope_norm', 'pool_w', 'pool_scale', 'swa_q_norm', 'swa_k_norm', 'swa_sink', 'na_q_norm', 'na_k_norm', 'na_rpb', 'ffn_w_up', 'ffn_conv_w', 'ffn_conv_b', 'ffn_w_down']
BIG = {'w_in': 2, 'w_out': 1, 'mla_w_qb': 2, 'mla_w_kvb': 2, 'ffn_w_up': 2, 'ffn_w_down': 1}
SMALL = ['c_ctx', 'b_mod', 'g_mix', 'g_ffn', 'mla_q_a_norm', 'mla_kv_a_norm', 'mla_q_nope_norm', 'mla_q_rope_norm', 'mla_k_nope_norm', 'mla_k_rope_norm', 'pool_w', 'pool_scale', 'swa_q_norm', 'swa_k_norm', 'swa_sink', 'na_q_norm', 'na_k_norm', 'na_rpb', 'ffn_conv_b']


def _pcall(body, **kw):
    return pl.pallas_call(body, **kw)


def _my_index():
    return 4 * lax.axis_index("x") + 2 * lax.axis_index("y") + lax.axis_index("c")


_COMM_SCRATCH = [pltpu.SemaphoreType.DMA((7,)), pltpu.SemaphoreType.DMA((7,)), pltpu.SemaphoreType.DMA(())]
_ANY = pl.BlockSpec(memory_space=pl.ANY)


def _gather_copies(x_ref, out_ref, send_sems, recv_sems, local_sem):
    x, y, c = lax.axis_index("x"), lax.axis_index("y"), lax.axis_index("c")
    me, sibling = (x, y, c), (x, y, 1 - c)
    chips = [(1 - x, y), (x, 1 - y), (1 - x, 1 - y)]

    def slot(px, py, pc):
        return out_ref.at[4 * px + 2 * py + pc]

    def copy(k, blk, to, src=None):
        return pltpu.make_async_remote_copy(
            src_ref=slot(*blk) if src is None else src, dst_ref=slot(*blk),
            send_sem=send_sems.at[k], recv_sem=recv_sems.at[k], device_id=to, device_id_type=MESH_ID)

    mine = pltpu.make_async_copy(x_ref, slot(*me), local_sem)
    first = [copy(0, me, sibling, src=x_ref)] + [copy(1 + j, me, (*chip, c), src=x_ref) for j, chip in enumerate(chips)]

    def start():
        mine.start()
        for cp in first:
            cp.start()

    def finish():
        passed = [copy(4 + j, (*chip, c), sibling) for j, chip in enumerate(chips)]
        for j, chip in enumerate(chips):
            copy(1 + j, (*chip, c), me).wait_recv()
            passed[j].start()
        copy(0, sibling, me).wait_recv()
        for j, chip in enumerate(chips):
            copy(4 + j, (*chip, 1 - c), me).wait_recv()
        for cp in first + passed:
            cp.wait_send()
        mine.wait()

    return start, finish


def _exchange_copies(t_ref, out_ref, send_sems, recv_sems, local_sem):
    x, y, c = lax.axis_index("x"), lax.axis_index("y"), lax.axis_index("c")
    me = 4 * x + 2 * y + c

    def peer(k):
        return (1 - x if k & 4 else x), (1 - y if k & 2 else y), (1 - c if k & 1 else c)

    def copy(k, landed):
        px, py, pc = peer(k)
        p = 4 * px + 2 * py + pc
        return pltpu.make_async_remote_copy(
            src_ref=t_ref.at[p], dst_ref=out_ref.at[p if landed else me],
            send_sem=send_sems.at[k - 1], recv_sem=recv_sems.at[k - 1], device_id=(px, py, pc), device_id_type=MESH_ID)

    mine = pltpu.make_async_copy(t_ref.at[me], out_ref.at[me], local_sem)
    sends = [copy(k, False) for k in range(1, N_DEV)]

    def start():
        mine.start()
        for cp in sends:
            cp.start()

    def finish():
        for k in range(1, N_DEV):
            copy(k, True).wait_recv()
        for cp in sends:
            cp.wait_send()
        mine.wait()

    return start, finish


_COMM = {'gather': _gather_copies, 'exchange': _exchange_copies}


def _comm_out_shape(kind, operand):
    shape = (N_DEV,) + operand.shape if kind == 'gather' else operand.shape
    return jax.ShapeDtypeStruct(shape, operand.dtype)


def _comm_call(kind, operand, name):
    def body(x_ref, out_ref, send_sems, recv_sems, local_sem):
        start, finish = _COMM[kind](x_ref, out_ref, send_sems, recv_sems, local_sem)
        start()
        finish()

    return _pcall(body, name=name, out_shape=_comm_out_shape(kind, operand), in_specs=[_ANY], out_specs=_ANY,
                  scratch_shapes=_COMM_SCRATCH)(operand)


def _all_gather(block, name):
    return _comm_call('gather', block, name)


def _all_to_all(parts, name):
    return _comm_call('exchange', parts, name)


def _carry_call(body, comm, name, grid, in_specs, out_specs, out_shape, scratch_shapes, semantics, operands):
    if comm is None:
        outs = _pcall(
            body, name=name, grid=grid, in_specs=in_specs, out_specs=out_specs, out_shape=out_shape,
            scratch_shapes=scratch_shapes,
            compiler_params=pltpu.CompilerParams(dimension_semantics=semantics, vmem_limit_bytes=VMEM_LIMIT),
        )(*operands)
        return outs, None

    kind, operand = comm
    n_in, n_out = len(in_specs), len(out_specs)
    steps = 1
    for g in grid:
        steps *= g

    def carrying(*refs):
        ins, x_ref = refs[:n_in], refs[n_in]
        outs, out_ref = refs[n_in + 1:n_in + 1 + n_out], refs[n_in + 1 + n_out]
        scratch, sems = refs[n_in + 2 + n_out:len(refs) - 3], refs[len(refs) - 3:]
        start, finish = _COMM[kind](x_ref, out_ref, *sems)
        step = pl.program_id(0)
        for axis in range(1, len(grid)):
            step = step * grid[axis] + pl.program_id(axis)

        @pl.when(step == 0)
        def _():
            start()

        body(*ins, *outs, *scratch)

        @pl.when(step == steps - 1)
        def _():
            finish()

    outs = _pcall(
        carrying, name=name, grid=grid, in_specs=list(in_specs) + [_ANY], out_specs=list(out_specs) + [_ANY],
        out_shape=list(out_shape) + [_comm_out_shape(kind, operand)], scratch_shapes=list(scratch_shapes) + _COMM_SCRATCH,
        compiler_params=pltpu.CompilerParams(dimension_semantics=("arbitrary",) * len(grid), vmem_limit_bytes=VMEM_LIMIT),
    )(*operands, operand)
    return outs[:-1], outs[-1]


_LANE_TILES = (1024, 768, 512, 384, 256, 128)
_ROW_TILES = (1088, 1024, 512, 256, 128)
_DEPTH_TILES = (2048, 1408) + _LANE_TILES
_TOKEN_DEPTH_TILES = (2176,) + _ROW_TILES


def _pick(dim, cands):
    for t in cands:
        if dim % t == 0:
            return t
    return dim


def _mm(a, b, mode, out_dtype, name, comm=None):
    if mode == 'nn':
        (M, K), (_, N) = a.shape, b.shape
        tm, tn, tk = _pick(M, _ROW_TILES), _pick(N, _LANE_TILES), _pick(K, _DEPTH_TILES)
        a_spec = pl.BlockSpec((tm, tk), lambda i, j, k: (i, k))
        b_spec = pl.BlockSpec((tk, tn), lambda i, j, k: (k, j))
        dn = (((1,), (0,)), ((), ()))
    elif mode == 'nt':
        (M, K), (N, _) = a.shape, b.shape
        tm, tn, tk = _pick(M, _ROW_TILES), _pick(N, _LANE_TILES), _pick(K, _DEPTH_TILES)
        a_spec = pl.BlockSpec((tm, tk), lambda i, j, k: (i, k))
        b_spec = pl.BlockSpec((tn, tk), lambda i, j, k: (j, k))
        dn = (((1,), (1,)), ((), ()))
    else:
        (K, M), (_, N) = a.shape, b.shape
        tm, tn, tk = _pick(M, _LANE_TILES), _pick(N, _LANE_TILES), _pick(K, _TOKEN_DEPTH_TILES)
        a_spec = pl.BlockSpec((tk, tm), lambda i, j, k: (k, i))
        b_spec = pl.BlockSpec((tk, tn), lambda i, j, k: (k, j))
        dn = (((0,), (0,)), ((), ()))
    grid = (M // tm, N // tn, K // tk)

    def matmul_step(a_ref, b_ref, o_ref, acc):
        @pl.when(pl.program_id(2) == 0)
        def _():
            acc[...] = jnp.zeros_like(acc)

        acc[...] += lax.dot_general(a_ref[...], b_ref[...], dn, preferred_element_type=F32)

        @pl.when(pl.program_id(2) == grid[2] - 1)
        def _():
            o_ref[...] = acc[...].astype(o_ref.dtype)

    outs, moved = _carry_call(
        matmul_step, comm, name, grid, [a_spec, b_spec], [pl.BlockSpec((tm, tn), lambda i, j, k: (i, j))],
        [jax.ShapeDtypeStruct((M, N), out_dtype)], [pltpu.VMEM((tm, tn), F32)], ("parallel", "parallel", "arbitrary"),
        (a, b))
    return outs[0] if comm is None else (outs[0], moved)


def _proj_fwd(a16, w, next_shard, name):
    if next_shard is None:
        return _mm(a16, w, 'nn', F32, name + '_nn'), None
    return _mm(a16, w, 'nn', F32, name + '_nn_gather', comm=('gather', next_shard.astype(BF16)))


def _proj_bwd(a16, w, dy, d_gathered, name):
    dy16 = dy.astype(BF16)
    if d_gathered is None:
        return _mm(dy16, w, 'nt', F32, name + '_nt'), _mm(a16, dy16, 'tn', BF16, name + '_tn'), None
    half = d_gathered.shape[1] // 2
    dw, top = _mm(a16, dy16, 'tn', BF16, name + '_tn_exchange', comm=('exchange', d_gathered[:, :half]))
    da, low = _mm(dy16, w, 'nt', F32, name + '_nt_exchange', comm=('exchange', d_gathered[:, half:]))
    d_shard = jnp.concatenate([_sum_parts(top, name + '_sum_top'), _sum_parts(low, name + '_sum_low')], axis=0)
    return da, dw, d_shard


@functools.partial(jax.custom_vjp, nondiff_argnums=(3,))
def pmm(a, w, next_shard, name):
    return _proj_fwd(a.astype(BF16), w, next_shard, name)


def _pmm_fwd(a, w, next_shard, name):
    a16 = a.astype(BF16)
    return _proj_fwd(a16, w, next_shard, name), (a16, w)


def _pmm_bwd(name, res, cts):
    return _proj_bwd(*res, cts[0], cts[1], name)


pmm.defvjp(_pmm_fwd, _pmm_bwd)


FFN_TILE = 128


def _neighbours(x, n):
    T = x.shape[0]
    t = lax.broadcasted_iota(jnp.int32, x.shape, 0)
    prev = jnp.where((t == 0) | (t == n), 0.0, pltpu.roll(x, 1, 0))
    nxt = jnp.where((t == n - 1) | (t == T - 1), 0.0, pltpu.roll(x, T - 1, 0))
    return prev, nxt


def _gate_specs(rows, f):
    tiles = f // FFN_TILE
    return [pl.BlockSpec((rows, FFN_TILE), lambda j: (0, j)), pl.BlockSpec((rows, FFN_TILE), lambda j: (0, j + tiles))]


def _conv3(x, cw_ref, cb_ref, n):
    prev, nxt = _neighbours(x, n)
    return prev * cw_ref[0:1, :] + x * cw_ref[1:2, :] + nxt * cw_ref[2:3, :] + cb_ref[...], prev, nxt


def _gate_fwd_call(a, cw, cb, n):
    T, f = a.shape[0], a.shape[1] // 2

    def body(ag_ref, av_ref, wg_ref, wv_ref, bg_ref, bv_ref, u_ref):
        g = _conv3(ag_ref[...], wg_ref, bg_ref, n)[0]
        v = _conv3(av_ref[...], wv_ref, bv_ref, n)[0]
        u_ref[...] = (g * jax.nn.sigmoid(g) * v).astype(u_ref.dtype)

    return _pcall(
        body, name='ffn_gate_fwd', grid=(f // FFN_TILE,),
        in_specs=_gate_specs(T, f) + _gate_specs(3, f) + _gate_specs(1, f),
        out_specs=pl.BlockSpec((T, FFN_TILE), lambda j: (0, j)),
        out_shape=jax.ShapeDtypeStruct((T, f), BF16),
        compiler_params=pltpu.CompilerParams(dimension_semantics=("parallel",), vmem_limit_bytes=VMEM_LIMIT),
    )(a, a, cw, cw, cb, cb)


def _gate_bwd_call(a, cw, cb, du, n):
    T, f = a.shape[0], a.shape[1] // 2

    def half(dz, x, prev, nxt, w_ref, da_ref, dw_ref):
        t = lax.broadcasted_iota(jnp.int32, dz.shape, 0)
        from_next = pltpu.roll(jnp.where((t == 0) | (t == n), 0.0, dz), T - 1, 0)
        from_prev = pltpu.roll(jnp.where((t == n - 1) | (t == T - 1), 0.0, dz), 1, 0)
        da_ref[...] = (dz * w_ref[1:2, :] + from_next * w_ref[0:1, :] + from_prev * w_ref[2:3, :]).astype(da_ref.dtype)
        dw_ref[0:1, :] = jnp.sum(dz * prev, axis=0, keepdims=True)
        dw_ref[1:2, :] = jnp.sum(dz * x, axis=0, keepdims=True)
        dw_ref[2:3, :] = jnp.sum(dz * nxt, axis=0, keepdims=True)
        dw_ref[3:4, :] = jnp.sum(dz, axis=0, keepdims=True)
        dw_ref[4:8, :] = jnp.zeros((4, FFN_TILE), F32)

    def body(ag_ref, av_ref, wg_ref, wv_ref, bg_ref, bv_ref, du_ref, dag_ref, dav_ref, dwg_ref, dwv_ref):
        xg, xv = ag_ref[...], av_ref[...]
        g, g_prev, g_next = _conv3(xg, wg_ref, bg_ref, n)
        v, v_prev, v_next = _conv3(xv, wv_ref, bv_ref, n)
        sg = jax.nn.sigmoid(g)
        du = du_ref[...]
        half(du * v * (sg * (1.0 + g * (1.0 - sg))), xg, g_prev, g_next, wg_ref, dag_ref, dwg_ref)
        half(du * (g * sg), xv, v_prev, v_next, wv_ref, dav_ref, dwv_ref)

    tile = lambda rows: pl.BlockSpec((rows, FFN_TILE), lambda j: (0, j))
    return _pcall(
        body, name='ffn_gate_bwd', grid=(f // FFN_TILE,),
        in_specs=_gate_specs(T, f) + _gate_specs(3, f) + _gate_specs(1, f) + [tile(T)],
        out_specs=[tile(T), tile(T), tile(8), tile(8)],
        out_shape=[jax.ShapeDtypeStruct((T, f), BF16), jax.ShapeDtypeStruct((T, f), BF16),
                   jax.ShapeDtypeStruct((8, f), F32), jax.ShapeDtypeStruct((8, f), F32)],
        compiler_params=pltpu.CompilerParams(dimension_semantics=("parallel",), vmem_limit_bytes=VMEM_LIMIT),
    )(a, a, cw, cw, cb, cb, du)


@functools.partial(jax.custom_vjp, nondiff_argnums=(7,))
def conv_ffn(h, w_up, cw, cb, w_down, next_up, next_down, n):
    return _conv_ffn_fwd(h, w_up, cw, cb, w_down, next_up, next_down, n)[0]


def _conv_ffn_fwd(h, w_up, cw, cb, w_down, next_up, next_down, n):
    h16 = h.astype(BF16)
    a, g_up = _proj_fwd(h16, w_up, next_up, 'ffn_up')
    u16 = _gate_fwd_call(a, cw, cb.reshape(1, -1), n)
    y, g_down = _proj_fwd(u16, w_down, next_down, 'ffn_down')
    return (y, g_up, g_down), (h16, w_up, cw, cb, w_down, a, u16)


def _conv_ffn_bwd(n, res, cts):
    h16, w_up, cw, cb, w_down, a, u16 = res
    dy, d_g_up, d_g_down = cts
    du, dw_down, d_next_down = _proj_bwd(u16, w_down, dy, d_g_down, 'ffn_down')
    dag, dav, dwg, dwv = _gate_bwd_call(a, cw, cb.reshape(1, -1), du, n)
    dcw = jnp.concatenate([dwg, dwv], axis=1)
    dh, dw_up, d_next_up = _proj_bwd(h16, w_up, jnp.concatenate([dag, dav], axis=1), d_g_up, 'ffn_up')
    return dh, dw_up, dcw[:3], dcw[3], dw_down, d_next_up, d_next_down


conv_ffn.defvjp(_conv_ffn_fwd, _conv_ffn_bwd)


def _attn_fwd_call(q, k, v, scale, comm=None):
    H, nq, dq = q.shape
    nk, dv = v.shape[1], v.shape[2]
    tq = _pick(nq, (256, 128))

    def body(q_ref, k_ref, v_ref, o_ref, lse_ref):
        s = lax.dot_general(q_ref[0], k_ref[0], (((1,), (1,)), ((), ())), preferred_element_type=F32) * scale
        m = jnp.max(s, axis=1, keepdims=True)
        p = jnp.exp(s - m)
        l = jnp.sum(p, axis=1, keepdims=True)
        pn = (p * (1.0 / l)).astype(BF16)
        o_ref[...] = jnp.dot(pn, v_ref[0], preferred_element_type=F32)
        lse_ref[0] = m + jnp.log(l)

    return _carry_call(
        body, comm, 'mla_attn_fwd', (H, nq // tq),
        [pl.BlockSpec((1, tq, dq), lambda h, i: (h, i, 0)),
         pl.BlockSpec((1, nk, dq), lambda h, i: (h, 0, 0)),
         pl.BlockSpec((1, nk, dv), lambda h, i: (h, 0, 0))],
        [pl.BlockSpec((tq, dv), lambda h, i: (i, h)), pl.BlockSpec((1, tq, 1), lambda h, i: (h, i, 0))],
        [jax.ShapeDtypeStruct((nq, H * dv), F32), jax.ShapeDtypeStruct((H, nq, 1), F32)], [],
        ("parallel", "parallel"), (q, k, v))


def _attn_bwd_call(q, k, v, o, lse, do, scale, comm=None):
    H, nq, dq = q.shape
    nk, dv = v.shape[1], v.shape[2]
    tq = _pick(nq, (128,))

    def body(q_ref, k_ref, v_ref, o_ref, lse_ref, do_ref, dq_ref, dk_ref, dv_ref):
        @pl.when(pl.program_id(1) == 0)
        def _():
            dk_ref[...] = jnp.zeros_like(dk_ref)
            dv_ref[...] = jnp.zeros_like(dv_ref)

        q16, k16, v16 = q_ref[0], k_ref[0], v_ref[0]
        do = do_ref[...]
        do16 = do.astype(BF16)
        s = lax.dot_general(q16, k16, (((1,), (1,)), ((), ())), preferred_element_type=F32) * scale
        p = jnp.exp(s - lse_ref[0])
        dv_ref[0] += lax.dot_general(p.astype(BF16), do16, (((0,), (0,)), ((), ())), preferred_element_type=F32)
        dp = lax.dot_general(do16, v16, (((1,), (1,)), ((), ())), preferred_element_type=F32)
        delta = jnp.sum(do * o_ref[...], axis=1, keepdims=True)
        ds16 = (p * (dp - delta) * scale).astype(BF16)
        dq_ref[0] = jnp.dot(ds16, k16, preferred_element_type=F32)
        dk_ref[0] += lax.dot_general(ds16, q16, (((0,), (0,)), ((), ())), preferred_element_type=F32)

    return _carry_call(
        body, comm, 'mla_attn_bwd', (H, nq // tq),
        [pl.BlockSpec((1, tq, dq), lambda h, i: (h, i, 0)),
         pl.BlockSpec((1, nk, dq), lambda h, i: (h, 0, 0)),
         pl.BlockSpec((1, nk, dv), lambda h, i: (h, 0, 0)),
         pl.BlockSpec((tq, dv), lambda h, i: (i, h)),
         pl.BlockSpec((1, tq, 1), lambda h, i: (h, i, 0)),
         pl.BlockSpec((tq, dv), lambda h, i: (i, h))],
        [pl.BlockSpec((1, tq, dq), lambda h, i: (h, i, 0)),
         pl.BlockSpec((1, nk, dq), lambda h, i: (h, 0, 0)),
         pl.BlockSpec((1, nk, dv), lambda h, i: (h, 0, 0))],
        [jax.ShapeDtypeStruct((H, nq, dq), F32), jax.ShapeDtypeStruct((H, nk, dq), F32),
         jax.ShapeDtypeStruct((H, nk, dv), F32)], [], ("parallel", "arbitrary"), (q, k, v, o, lse, do))


def _gather_comm(shard):
    return None if shard is None else ('gather', shard.astype(BF16))


def _exchange_comm(d_gathered):
    return None if d_gathered is None else ('exchange', d_gathered)


def _reduce_moved(parts, name):
    return None if parts is None else _sum_parts(parts, name)


@functools.partial(jax.custom_vjp, nondiff_argnums=(4,))
def attention(q, k, v, shard, scale):
    (o, _), gathered = _attn_fwd_call(q.astype(BF16), k.astype(BF16), v.astype(BF16), scale, _gather_comm(shard))
    return o, gathered


def _attention_fwd(q, k, v, shard, scale):
    q16, k16, v16 = q.astype(BF16), k.astype(BF16), v.astype(BF16)
    (o, lse), gathered = _attn_fwd_call(q16, k16, v16, scale, _gather_comm(shard))
    return (o, gathered), (q16, k16, v16, o, lse)


def _attention_bwd(scale, res, cts):
    q16, k16, v16, o, lse = res
    grads, parts = _attn_bwd_call(q16, k16, v16, o, lse, cts[0], scale, _exchange_comm(cts[1]))
    return tuple(grads) + (_reduce_moved(parts, 'mla_attn_sum'),)


attention.defvjp(_attention_fwd, _attention_bwd)


def _win_geometry(kind, n):
    if kind == 'na':
        rows = n // GRID_W
        kh = min(NA_KH, rows)

        def start(i):
            return jnp.clip(i - kh // 2, 0, rows - kh) * GRID_W

        def bidx(i):
            return jnp.clip(i - kh // 2, 0, rows - kh) - i + (NA_KH - 1)

        return GRID_W, kh * GRID_W, start, bidx
    nb = n // SWA_BLOCK

    def start(i):
        return i * SWA_BLOCK

    def bidx(i):
        return jnp.where(i == 0, 0, jnp.where(i == nb - 1, 2, 1))

    return SWA_BLOCK, 3 * SWA_BLOCK, start, bidx


def _dot_nt(a, b):
    return lax.dot_general(a, b, (((1,), (1,)), ((), ())), preferred_element_type=F32)


def _dot_tn(a, b):
    return lax.dot_general(a, b, (((0,), (0,)), ((), ())), preferred_element_type=F32)


def _win_specs(q, k, kc, bias):
    hq, n, d = q.shape
    grp = hq // k.shape[0]
    hb = bias.shape[0]
    return [
        pl.BlockSpec((1, n, d), lambda h: (h, 0, 0)),
        pl.BlockSpec((1,) + k.shape[1:], lambda h: (h // grp, 0, 0)),
        pl.BlockSpec((1,) + k.shape[1:], lambda h: (h // grp, 0, 0)),
        pl.BlockSpec((1,) + kc.shape[1:], lambda h: (h // grp, 0, 0)),
        pl.BlockSpec((1,) + kc.shape[1:], lambda h: (h // grp, 0, 0)),
        pl.BlockSpec((1,) + bias.shape[1:], (lambda h: (h, 0, 0, 0)) if hb > 1 else (lambda h: (0, 0, 0, 0))),
        pl.BlockSpec(memory_space=pltpu.SMEM),
    ]


def _win_fwd_call(q, k, v, kc, vc, bias, sink, kind, comm=None):
    hq, n, d = q.shape
    scale = d ** -0.5
    qb, wk, start, bidx = _win_geometry(kind, n)
    has_sink = kind == 'swa'

    def body(q_ref, k_ref, v_ref, kc_ref, vc_ref, b_ref, sink_ref, o_ref, lse_ref):
        kc16, vc16 = kc_ref[0], vc_ref[0]
        snk = sink_ref[pl.program_id(0)]

        def step(i, carry):
            qs = pl.multiple_of(i * qb, qb)
            ks = pl.multiple_of(start(i), GRID_W)
            q16 = q_ref[0, pl.ds(qs, qb), :]
            s1 = _dot_nt(q16, k_ref[0, pl.ds(ks, wk), :]) * scale + b_ref[0, bidx(i)]
            s2 = _dot_nt(q16, kc16) * scale
            m = jnp.maximum(jnp.max(s1, axis=1, keepdims=True), jnp.max(s2, axis=1, keepdims=True))
            if has_sink:
                m = jnp.maximum(m, snk)
            p1 = jnp.exp(s1 - m)
            p2 = jnp.exp(s2 - m)
            l = jnp.sum(p1, axis=1, keepdims=True) + jnp.sum(p2, axis=1, keepdims=True)
            if has_sink:
                l = l + jnp.exp(snk - m)
            inv = 1.0 / l
            o = (jnp.dot((p1 * inv).astype(BF16), v_ref[0, pl.ds(ks, wk), :], preferred_element_type=F32)
                 + jnp.dot((p2 * inv).astype(BF16), vc16, preferred_element_type=F32))
            o_ref[0, pl.ds(qs, qb), :] = o
            lse_ref[0, pl.ds(qs, qb), :] = m + jnp.log(l)
            return carry

        lax.fori_loop(0, n // qb, step, 0, unroll=4)

    return _carry_call(
        body, comm, kind + '_attn_fwd', (hq,), _win_specs(q, k, kc, bias),
        [pl.BlockSpec((1, n, d), lambda h: (h, 0, 0)), pl.BlockSpec((1, n, 1), lambda h: (h, 0, 0))],
        [jax.ShapeDtypeStruct((hq, n, d), F32), jax.ShapeDtypeStruct((hq, n, 1), F32)], [], ("parallel",),
        (q, k, v, kc, vc, bias, sink))


def _win_bwd_call(q, k, v, kc, vc, bias, sink, o, lse, do, kind, comm=None):
    hq, n, d = q.shape
    scale = d ** -0.5
    qb, wk, start, bidx = _win_geometry(kind, n)
    has_sink = kind == 'swa'
    bias_grad = kind == 'na'

    def body(q_ref, k_ref, v_ref, kc_ref, vc_ref, b_ref, sink_ref, o_ref, lse_ref, do_ref,
             dq_ref, dk_ref, dv_ref, dkc_ref, dvc_ref, db_ref, dsink_ref):
        kc16, vc16 = kc_ref[0], vc_ref[0]
        snk = sink_ref[pl.program_id(0)]
        dk_ref[...] = jnp.zeros_like(dk_ref)
        dv_ref[...] = jnp.zeros_like(dv_ref)
        dkc_ref[...] = jnp.zeros_like(dkc_ref)
        dvc_ref[...] = jnp.zeros_like(dvc_ref)
        db_ref[...] = jnp.zeros_like(db_ref)

        def step(i, dsink):
            qs = pl.multiple_of(i * qb, qb)
            ks = pl.multiple_of(start(i), GRID_W)
            q16 = q_ref[0, pl.ds(qs, qb), :]
            k16 = k_ref[0, pl.ds(ks, wk), :]
            v16 = v_ref[0, pl.ds(ks, wk), :]
            lse = lse_ref[0, pl.ds(qs, qb), :]
            do = do_ref[0, pl.ds(qs, qb), :]
            do16 = do.astype(BF16)
            p1 = jnp.exp(_dot_nt(q16, k16) * scale + b_ref[0, bidx(i)] - lse)
            p2 = jnp.exp(_dot_nt(q16, kc16) * scale - lse)
            delta = jnp.sum(do * o_ref[0, pl.ds(qs, qb), :], axis=1, keepdims=True)
            ds1 = p1 * (_dot_nt(do16, v16) - delta)
            ds2 = p2 * (_dot_nt(do16, vc16) - delta)
            if bias_grad:
                db_ref[0, bidx(i)] += ds1
            ds1 = (ds1 * scale).astype(BF16)
            ds2 = (ds2 * scale).astype(BF16)
            dq_ref[0, pl.ds(qs, qb), :] = (jnp.dot(ds1, k16, preferred_element_type=F32)
                                          + jnp.dot(ds2, kc16, preferred_element_type=F32))
            dk_ref[0, pl.ds(ks, wk), :] += _dot_tn(ds1, q16)
            dv_ref[0, pl.ds(ks, wk), :] += _dot_tn(p1.astype(BF16), do16)
            dkc_ref[0] += _dot_tn(ds2, q16)
            dvc_ref[0] += _dot_tn(p2.astype(BF16), do16)
            if has_sink:
                dsink = dsink - jnp.sum(jnp.exp(snk - lse) * delta)
            return dsink

        dsink = lax.fori_loop(0, n // qb, step, jnp.zeros((), F32), unroll=2)
        dsink_ref[...] = jnp.full(dsink_ref.shape, dsink, F32)

    per_head = lambda shape: pl.BlockSpec((1,) + shape[1:], lambda h: (h,) + (0,) * (len(shape) - 1))
    kq = (hq,) + k.shape[1:]
    cq = (hq,) + kc.shape[1:]
    bq = (hq,) + bias.shape[1:]
    in_specs = _win_specs(q, k, kc, bias) + [per_head(o.shape), per_head(lse.shape), per_head(do.shape)]
    out_shapes = [q.shape, kq, kq, cq, cq, bq, (hq, 8, 128)]
    return _carry_call(
        body, comm, kind + '_attn_bwd', (hq,), in_specs, [per_head(s) for s in out_shapes],
        [jax.ShapeDtypeStruct(s, F32) for s in out_shapes], [], ("parallel",),
        (q, k, v, kc, vc, bias, sink, o, lse, do))


@functools.partial(jax.custom_vjp, nondiff_argnums=(8,))
def win_attention(q, k, v, kc, vc, bias, sink, shard, kind):
    b16 = lambda t: t.astype(BF16)
    (o, _), gathered = _win_fwd_call(b16(q), b16(k), b16(v), b16(kc), b16(vc), bias, sink, kind, _gather_comm(shard))
    return o, gathered


def _win_attention_fwd(q, k, v, kc, vc, bias, sink, shard, kind):
    res = tuple(t.astype(BF16) for t in (q, k, v, kc, vc)) + (bias, sink)
    (o, lse), gathered = _win_fwd_call(*res, kind, _gather_comm(shard))
    return (o, gathered), res + (o, lse)


def _win_attention_bwd(kind, res, cts):
    q, k, v, kc, vc, bias, sink, o, lse = res
    (dq, dk, dv, dkc, dvc, db, dsink), parts = _win_bwd_call(
        q, k, v, kc, vc, bias, sink, o, lse, cts[0], kind, _exchange_comm(cts[1]))
    hkv = k.shape[0]
    fold = lambda t: t.reshape((hkv, -1) + t.shape[1:]).sum(axis=1)
    if bias.shape[0] == 1:
        db = jnp.zeros_like(bias)
    return (dq, fold(dk), fold(dv), fold(dkc), fold(dvc), db, dsink[:, 0, 0], _reduce_moved(parts, kind + '_attn_sum'))


win_attention.defvjp(_win_attention_fwd, _win_attention_bwd)


def _loss_head(y, target):
    n, d = y.shape
    tr = _pick(n, (512, 256, 128))
    nb = n // tr

    def body(y_ref, t_ref, dy_ref, part_ref):
        err = y_ref[...] - t_ref[...]
        dy_ref[...] = err * (1.0 / d)
        part_ref[...] = jnp.full(part_ref.shape, jnp.sum(err * err), F32)

    dy, part = _pcall(
        body, name='loss_head', grid=(nb,),
        in_specs=[pl.BlockSpec((tr, d), lambda i: (i, 0)), pl.BlockSpec((tr, d), lambda i: (i, 0))],
        out_specs=[pl.BlockSpec((tr, d), lambda i: (i, 0)), pl.BlockSpec((1, 8, 128), lambda i: (i, 0, 0))],
        out_shape=[jax.ShapeDtypeStruct((n, d), F32), jax.ShapeDtypeStruct((nb, 8, 128), F32)],
        compiler_params=pltpu.CompilerParams(dimension_semantics=("parallel",), vmem_limit_bytes=VMEM_LIMIT),
    )(y, target)
    return 0.5 * jnp.sum(part[:, 0, 0]) / d, dy


def _sum_parts(parts, name):
    P, R, C = parts.shape
    tr = _pick(R, (256, 128, 64, 32, 16, 8))

    def body(p_ref, o_ref):
        acc = p_ref[0].astype(F32)
        for i in range(1, P):
            acc = acc + p_ref[i].astype(F32)
        o_ref[...] = acc

    return _pcall(
        body, name=name, grid=(R // tr,),
        in_specs=[pl.BlockSpec((P, tr, C), lambda i: (0, i, 0))],
        out_specs=pl.BlockSpec((tr, C), lambda i: (i, 0)),
        out_shape=jax.ShapeDtypeStruct((R, C), F32),
        compiler_params=pltpu.CompilerParams(dimension_semantics=("parallel",), vmem_limit_bytes=VMEM_LIMIT),
    )(parts)


def _adamw(parts, w, m, v, name):
    P, R, C = parts.shape
    tr = _pick(R, (128, 64, 32, 16, 8))
    c1 = 1.0 / (1.0 - ADAM_B1 ** ADAM_STEP)
    c2 = 1.0 / (1.0 - ADAM_B2 ** ADAM_STEP)

    def body(p_ref, w_ref, m_ref, v_ref, g_out, d_out, m_out, v_out):
        g = p_ref[0].astype(F32)
        for i in range(1, P):
            g = g + p_ref[i].astype(F32)
        m_new = ADAM_B1 * m_ref[...] + (1.0 - ADAM_B1) * g
        v_new = ADAM_B2 * v_ref[...] + (1.0 - ADAM_B2) * (g * g)
        g_out[...] = g
        m_out[...] = m_new
        v_out[...] = v_new
        d_out[...] = -ADAM_LR * ((m_new * c1) / (jnp.sqrt(v_new * c2) + ADAM_EPS) + ADAM_WD * w_ref[...])

    blk = pl.BlockSpec((tr, C), lambda i: (i, 0))
    return _pcall(
        body, name=name, grid=(R // tr,),
        in_specs=[pl.BlockSpec((P, tr, C), lambda i: (0, i, 0)), blk, blk, blk],
        out_specs=[blk, blk, blk, blk],
        out_shape=[jax.ShapeDtypeStruct((R, C), F32)] * 4,
        compiler_params=pltpu.CompilerParams(dimension_semantics=("parallel",), vmem_limit_bytes=VMEM_LIMIT),
    )(parts, w, m, v)


def rms_norm(x, g):
    return x * lax.rsqrt(jnp.mean(x * x, axis=-1, keepdims=True) + NORM_EPS) * g


def modulate(h, shift, scale):
    return h * (1.0 + scale) + shift


def axial_angles(n, d_rot):
    t = jnp.arange(n)
    row = (t // GRID_W).astype(F32)
    col = (t % GRID_W).astype(F32)
    d_axis = d_rot // 2
    inv_freq = ROPE_BASE ** (-jnp.arange(0, d_axis, 2, dtype=F32) / d_axis)
    return (row[:, None] * inv_freq, col[:, None] * inv_freq)


def rope_segment(x, ang):
    cos = jnp.cos(ang)[:, None, :]
    sin = jnp.sin(ang)[:, None, :]
    x1, x2 = jnp.split(x, 2, axis=-1)
    return jnp.concatenate([x1 * cos - x2 * sin, x2 * cos + x1 * sin], axis=-1)


def axial_rope(x, ang):
    half = x.shape[-1] // 2
    return jnp.concatenate([rope_segment(x[..., :half], ang[0]), rope_segment(x[..., half:], ang[1])], axis=-1)


def rope_latent(t, n, ang):
    return jnp.concatenate([axial_rope(t[:n], ang), t[n:]], axis=0)


def mla_attend_ctx(q_nope, q_rope, k_nope, k_rope, v):
    s = (jnp.einsum('qhd,khd->hqk', q_nope, k_nope, preferred_element_type=F32)
         + jnp.einsum('qhr,kr->hqk', q_rope, k_rope, preferred_element_type=F32))
    p = jax.nn.softmax(s * MLA_SCALE, axis=-1)
    return jnp.einsum('hqk,khd->qhd', p, v)


def pool_mixer(u, w_pool, scale):
    n = u.shape[0]
    csum = jnp.pad(jnp.cumsum(u, axis=0), ((1, 0), (0, 0)))
    t = jnp.arange(n)
    diffs = []
    for g, w in enumerate(POOL_WINDOWS):
        sl = slice(g * POOL_GROUP, (g + 1) * POOL_GROUP)
        lo = jnp.clip(t - w // 2, 0, n)
        hi = jnp.clip(t + w // 2, 0, n)
        cs = csum[:, sl]
        mean = (cs[hi] - cs[lo]) / (hi - lo).astype(F32)[:, None]
        diffs.append(mean - u[:, sl])
    d = jnp.stack(diffs, axis=1)
    y = jnp.einsum('ngc,gcd->ngd', d, w_pool).reshape(n, POOL_WIDTH)
    return y * scale


def swa_latent(q, k, v, k_ctx, v_ctx, sink, shard):
    n, hq, d = q.shape
    blk = SWA_BLOCK
    a = jnp.arange(blk)[:, None]
    j = jnp.arange(3 * blk)[None, :]
    near = jnp.abs(j - blk - a) <= SWA_WINDOW
    tiles = jnp.stack([near & (j >= blk), near, near & (j < 2 * blk)])
    bias = jnp.where(tiles, 0.0, NEG_INF).astype(F32)[None]
    heads = lambda t: t.transpose(1, 0, 2)
    pad = lambda t: jnp.pad(heads(t), ((0, 0), (blk, blk), (0, 0)))
    o, gathered = win_attention(heads(q), pad(k), pad(v), heads(k_ctx), heads(v_ctx), bias, sink, shard, 'swa')
    return o.transpose(1, 0, 2).reshape(n, hq * d), gathered


def ctx_attention(q, k, v, sink):
    nq, hq, d = q.shape
    hkv = k.shape[1]
    grp = hq // hkv
    nk = k.shape[0]
    qg = q.reshape(nq, hkv, grp, d)
    s = jnp.einsum('qhgd,khd->hgqk', qg, k, preferred_element_type=F32) * (d ** -0.5)
    if sink is not None:
        s_sink = jnp.broadcast_to(sink.reshape(hkv, grp)[:, :, None, None], s.shape[:-1] + (1,))
        s = jnp.concatenate([s, s_sink], axis=-1)
    p = jax.nn.softmax(s, axis=-1)[..., :nk]
    o = jnp.einsum('hgqk,khd->qhgd', p, v)
    return o.reshape(nq, hq * d)


def na_bias_tiles(rpb, n):
    rows = n // GRID_W
    kh = min(NA_KH, rows)
    qc = jnp.arange(GRID_W)[:, None]
    kc = jnp.arange(GRID_W)[None, :]
    dc = jnp.clip(kc - qc, 1 - NA_KW, NA_KW - 1) + (NA_KW - 1)
    onehot = (dc[None] == jnp.arange(2 * NA_KW - 1)[:, None, None]).astype(F32)
    toeplitz = jnp.einsum('hdt,tqk->hdqk', rpb, onehot, precision=lax.Precision.HIGHEST)
    q_col0 = jnp.clip(qc - NA_KW // 2, 0, GRID_W - NA_KW)
    valid = (kc >= q_col0) & (kc < q_col0 + NA_KW)
    masked = jnp.where(valid, toeplitz, NEG_INF)
    return jnp.stack([jnp.concatenate([masked[:, off + j] for j in range(kh)], axis=-1) for off in range(NA_KH)], axis=1)


def na_latent(q, k, v, k_ctx, v_ctx, rpb, shard):
    n, h, d = q.shape
    heads = lambda t: t.transpose(1, 0, 2)
    o, gathered = win_attention(heads(q), heads(k), heads(v), heads(k_ctx), heads(v_ctx), na_bias_tiles(rpb, n),
                                jnp.zeros((h,), F32), shard, 'na')
    return o.transpose(1, 0, 2).reshape(n, h * d), gathered


def _assemble(g, name):
    _, r, c = g.shape
    w = g.reshape(N_DEV * r, c) if BIG[name] == 1 else g.transpose(1, 0, 2).reshape(r, N_DEV * c)
    if name == 'w_in':
        w = jnp.pad(w, ((0, 0), (0, IN_COLS_PAD - IN_COLS)))
    return w


LAYER0_CARRIERS = {'mla': 'ffn_w_up', 'na': 'ffn_w_down', 'swa': 'w_out'}


def _forward(x, mod_x, mod_c, gathered0, shards, ws, ctx):
    n = x.shape[0]
    depth = mod_x.shape[0]
    ang_mla = axial_angles(n, MLA_ROPE)
    ang_swa = axial_angles(n, SWA_HEAD_DIM)
    gathered = dict(gathered0)
    for l in range(depth):
        update_ctx = l < depth - 1
        cur, gathered = gathered, {}
        nxt = {name: shards[name][l + 1] if update_ctx else None for name in BIG}
        carried = {kind: shards[name][0] if l == 0 else None for kind, name in LAYER0_CARRIERS.items()}

        def weight(name):
            return _assemble(cur[name], name)

        def proj(a, name):
            y, gathered[name] = pmm(a, weight(name), nxt[name], name)
            return y

        sh_m, sc_m, gt_m, sh_f, sc_f, gt_f = jnp.split(mod_x[l], 6)
        csh_m, csc_m, cgt_m, csh_f, csc_f, cgt_f = jnp.split(mod_c[l], 6)

        h_all = jnp.concatenate([modulate(rms_norm(x, ws['g_mix'][l]), sh_m, sc_m),
                                 modulate(rms_norm(ctx, ws['g_mix'][l]), csh_m, csc_m)], axis=0)
        p_all = proj(h_all, 'w_in')[:, :IN_COLS]
        mla_p, pool_p, swa_p, na_p = jnp.split(p_all, IN_SPLITS, axis=-1)
        T = p_all.shape[0]

        cq, ckv, kr = jnp.split(mla_p, [MLA_Q_LORA, MLA_Q_LORA + MLA_KV_LORA], axis=-1)
        q = proj(rms_norm(cq, ws['mla_q_a_norm'][l]), 'mla_w_qb').reshape(T, MLA_HEADS, MLA_NOPE + MLA_ROPE)
        kv = proj(rms_norm(ckv, ws['mla_kv_a_norm'][l]), 'mla_w_kvb').reshape(T, MLA_HEADS, MLA_NOPE + MLA_V)
        q_nope = rms_norm(q[..., :MLA_NOPE], ws['mla_q_nope_norm'][l])
        q_rope = rope_latent(rms_norm(q[..., MLA_NOPE:], ws['mla_q_rope_norm'][l]), n, ang_mla)
        k_nope = rms_norm(kv[..., :MLA_NOPE], ws['mla_k_nope_norm'][l])
        v_mla = kv[..., MLA_NOPE:]
        k_rope = rope_latent(rms_norm(kr, ws['mla_k_rope_norm'][l])[:, None, :], n, ang_mla)
        q_cat = jnp.concatenate([q_nope, q_rope], axis=-1).transpose(1, 0, 2)
        k_cat = jnp.concatenate([k_nope, jnp.broadcast_to(k_rope, (T, MLA_HEADS, MLA_ROPE))], axis=-1).transpose(1, 0, 2)
        out_a, moved = attention(q_cat[:, :n], k_cat, v_mla.transpose(1, 0, 2), carried['mla'], MLA_SCALE)
        if l == 0:
            cur[LAYER0_CARRIERS['mla']] = moved

        out_b = pool_mixer(pool_p[:n], ws['pool_w'][l], ws['pool_scale'][l])

        sq, sk, sv = jnp.split(swa_p, [SWA_HEADS * SWA_HEAD_DIM, (SWA_HEADS + SWA_KV_HEADS) * SWA_HEAD_DIM], axis=-1)
        sq = rope_latent(rms_norm(sq.reshape(T, SWA_HEADS, SWA_HEAD_DIM), ws['swa_q_norm'][l]), n, ang_swa)
        sk = rope_latent(rms_norm(sk.reshape(T, SWA_KV_HEADS, SWA_HEAD_DIM), ws['swa_k_norm'][l]), n, ang_swa)
        sv = sv.reshape(T, SWA_KV_HEADS, SWA_HEAD_DIM)
        out_c, moved = swa_latent(sq[:n], sk[:n], sv[:n], sk[n:], sv[n:], ws['swa_sink'][l], carried['swa'])
        if l == 0:
            cur[LAYER0_CARRIERS['swa']] = moved

        nq_, nk_, nv_ = jnp.split(na_p, 3, axis=-1)
        nq_ = rms_norm(nq_.reshape(T, NA_HEADS, NA_HEAD_DIM), ws['na_q_norm'][l])
        nk_ = rms_norm(nk_.reshape(T, NA_HEADS, NA_HEAD_DIM), ws['na_k_norm'][l])
        nv_ = nv_.reshape(T, NA_HEADS, NA_HEAD_DIM)
        out_d, moved = na_latent(nq_[:n], nk_[:n], nv_[:n], nk_[n:], nv_[n:], ws['na_rpb'][l], carried['na'])
        if l == 0:
            cur[LAYER0_CARRIERS['na']] = moved

        mix_x = jnp.concatenate([out_a, out_b, out_c, out_d], axis=-1)

        def ffn(h):
            y, gathered['ffn_w_up'], gathered['ffn_w_down'] = conv_ffn(
                h, weight('ffn_w_up'), ws['ffn_conv_w'][l], ws['ffn_conv_b'][l], weight('ffn_w_down'),
                nxt['ffn_w_up'], nxt['ffn_w_down'], n)
            return y

        if update_ctx:
            L = T - n
            mix_c = jnp.concatenate([
                mla_attend_ctx(q_nope[n:], q_rope[n:], k_nope[n:], k_rope[n:, 0], v_mla[n:]).reshape(L, MLA_HEADS * MLA_V),
                pool_mixer(pool_p[n:], ws['pool_w'][l], ws['pool_scale'][l]),
                ctx_attention(sq[n:], sk[n:], sv[n:], ws['swa_sink'][l]),
                ctx_attention(nq_[n:], nk_[n:], nv_[n:], None),
            ], axis=-1)
            o_all = proj(jnp.concatenate([mix_x, mix_c], axis=0), 'w_out')
            x = x + gt_m * o_all[:n]
            ctx = ctx + cgt_m * o_all[n:]
            h2 = jnp.concatenate([modulate(rms_norm(x, ws['g_ffn'][l]), sh_f, sc_f),
                                  modulate(rms_norm(ctx, ws['g_ffn'][l]), csh_f, csc_f)], axis=0)
            f_all = ffn(h2)
            x = x + gt_f * f_all[:n]
            ctx = ctx + cgt_f * f_all[n:]
        else:
            x = x + gt_m * proj(mix_x, 'w_out')
            x = x + gt_f * ffn(modulate(rms_norm(x, ws['g_ffn'][l]), sh_f, sc_f))
    return x


def _pack_rows(vecs):
    flat = jnp.concatenate([v.reshape(-1).astype(F32) for v in vecs])
    pad = (-flat.shape[0]) % 1024
    return jnp.pad(flat, (0, pad)).reshape(-1, 128)


def _unpack(flat, shapes):
    out, off = [], 0
    for s in shapes:
        size = 1
        for d in s:
            size *= d
        out.append(flat[off:off + size].reshape(s))
        off += size
    return out


def _silu_grad(z):
    s = jax.nn.sigmoid(z)
    return s * (1.0 + z * (1.0 - s))


def kernel(x, c, ctx, c_ctx, w_mod, b_mod, g_mix, g_ffn, w_in, w_out, mla_q_a_norm, mla_w_qb, mla_kv_a_norm, mla_w_kvb, mla_q_nope_norm, mla_q_rope_norm, mla_k_nope_norm, mla_k_rope_norm, pool_w, pool_scale, swa_q_norm, swa_k_norm, swa_sink, na_q_norm, na_k_norm, na_rpb, ffn_w_up, ffn_conv_w, ffn_conv_b, ffn_w_down, loss_target, m_c_ctx, m_w_mod, m_b_mod, m_g_mix, m_g_ffn, m_w_in, m_w_out, m_mla_q_a_norm, m_mla_w_qb, m_mla_kv_a_norm, m_mla_w_kvb, m_mla_q_nope_norm, m_mla_q_rope_norm, m_mla_k_nope_norm, m_mla_k_rope_norm, m_pool_w, m_pool_scale, m_swa_q_norm, m_swa_k_norm, m_swa_sink, m_na_q_norm, m_na_k_norm, m_na_rpb, m_ffn_w_up, m_ffn_conv_w, m_ffn_conv_b, m_ffn_w_down, v_c_ctx, v_w_mod, v_b_mod, v_g_mix, v_g_ffn, v_w_in, v_w_out, v_mla_q_a_norm, v_mla_w_qb, v_mla_kv_a_norm, v_mla_w_kvb, v_mla_q_nope_norm, v_mla_q_rope_norm, v_mla_k_nope_norm, v_mla_k_rope_norm, v_pool_w, v_pool_scale, v_swa_q_norm, v_swa_k_norm, v_swa_sink, v_na_q_norm, v_na_k_norm, v_na_rpb, v_ffn_w_up, v_ffn_conv_w, v_ffn_conv_b, v_ffn_w_down):
    return _step(x, c, ctx, c_ctx, w_mod, b_mod, g_mix, g_ffn, w_in, w_out, mla_q_a_norm, mla_w_qb, mla_kv_a_norm, mla_w_kvb, mla_q_nope_norm, mla_q_rope_norm, mla_k_nope_norm, mla_k_rope_norm, pool_w, pool_scale, swa_q_norm, swa_k_norm, swa_sink, na_q_norm, na_k_norm, na_rpb, ffn_w_up, ffn_conv_w, ffn_conv_b, ffn_w_down, loss_target, m_c_ctx, m_w_mod, m_b_mod, m_g_mix, m_g_ffn, m_w_in, m_w_out, m_mla_q_a_norm, m_mla_w_qb, m_mla_kv_a_norm, m_mla_w_kvb, m_mla_q_nope_norm, m_mla_q_rope_norm, m_mla_k_nope_norm, m_mla_k_rope_norm, m_pool_w, m_pool_scale, m_swa_q_norm, m_swa_k_norm, m_swa_sink, m_na_q_norm, m_na_k_norm, m_na_rpb, m_ffn_w_up, m_ffn_conv_w, m_ffn_conv_b, m_ffn_w_down, v_c_ctx, v_w_mod, v_b_mod, v_g_mix, v_g_ffn, v_w_in, v_w_out, v_mla_q_a_norm, v_mla_w_qb, v_mla_kv_a_norm, v_mla_w_kvb, v_mla_q_nope_norm, v_mla_q_rope_norm, v_mla_k_nope_norm, v_mla_k_rope_norm, v_pool_w, v_pool_scale, v_swa_q_norm, v_swa_k_norm, v_swa_sink, v_na_q_norm, v_na_k_norm, v_na_rpb, v_ffn_w_up, v_ffn_conv_w, v_ffn_conv_b, v_ffn_w_down)


def _step(*args):
    n_in = len(ARG_NAMES)
    n_w = len(WEIGHTS)
    given = dict(zip(ARG_NAMES, args[:n_in]))
    mom = dict(zip(WEIGHTS, args[n_in:n_in + n_w]))
    var = dict(zip(WEIGHTS, args[n_in + n_w:n_in + 2 * n_w]))
    me = _my_index()

    x = given['x'][0]
    ctx = given['ctx'][0]
    target = given['loss_target'][0]
    n, D = x.shape
    depth = given['w_mod'].shape[0]
    mod_cols = given['w_mod'].shape[2]
    conv_cols = given['ffn_conv_w'].shape[2]

    late0 = tuple(LAYER0_CARRIERS.values())
    gathered0 = {name: _all_gather(given[name][0].astype(BF16), 'ag_' + name) for name in BIG if name not in late0}
    misc = _all_gather(_pack_rows([given['c'], given['ffn_conv_w']]), 'ag_cond')
    misc = misc.reshape(N_DEV, -1)
    c_all = misc[:, :D]
    conv_w = misc[:, D:D + depth * 3 * conv_cols].reshape(N_DEV, depth, 3, conv_cols)
    conv_w = conv_w.transpose(1, 2, 0, 3).reshape(depth, 3, N_DEV * conv_cols)

    cond = jnp.concatenate([c_all, given['c_ctx'][None], jnp.zeros((16 - N_DEV - 1, D), F32)], axis=0)
    s16 = jax.nn.silu(cond).astype(BF16)
    wm16 = given['w_mod'].astype(BF16)
    b_loc = lax.dynamic_slice_in_dim(given['b_mod'], me * mod_cols, mod_cols, axis=1)
    mod_part = jnp.stack([_mm(s16, wm16[l], 'nn', F32, 'mod_fwd') + b_loc[l] for l in range(depth)])
    mod_all = _all_gather(mod_part, 'ag_mod').transpose(1, 2, 0, 3).reshape(depth, 16, N_DEV * mod_cols)
    mod_x = lax.dynamic_index_in_dim(mod_all, me, axis=1, keepdims=False)
    mod_c = mod_all[:, N_DEV]

    ws = {name: given[name] for name in SMALL if name not in ('c_ctx', 'b_mod')}
    ws['ffn_conv_w'] = conv_w
    shards = {name: tuple(given[name][l] if l or name in late0 else None for l in range(depth)) for name in BIG}
    y, vjp = jax.vjp(lambda *d: _forward(*d, ctx), x, mod_x, mod_c, gathered0, shards, ws)
    loss_local, dy = _loss_head(y, target)
    g_x, g_mod_x, g_mod_c, g_gathered0, g_shards, g_ws = vjp(dy)

    g_big = {}
    for name in BIG:
        g0 = g_shards[name][0]
        if name not in late0:
            g0 = _sum_parts(_all_to_all(g_gathered0[name], 'a2a_' + name), 'sum_' + name)
        g_big[name] = jnp.stack((g0,) + tuple(g_shards[name][1:]))

    g_mod = jnp.zeros((depth, 16, N_DEV * mod_cols), F32)
    g_mod = lax.dynamic_update_slice_in_dim(g_mod, g_mod_x[:, None, :], me, axis=1)
    g_mod = g_mod.at[:, N_DEV].set(g_mod_c)
    g_mod_parts = g_mod.reshape(depth, 16, N_DEV, mod_cols).transpose(2, 0, 1, 3).reshape(N_DEV, -1)
    g_conv_parts = g_ws['ffn_conv_w'].reshape(depth, 3, N_DEV, conv_cols).transpose(2, 0, 1, 3).reshape(N_DEV, -1)
    n_mod = depth * 16 * mod_cols
    n_conv = depth * 3 * conv_cols
    f32_parts = jnp.concatenate([g_mod_parts, g_conv_parts], axis=1)
    f32_pad = (-f32_parts.shape[1]) % 1024
    f32_parts = jnp.pad(f32_parts, ((0, 0), (0, f32_pad))).reshape(N_DEV, -1, 128)
    f32_parts = _all_to_all(f32_parts, 'a2a_f32')
    f32_sum = _sum_parts(f32_parts, 'sum_f32').reshape(-1)
    g_mod_loc = f32_sum[:n_mod].reshape(depth, 16, mod_cols)
    g_conv_loc = f32_sum[n_mod:n_mod + n_conv].reshape(depth * 3, conv_cols)

    g_mod16 = g_mod_loc.astype(BF16)
    g_w_mod = jnp.stack([_mm(s16, g_mod16[l], 'tn', F32, 'mod_dw') for l in range(depth)])
    d_silu = sum(_mm(g_mod16[l], wm16[l], 'nt', F32, 'mod_dc') for l in range(depth))
    g_c_ctx_part = d_silu[N_DEV] * _silu_grad(given['c_ctx'])

    small_grads = {name: g_ws[name] for name in SMALL if name not in ('c_ctx', 'b_mod')}
    small_grads['c_ctx'] = g_c_ctx_part
    small_grads['b_mod'] = g_mod_x + g_mod_c
    small_shapes = [given[name].shape for name in SMALL]
    n_small = sum(int(given[name].size) for name in SMALL)
    packed = _pack_rows([small_grads[name] for name in SMALL] + [loss_local.reshape(1)])
    small_parts = _all_gather(packed, 'ag_small')

    out_g, out_d, out_m, out_v = {}, {}, {}, {}

    def update(name, parts):
        shape = given[name].shape
        flat = (shape[0] * shape[1], shape[2])
        res = _adamw(parts.reshape((-1,) + flat), given[name].reshape(flat), mom[name].reshape(flat),
                     var[name].reshape(flat), 'adamw_' + name)
        out_g[name], out_d[name], out_m[name], out_v[name] = (r.reshape(shape) for r in res)

    for name in BIG:
        update(name, g_big[name])
    update('w_mod', g_w_mod)
    update('ffn_conv_w', g_conv_loc)

    zero1 = jnp.zeros((1,), F32)
    res = _adamw(small_parts, _pack_rows([given[k] for k in SMALL] + [zero1]), _pack_rows([mom[k] for k in SMALL] + [zero1]),
                 _pack_rows([var[k] for k in SMALL] + [zero1 + 1.0]), 'adamw_small')
    flats = [r.reshape(-1) for r in res]
    for name, g_, d_, m_, v_ in zip(SMALL, *[_unpack(f, small_shapes) for f in flats]):
        out_g[name], out_d[name], out_m[name], out_v[name] = g_, d_, m_, v_
    loss = flats[0][n_small]

    return (loss, g_x[None], *[out_g[k] for k in WEIGHTS], *[out_d[k] for k in WEIGHTS],
            *[out_m[k] for k in WEIGHTS], *[out_v[k] for k in WEIGHTS])
```

```python
import functools

import jax
import jax.numpy as jnp
from jax import lax
from jax.experimental import pallas as pl
from jax.experimental.pallas import tpu as pltpu

F32 = jnp.float32
BF16 = jnp.bfloat16
N_DEV = 8
MESH_ID = pl.DeviceIdType.MESH

GRID_W = 64
ROPE_BASE = 10000.0
NORM_EPS = 1e-6
NEG_INF = -1e30

MLA_HEADS = 4
MLA_NOPE = 128
MLA_ROPE = 64
MLA_V = 128
MLA_Q_LORA = 512
MLA_KV_LORA = 256
MLA_SCALE = (MLA_NOPE + MLA_ROPE) ** -0.5
POOL_WINDOWS = (2, 4, 8, 16)
POOL_GROUP = 128
POOL_WIDTH = POOL_GROUP * len(POOL_WINDOWS)
SWA_HEADS = 8
SWA_KV_HEADS = 2
SWA_HEAD_DIM = 64
SWA_WINDOW = 128
SWA_BLOCK = 128
NA_HEADS = 8
NA_HEAD_DIM = 64
NA_KH = 8
NA_KW = 16
NA_QC = 16
NA_KC = NA_QC + NA_KW

A_COLS = MLA_Q_LORA + MLA_KV_LORA + MLA_ROPE
B_COLS = POOL_WIDTH
C_COLS = (SWA_HEADS + 2 * SWA_KV_HEADS) * SWA_HEAD_DIM
D_COLS = 3 * NA_HEADS * NA_HEAD_DIM
IN_COLS = A_COLS + B_COLS + C_COLS + D_COLS
IN_COLS_PAD = 3840
IN_SPLITS = (A_COLS, A_COLS + B_COLS, A_COLS + B_COLS + C_COLS)

ADAM_LR = 0.001
ADAM_B1 = 0.9
ADAM_B2 = 0.999
ADAM_EPS = 1e-08
ADAM_WD = 0.01
ADAM_STEP = 10

VMEM_LIMIT = 48 << 20

ARG_NAMES = ['x', 'c', 'ctx', 'c_ctx', 'w_mod', 'b_mod', 'g_mix', 'g_ffn', 'w_in', 'w_out', 'mla_q_a_norm', 'mla_w_qb', 'mla_kv_a_norm', 'mla_w_kvb', 'mla_q_nope_norm', 'mla_q_rope_norm', 'mla_k_nope_norm', 'mla_k_rope_norm', 'pool_w', 'pool_scale', 'swa_q_norm', 'swa_k_norm', 'swa_sink', 'na_q_norm', 'na_k_norm', 'na_rpb', 'ffn_w_up', 'ffn_conv_w', 'ffn_conv_b', 'ffn_w_down', 'loss_target']
WEIGHTS = ['c_ctx', 'w_mod', 'b_mod', 'g_mix', 'g_ffn', 'w_in', 'w_out', 'mla_q_a_norm', 'mla_w_qb', 'mla_kv_a_norm', 'mla_w_kvb', 'mla_q_nope_norm', 'mla_q_rope_norm', 'mla_k_nope_norm', 'mla_k_rope_norm', 'pool_w', 'pool_scale', 'swa_q_norm', 'swa_k_norm', 'swa_sink', 'na_q_norm', 'na_k_norm', 'na_rpb', 'ffn_w_up', 'ffn_conv_w', 'ffn_conv_b', 'ffn_w_down']
BIG = {'w_in': 2, 'w_out': 1, 'mla_w_qb': 2, 'mla_w_kvb': 2, 'ffn_w_up': 2, 'ffn_w_down': 1}
SMALL = ['c_ctx', 'b_mod', 'g_mix', 'g_ffn', 'mla_q_a_norm', 'mla_kv_a_norm', 'mla_q_nope_norm', 'mla_q_rope_norm', 'mla_k_nope_norm', 'mla_k_rope_norm', 'pool_w', 'pool_scale', 'swa_q_norm', 'swa_k_norm', 'swa_sink', 'na_q_norm', 'na_k_norm', 'na_rpb', 'ffn_conv_b']


def _pcall(body, **kw):
    return pl.pallas_call(body, **kw)


def _my_index():
    return 4 * lax.axis_index("x") + 2 * lax.axis_index("y") + lax.axis_index("c")


_COMM_SCRATCH = [pltpu.SemaphoreType.DMA((7,)), pltpu.SemaphoreType.DMA((7,)), pltpu.SemaphoreType.DMA(())]
_ANY = pl.BlockSpec(memory_space=pl.ANY)


def _gather_copies(x_ref, out_ref, send_sems, recv_sems, local_sem):
    x, y, c = lax.axis_index("x"), lax.axis_index("y"), lax.axis_index("c")
    me, sibling = (x, y, c), (x, y, 1 - c)
    chips = [(1 - x, y), (x, 1 - y), (1 - x, 1 - y)]

    def slot(px, py, pc):
        return out_ref.at[4 * px + 2 * py + pc]

    def copy(k, blk, to, src=None):
        return pltpu.make_async_remote_copy(
            src_ref=slot(*blk) if src is None else src, dst_ref=slot(*blk),
            send_sem=send_sems.at[k], recv_sem=recv_sems.at[k], device_id=to, device_id_type=MESH_ID)

    mine = pltpu.make_async_copy(x_ref, slot(*me), local_sem)
    first = [copy(0, me, sibling, src=x_ref)] + [copy(1 + j, me, (*chip, c), src=x_ref) for j, chip in enumerate(chips)]

    def start():
        mine.start()
        for cp in first:
            cp.start()

    def finish():
        passed = [copy(4 + j, (*chip, c), sibling) for j, chip in enumerate(chips)]
        for j, chip in enumerate(chips):
            copy(1 + j, (*chip, c), me).wait_recv()
            passed[j].start()
        copy(0, sibling, me).wait_recv()
        for j, chip in enumerate(chips):
            copy(4 + j, (*chip, 1 - c), me).wait_recv()
        for cp in first + passed:
            cp.wait_send()
        mine.wait()

    return start, finish


def _exchange_copies(t_ref, out_ref, send_sems, recv_sems, local_sem):
    x, y, c = lax.axis_index("x"), lax.axis_index("y"), lax.axis_index("c")
    me = 4 * x + 2 * y + c

    def peer(k):
        return (1 - x if k & 4 else x), (1 - y if k & 2 else y), (1 - c if k & 1 else c)

    def copy(k, landed):
        px, py, pc = peer(k)
        p = 4 * px + 2 * py + pc
        return pltpu.make_async_remote_copy(
            src_ref=t_ref.at[p], dst_ref=out_ref.at[p if landed else me],
            send_sem=send_sems.at[k - 1], recv_sem=recv_sems.at[k - 1], device_id=(px, py, pc), device_id_type=MESH_ID)

    mine = pltpu.make_async_copy(t_ref.at[me], out_ref.at[me], local_sem)
    sends = [copy(k, False) for k in range(1, N_DEV)]

    def start():
        mine.start()
        for cp in sends:
            cp.start()

    def finish():
        for k in range(1, N_DEV):
            copy(k, True).wait_recv()
        for cp in sends:
            cp.wait_send()
        mine.wait()

    return start, finish


_COMM = {'gather': _gather_copies, 'exchange': _exchange_copies}


def _comm_out_shape(kind, operand):
    shape = (N_DEV,) + operand.shape if kind == 'gather' else operand.shape
    return jax.ShapeDtypeStruct(shape, operand.dtype)


def _comm_call(kind, operand, name):
    def body(x_ref, out_ref, send_sems, recv_sems, local_sem):
        start, finish = _COMM[kind](x_ref, out_ref, send_sems, recv_sems, local_sem)
        start()
        finish()

    return _pcall(body, name=name, out_shape=_comm_out_shape(kind, operand), in_specs=[_ANY], out_specs=_ANY,
                  scratch_shapes=_COMM_SCRATCH)(operand)


def _all_gather(block, name):
    return _comm_call('gather', block, name)


def _all_to_all(parts, name):
    return _comm_call('exchange', parts, name)


def _carry_call(body, comm, name, grid, in_specs, out_specs, out_shape, scratch_shapes, semantics, operands):
    if comm is None:
        outs = _pcall(
            body, name=name, grid=grid, in_specs=in_specs, out_specs=out_specs, out_shape=out_shape,
            scratch_shapes=scratch_shapes,
            compiler_params=pltpu.CompilerParams(dimension_semantics=semantics, vmem_limit_bytes=VMEM_LIMIT),
        )(*operands)
        return outs, None

    kind, operand = comm
    n_in, n_out = len(in_specs), len(out_specs)
    steps = 1
    for g in grid:
        steps *= g

    def carrying(*refs):
        ins, x_ref = refs[:n_in], refs[n_in]
        outs, out_ref = refs[n_in + 1:n_in + 1 + n_out], refs[n_in + 1 + n_out]
        scratch, sems = refs[n_in + 2 + n_out:len(refs) - 3], refs[len(refs) - 3:]
        start, finish = _COMM[kind](x_ref, out_ref, *sems)
        step = pl.program_id(0)
        for axis in range(1, len(grid)):
            step = step * grid[axis] + pl.program_id(axis)

        @pl.when(step == 0)
        def _():
            start()

        body(*ins, *outs, *scratch)

        @pl.when(step == steps - 1)
        def _():
            finish()

    outs = _pcall(
        carrying, name=name, grid=grid, in_specs=list(in_specs) + [_ANY], out_specs=list(out_specs) + [_ANY],
        out_shape=list(out_shape) + [_comm_out_shape(kind, operand)], scratch_shapes=list(scratch_shapes) + _COMM_SCRATCH,
        compiler_params=pltpu.CompilerParams(dimension_semantics=("arbitrary",) * len(grid), vmem_limit_bytes=VMEM_LIMIT),
    )(*operands, operand)
    return outs[:-1], outs[-1]


_LANE_TILES = (1024, 768, 512, 384, 256, 128)
_ROW_TILES = (1088, 1024, 512, 256, 128)
_DEPTH_TILES = (2048, 1408) + _LANE_TILES
_TOKEN_DEPTH_TILES = (2176,) + _ROW_TILES


def _pick(dim, cands):
    for t in cands:
        if dim % t == 0:
            return t
    return dim


def _mm(a, b, mode, out_dtype, name, comm=None):
    if mode == 'nn':
        (M, K), (_, N) = a.shape, b.shape
        tm, tn, tk = _pick(M, _ROW_TILES), _pick(N, _LANE_TILES), _pick(K, _DEPTH_TILES)
        a_spec = pl.BlockSpec((tm, tk), lambda i, j, k: (i, k))
        b_spec = pl.BlockSpec((tk, tn), lambda i, j, k: (k, j))
        dn = (((1,), (0,)), ((), ()))
    elif mode == 'nt':
        (M, K), (N, _) = a.shape, b.shape
        tm, tn, tk = _pick(M, _ROW_TILES), _pick(N, _LANE_TILES), _pick(K, _DEPTH_TILES)
        a_spec = pl.BlockSpec((tm, tk), lambda i, j, k: (i, k))
        b_spec = pl.BlockSpec((tn, tk), lambda i, j, k: (j, k))
        dn = (((1,), (1,)), ((), ()))
    else:
        (K, M), (_, N) = a.shape, b.shape
        tm, tn, tk = _pick(M, _LANE_TILES), _pick(N, _LANE_TILES), _pick(K, _TOKEN_DEPTH_TILES)
        a_spec = pl.BlockSpec((tk, tm), lambda i, j, k: (k, i))
        b_spec = pl.BlockSpec((tk, tn), lambda i, j, k: (k, j))
        dn = (((0,), (0,)), ((), ()))
    grid = (M // tm, N // tn, K // tk)

    def matmul_step(a_ref, b_ref, o_ref, acc):
        @pl.when(pl.program_id(2) == 0)
        def _():
            acc[...] = jnp.zeros_like(acc)

        acc[...] += lax.dot_general(a_ref[...], b_ref[...], dn, preferred_element_type=F32)

        @pl.when(pl.program_id(2) == grid[2] - 1)
        def _():
            o_ref[...] = acc[...].astype(o_ref.dtype)

    outs, moved = _carry_call(
        matmul_step, comm, name, grid, [a_spec, b_spec], [pl.BlockSpec((tm, tn), lambda i, j, k: (i, j))],
        [jax.ShapeDtypeStruct((M, N), out_dtype)], [pltpu.VMEM((tm, tn), F32)], ("parallel", "parallel", "arbitrary"),
        (a, b))
    return outs[0] if comm is None else (outs[0], moved)


def _proj_fwd(a16, w, next_shard, name):
    if next_shard is None:
        return _mm(a16, w, 'nn', F32, name + '_nn'), None
    return _mm(a16, w, 'nn', F32, name + '_nn_gather', comm=('gather', next_shard.astype(BF16)))


def _proj_bwd(a16, w, dy, d_gathered, name):
    dy16 = dy.astype(BF16)
    if d_gathered is None:
        return _mm(dy16, w, 'nt', F32, name + '_nt'), _mm(a16, dy16, 'tn', BF16, name + '_tn'), None
    half = d_gathered.shape[1] // 2
    dw, top = _mm(a16, dy16, 'tn', BF16, name + '_tn_exchange', comm=('exchange', d_gathered[:, :half]))
    da, low = _mm(dy16, w, 'nt', F32, name + '_nt_exchange', comm=('exchange', d_gathered[:, half:]))
    d_shard = jnp.concatenate([_sum_parts(top, name + '_sum_top'), _sum_parts(low, name + '_sum_low')], axis=0)
    return da, dw, d_shard


@functools.partial(jax.custom_vjp, nondiff_argnums=(3,))
def pmm(a, w, next_shard, name):
    return _proj_fwd(a.astype(BF16), w, next_shard, name)


def _pmm_fwd(a, w, next_shard, name):
    a16 = a.astype(BF16)
    return _proj_fwd(a16, w, next_shard, name), (a16, w)


def _pmm_bwd(name, res, cts):
    return _proj_bwd(*res, cts[0], cts[1], name)


pmm.defvjp(_pmm_fwd, _pmm_bwd)


def _normmod_rows(x, ctx):
    n, d = x.shape
    rows = (n,) if ctx is None else (n, ctx.shape[0])
    tr = next(t for t in (256, 128, 64, 32, 16, 8) if all(r % t == 0 for r in rows))
    nx = n // tr
    specs = [pl.BlockSpec((tr, d), lambda i: (jnp.minimum(i, nx - 1), 0))]
    if ctx is not None:
        specs.append(pl.BlockSpec((tr, d), lambda i: (jnp.maximum(i - nx, 0), 0)))
    return tr, nx, sum(rows) // tr, specs


def _normmod_terms(refs, has_ctx, nx):
    is_x = pl.program_id(0) < nx
    vec_ref = refs[2] if has_ctx else refs[1]
    t = jnp.where(is_x, refs[0][...], refs[1][...]) if has_ctx else refs[0][...]
    shift = jnp.where(is_x, vec_ref[1:2, :], vec_ref[3:4, :])
    scale = jnp.where(is_x, vec_ref[2:3, :], vec_ref[4:5, :])
    r = lax.rsqrt(jnp.mean(t * t, axis=1, keepdims=True) + NORM_EPS)
    return is_x, t, r, vec_ref[0:1, :], shift, scale


def _normmod_fwd_call(x, ctx, vec):
    has_ctx = ctx is not None
    d = x.shape[1]
    tr, nx, blocks, specs = _normmod_rows(x, ctx)

    def body(*refs):
        _, t, r, g, shift, scale = _normmod_terms(refs, has_ctx, nx)
        refs[-1][...] = ((t * r * g) * (1.0 + scale) + shift).astype(BF16)

    operands = (x, ctx, vec) if has_ctx else (x, vec)
    return _pcall(
        body, name='normmod_fwd', grid=(blocks,), in_specs=specs + [pl.BlockSpec((8, d), lambda i: (0, 0))],
        out_specs=pl.BlockSpec((tr, d), lambda i: (i, 0)), out_shape=jax.ShapeDtypeStruct((blocks * tr, d), BF16),
        compiler_params=pltpu.CompilerParams(dimension_semantics=("parallel",), vmem_limit_bytes=VMEM_LIMIT),
    )(*operands)


def _normmod_bwd_call(x, ctx, vec, dh):
    has_ctx = ctx is not None
    d = x.shape[1]
    tr, nx, blocks, specs = _normmod_rows(x, ctx)
    n_in = len(specs) + 2

    def body(*refs):
        is_x, t, r, g, shift, scale = _normmod_terms(refs, has_ctx, nx)
        dh_ref, dx_ref, dvec_ref = refs[n_in - 1], refs[n_in], refs[-1]

        @pl.when(pl.program_id(0) == 0)
        def _():
            dvec_ref[...] = jnp.zeros_like(dvec_ref)

        xn = t * r
        dh = dh_ref[...]
        dy = dh * (1.0 + scale)
        dxn = dy * g
        dt = r * (dxn - xn * jnp.mean(dxn * xn, axis=1, keepdims=True))
        d_shift = jnp.sum(dh, axis=0, keepdims=True)
        d_scale = jnp.sum(dh * (xn * g), axis=0, keepdims=True)
        dvec_ref[0:1, :] += jnp.sum(dy * xn, axis=0, keepdims=True)

        @pl.when(is_x)
        def _():
            dx_ref[...] = dt
            dvec_ref[1:2, :] += d_shift
            dvec_ref[2:3, :] += d_scale

        if has_ctx:
            @pl.when(jnp.logical_not(is_x))
            def _():
                refs[n_in + 1][...] = dt
                dvec_ref[3:4, :] += d_shift
                dvec_ref[4:5, :] += d_scale

    vec_spec = pl.BlockSpec((8, d), lambda i: (0, 0))
    operands = (x, ctx, vec, dh) if has_ctx else (x, vec, dh)
    shapes = [jax.ShapeDtypeStruct(x.shape, F32)] + ([jax.ShapeDtypeStruct(ctx.shape, F32)] if has_ctx else [])
    outs = _pcall(
        body, name='normmod_bwd', grid=(blocks,),
        in_specs=specs + [vec_spec, pl.BlockSpec((tr, d), lambda i: (i, 0))],
        out_specs=specs + [vec_spec], out_shape=shapes + [jax.ShapeDtypeStruct((8, d), F32)],
        compiler_params=pltpu.CompilerParams(dimension_semantics=("arbitrary",), vmem_limit_bytes=VMEM_LIMIT),
    )(*operands)
    return (outs[0], outs[1], outs[2]) if has_ctx else (outs[0], None, outs[1])


@functools.partial(jax.custom_vjp, nondiff_argnums=(5,))
def norm_proj(x, ctx, vec, w, next_shard, name):
    return _norm_proj_fwd(x, ctx, vec, w, next_shard, name)[0]


def _norm_proj_fwd(x, ctx, vec, w, next_shard, name):
    h16 = _normmod_fwd_call(x, ctx, vec)
    return _proj_fwd(h16, w, next_shard, name), (x, ctx, vec, h16, w)


def _norm_proj_bwd(name, res, cts):
    x, ctx, vec, h16, w = res
    dh, dw, d_shard = _proj_bwd(h16, w, cts[0], cts[1], name)
    dx, dctx, dvec = _normmod_bwd_call(x, ctx, vec, dh)
    return dx, dctx, dvec, dw, d_shard


norm_proj.defvjp(_norm_proj_fwd, _norm_proj_bwd)


FFN_TILE = 128


def _neighbours(x, n):
    T = x.shape[0]
    t = lax.broadcasted_iota(jnp.int32, x.shape, 0)
    prev = jnp.where((t == 0) | (t == n), 0.0, pltpu.roll(x, 1, 0))
    nxt = jnp.where((t == n - 1) | (t == T - 1), 0.0, pltpu.roll(x, T - 1, 0))
    return prev, nxt


def _gate_specs(rows, f):
    tiles = f // FFN_TILE
    return [pl.BlockSpec((rows, FFN_TILE), lambda j: (0, j)), pl.BlockSpec((rows, FFN_TILE), lambda j: (0, j + tiles))]


def _conv3(x, cw_ref, cb_ref, n):
    prev, nxt = _neighbours(x, n)
    return prev * cw_ref[0:1, :] + x * cw_ref[1:2, :] + nxt * cw_ref[2:3, :] + cb_ref[...], prev, nxt


def _gate_fwd_call(a, cw, cb, n):
    T, f = a.shape[0], a.shape[1] // 2

    def body(ag_ref, av_ref, wg_ref, wv_ref, bg_ref, bv_ref, u_ref):
        g = _conv3(ag_ref[...], wg_ref, bg_ref, n)[0]
        v = _conv3(av_ref[...], wv_ref, bv_ref, n)[0]
        u_ref[...] = (g * jax.nn.sigmoid(g) * v).astype(u_ref.dtype)

    return _pcall(
        body, name='ffn_gate_fwd', grid=(f // FFN_TILE,),
        in_specs=_gate_specs(T, f) + _gate_specs(3, f) + _gate_specs(1, f),
        out_specs=pl.BlockSpec((T, FFN_TILE), lambda j: (0, j)),
        out_shape=jax.ShapeDtypeStruct((T, f), BF16),
        compiler_params=pltpu.CompilerParams(dimension_semantics=("parallel",), vmem_limit_bytes=VMEM_LIMIT),
    )(a, a, cw, cw, cb, cb)


def _gate_bwd_call(a, cw, cb, du, n):
    T, f = a.shape[0], a.shape[1] // 2

    def half(dz, x, prev, nxt, w_ref, da_ref, dw_ref):
        t = lax.broadcasted_iota(jnp.int32, dz.shape, 0)
        from_next = pltpu.roll(jnp.where((t == 0) | (t == n), 0.0, dz), T - 1, 0)
        from_prev = pltpu.roll(jnp.where((t == n - 1) | (t == T - 1), 0.0, dz), 1, 0)
        da_ref[...] = (dz * w_ref[1:2, :] + from_next * w_ref[0:1, :] + from_prev * w_ref[2:3, :]).astype(da_ref.dtype)
        dw_ref[0:1, :] = jnp.sum(dz * prev, axis=0, keepdims=True)
        dw_ref[1:2, :] = jnp.sum(dz * x, axis=0, keepdims=True)
        dw_ref[2:3, :] = jnp.sum(dz * nxt, axis=0, keepdims=True)
        dw_ref[3:4, :] = jnp.sum(dz, axis=0, keepdims=True)
        dw_ref[4:8, :] = jnp.zeros((4, FFN_TILE), F32)

    def body(ag_ref, av_ref, wg_ref, wv_ref, bg_ref, bv_ref, du_ref, dag_ref, dav_ref, dwg_ref, dwv_ref):
        xg, xv = ag_ref[...], av_ref[...]
        g, g_prev, g_next = _conv3(xg, wg_ref, bg_ref, n)
        v, v_prev, v_next = _conv3(xv, wv_ref, bv_ref, n)
        sg = jax.nn.sigmoid(g)
        du = du_ref[...]
        half(du * v * (sg * (1.0 + g * (1.0 - sg))), xg, g_prev, g_next, wg_ref, dag_ref, dwg_ref)
        half(du * (g * sg), xv, v_prev, v_next, wv_ref, dav_ref, dwv_ref)

    tile = lambda rows: pl.BlockSpec((rows, FFN_TILE), lambda j: (0, j))
    return _pcall(
        body, name='ffn_gate_bwd', grid=(f // FFN_TILE,),
        in_specs=_gate_specs(T, f) + _gate_specs(3, f) + _gate_specs(1, f) + [tile(T)],
        out_specs=[tile(T), tile(T), tile(8), tile(8)],
        out_shape=[jax.ShapeDtypeStruct((T, f), BF16), jax.ShapeDtypeStruct((T, f), BF16),
                   jax.ShapeDtypeStruct((8, f), F32), jax.ShapeDtypeStruct((8, f), F32)],
        compiler_params=pltpu.CompilerParams(dimension_semantics=("parallel",), vmem_limit_bytes=VMEM_LIMIT),
    )(a, a, cw, cw, cb, cb, du)


@functools.partial(jax.custom_vjp, nondiff_argnums=(9,))
def conv_ffn(x, ctx, vec, w_up, cw, cb, w_down, next_up, next_down, n):
    return _conv_ffn_fwd(x, ctx, vec, w_up, cw, cb, w_down, next_up, next_down, n)[0]


def _conv_ffn_fwd(x, ctx, vec, w_up, cw, cb, w_down, next_up, next_down, n):
    h16 = _normmod_fwd_call(x, ctx, vec)
    a, g_up = _proj_fwd(h16, w_up, next_up, 'ffn_up')
    u16 = _gate_fwd_call(a, cw, cb.reshape(1, -1), n)
    y, g_down = _proj_fwd(u16, w_down, next_down, 'ffn_down')
    return (y, g_up, g_down), (x, ctx, vec, h16, w_up, cw, cb, w_down, a, u16)


def _conv_ffn_bwd(n, res, cts):
    x, ctx, vec, h16, w_up, cw, cb, w_down, a, u16 = res
    dy, d_g_up, d_g_down = cts
    du, dw_down, d_next_down = _proj_bwd(u16, w_down, dy, d_g_down, 'ffn_down')
    dag, dav, dwg, dwv = _gate_bwd_call(a, cw, cb.reshape(1, -1), du, n)
    dcw = jnp.concatenate([dwg, dwv], axis=1)
    dh, dw_up, d_next_up = _proj_bwd(h16, w_up, jnp.concatenate([dag, dav], axis=1), d_g_up, 'ffn_up')
    dx, dctx, dvec = _normmod_bwd_call(x, ctx, vec, dh)
    return dx, dctx, dvec, dw_up, dcw[:3], dcw[3], dw_down, d_next_up, d_next_down


conv_ffn.defvjp(_conv_ffn_fwd, _conv_ffn_bwd)


def _attn_fwd_call(q, k, v, scale, comm=None):
    H, nq, dq = q.shape
    nk, dv = v.shape[1], v.shape[2]
    tq = _pick(nq, (256, 128))

    def body(q_ref, k_ref, v_ref, o_ref, lse_ref):
        s = lax.dot_general(q_ref[0], k_ref[0], (((1,), (1,)), ((), ())), preferred_element_type=F32) * scale
        m = jnp.max(s, axis=1, keepdims=True)
        p = jnp.exp(s - m)
        l = jnp.sum(p, axis=1, keepdims=True)
        pn = (p * (1.0 / l)).astype(BF16)
        o_ref[...] = jnp.dot(pn, v_ref[0], preferred_element_type=F32)
        lse_ref[0] = m + jnp.log(l)

    return _carry_call(
        body, comm, 'mla_attn_fwd', (H, nq // tq),
        [pl.BlockSpec((1, tq, dq), lambda h, i: (h, i, 0)),
         pl.BlockSpec((1, nk, dq), lambda h, i: (h, 0, 0)),
         pl.BlockSpec((1, nk, dv), lambda h, i: (h, 0, 0))],
        [pl.BlockSpec((tq, dv), lambda h, i: (i, h)), pl.BlockSpec((1, tq, 1), lambda h, i: (h, i, 0))],
        [jax.ShapeDtypeStruct((nq, H * dv), F32), jax.ShapeDtypeStruct((H, nq, 1), F32)], [],
        ("parallel", "parallel"), (q, k, v))


def _attn_bwd_call(q, k, v, o, lse, do, scale, comm=None):
    H, nq, dq = q.shape
    nk, dv = v.shape[1], v.shape[2]
    tq = _pick(nq, (128,))

    def body(q_ref, k_ref, v_ref, o_ref, lse_ref, do_ref, dq_ref, dk_ref, dv_ref):
        @pl.when(pl.program_id(1) == 0)
        def _():
            dk_ref[...] = jnp.zeros_like(dk_ref)
            dv_ref[...] = jnp.zeros_like(dv_ref)

        q16, k16, v16 = q_ref[0], k_ref[0], v_ref[0]
        do = do_ref[...]
        do16 = do.astype(BF16)
        s = lax.dot_general(q16, k16, (((1,), (1,)), ((), ())), preferred_element_type=F32) * scale
        p = jnp.exp(s - lse_ref[0])
        dv_ref[0] += lax.dot_general(p.astype(BF16), do16, (((0,), (0,)), ((), ())), preferred_element_type=F32)
        dp = lax.dot_general(do16, v16, (((1,), (1,)), ((), ())), preferred_element_type=F32)
        delta = jnp.sum(do * o_ref[...], axis=1, keepdims=True)
        ds16 = (p * (dp - delta) * scale).astype(BF16)
        dq_ref[0] = jnp.dot(ds16, k16, preferred_element_type=F32)
        dk_ref[0] += lax.dot_general(ds16, q16, (((0,), (0,)), ((), ())), preferred_element_type=F32)

    return _carry_call(
        body, comm, 'mla_attn_bwd', (H, nq // tq),
        [pl.BlockSpec((1, tq, dq), lambda h, i: (h, i, 0)),
         pl.BlockSpec((1, nk, dq), lambda h, i: (h, 0, 0)),
         pl.BlockSpec((1, nk, dv), lambda h, i: (h, 0, 0)),
         pl.BlockSpec((tq, dv), lambda h, i: (i, h)),
         pl.BlockSpec((1, tq, 1), lambda h, i: (h, i, 0)),
         pl.BlockSpec((tq, dv), lambda h, i: (i, h))],
        [pl.BlockSpec((1, tq, dq), lambda h, i: (h, i, 0)),
         pl.BlockSpec((1, nk, dq), lambda h, i: (h, 0, 0)),
         pl.BlockSpec((1, nk, dv), lambda h, i: (h, 0, 0))],
        [jax.ShapeDtypeStruct((H, nq, dq), F32), jax.ShapeDtypeStruct((H, nk, dq), F32),
         jax.ShapeDtypeStruct((H, nk, dv), F32)], [], ("parallel", "arbitrary"), (q, k, v, o, lse, do))


def _gather_comm(shard):
    return None if shard is None else ('gather', shard.astype(BF16))


def _exchange_comm(d_gathered):
    return None if d_gathered is None else ('exchange', d_gathered)


def _reduce_moved(parts, name):
    return None if parts is None else _sum_parts(parts, name)


@functools.partial(jax.custom_vjp, nondiff_argnums=(4,))
def attention(q, k, v, shard, scale):
    (o, _), gathered = _attn_fwd_call(q.astype(BF16), k.astype(BF16), v.astype(BF16), scale, _gather_comm(shard))
    return o, gathered


def _attention_fwd(q, k, v, shard, scale):
    q16, k16, v16 = q.astype(BF16), k.astype(BF16), v.astype(BF16)
    (o, lse), gathered = _attn_fwd_call(q16, k16, v16, scale, _gather_comm(shard))
    return (o, gathered), (q16, k16, v16, o, lse)


def _attention_bwd(scale, res, cts):
    q16, k16, v16, o, lse = res
    grads, parts = _attn_bwd_call(q16, k16, v16, o, lse, cts[0], scale, _exchange_comm(cts[1]))
    return tuple(grads) + (_reduce_moved(parts, 'mla_attn_sum'),)


attention.defvjp(_attention_fwd, _attention_bwd)


def _win_geometry(kind, n):
    if kind == 'na':
        rows = n // GRID_W
        kh = min(NA_KH, rows)

        def start(i):
            return jnp.clip(i - kh // 2, 0, rows - kh) * GRID_W

        def bidx(i):
            return jnp.clip(i - kh // 2, 0, rows - kh) - i + (NA_KH - 1)

        return GRID_W, kh * GRID_W, start, bidx
    nb = n // SWA_BLOCK

    def start(i):
        return i * SWA_BLOCK

    def bidx(i):
        return jnp.where(i == 0, 0, jnp.where(i == nb - 1, 2, 1))

    return SWA_BLOCK, 3 * SWA_BLOCK, start, bidx


def _dot_nt(a, b):
    return lax.dot_general(a, b, (((1,), (1,)), ((), ())), preferred_element_type=F32)


def _dot_tn(a, b):
    return lax.dot_general(a, b, (((0,), (0,)), ((), ())), preferred_element_type=F32)


def _win_specs(q, k, kc, bias):
    hq, n, d = q.shape
    grp = hq // k.shape[0]
    hb = bias.shape[0]
    return [
        pl.BlockSpec((1, n, d), lambda h: (h, 0, 0)),
        pl.BlockSpec((1,) + k.shape[1:], lambda h: (h // grp, 0, 0)),
        pl.BlockSpec((1,) + k.shape[1:], lambda h: (h // grp, 0, 0)),
        pl.BlockSpec((1,) + kc.shape[1:], lambda h: (h // grp, 0, 0)),
        pl.BlockSpec((1,) + kc.shape[1:], lambda h: (h // grp, 0, 0)),
        pl.BlockSpec((1,) + bias.shape[1:], (lambda h: (h, 0, 0, 0)) if hb > 1 else (lambda h: (0, 0, 0, 0))),
        pl.BlockSpec(memory_space=pltpu.SMEM),
    ]


def _win_fwd_call(q, k, v, kc, vc, bias, sink, kind, comm=None):
    hq, n, d = q.shape
    scale = d ** -0.5
    qb, wk, start, bidx = _win_geometry(kind, n)
    has_sink = kind == 'swa'

    def body(q_ref, k_ref, v_ref, kc_ref, vc_ref, b_ref, sink_ref, o_ref, lse_ref):
        kc16, vc16 = kc_ref[0], vc_ref[0]
        snk = sink_ref[pl.program_id(0)]

        def step(i, carry):
            qs = pl.multiple_of(i * qb, qb)
            ks = pl.multiple_of(start(i), GRID_W)
            q16 = q_ref[0, pl.ds(qs, qb), :]
            s1 = _dot_nt(q16, k_ref[0, pl.ds(ks, wk), :]) * scale + b_ref[0, bidx(i)]
            s2 = _dot_nt(q16, kc16) * scale
            m = jnp.maximum(jnp.max(s1, axis=1, keepdims=True), jnp.max(s2, axis=1, keepdims=True))
            if has_sink:
                m = jnp.maximum(m, snk)
            p1 = jnp.exp(s1 - m)
            p2 = jnp.exp(s2 - m)
            l = jnp.sum(p1, axis=1, keepdims=True) + jnp.sum(p2, axis=1, keepdims=True)
            if has_sink:
                l = l + jnp.exp(snk - m)
            inv = 1.0 / l
            o = (jnp.dot((p1 * inv).astype(BF16), v_ref[0, pl.ds(ks, wk), :], preferred_element_type=F32)
                 + jnp.dot((p2 * inv).astype(BF16), vc16, preferred_element_type=F32))
            o_ref[0, pl.ds(qs, qb), :] = o
            lse_ref[0, pl.ds(qs, qb), :] = m + jnp.log(l)
            return carry

        lax.fori_loop(0, n // qb, step, 0, unroll=4)

    return _carry_call(
        body, comm, kind + '_attn_fwd', (hq,), _win_specs(q, k, kc, bias),
        [pl.BlockSpec((1, n, d), lambda h: (h, 0, 0)), pl.BlockSpec((1, n, 1), lambda h: (h, 0, 0))],
        [jax.ShapeDtypeStruct((hq, n, d), F32), jax.ShapeDtypeStruct((hq, n, 1), F32)], [], ("parallel",),
        (q, k, v, kc, vc, bias, sink))


def _win_bwd_call(q, k, v, kc, vc, bias, sink, o, lse, do, kind, comm=None):
    hq, n, d = q.shape
    scale = d ** -0.5
    qb, wk, start, bidx = _win_geometry(kind, n)
    has_sink = kind == 'swa'
    bias_grad = kind == 'na'

    def body(q_ref, k_ref, v_ref, kc_ref, vc_ref, b_ref, sink_ref, o_ref, lse_ref, do_ref,
             dq_ref, dk_ref, dv_ref, dkc_ref, dvc_ref, db_ref, dsink_ref):
        kc16, vc16 = kc_ref[0], vc_ref[0]
        snk = sink_ref[pl.program_id(0)]
        dk_ref[...] = jnp.zeros_like(dk_ref)
        dv_ref[...] = jnp.zeros_like(dv_ref)
        dkc_ref[...] = jnp.zeros_like(dkc_ref)
        dvc_ref[...] = jnp.zeros_like(dvc_ref)
        db_ref[...] = jnp.zeros_like(db_ref)

        def step(i, dsink):
            qs = pl.multiple_of(i * qb, qb)
            ks = pl.multiple_of(start(i), GRID_W)
            q16 = q_ref[0, pl.ds(qs, qb), :]
            k16 = k_ref[0, pl.ds(ks, wk), :]
            v16 = v_ref[0, pl.ds(ks, wk), :]
            lse = lse_ref[0, pl.ds(qs, qb), :]
            do = do_ref[0, pl.ds(qs, qb), :]
            do16 = do.astype(BF16)
            p1 = jnp.exp(_dot_nt(q16, k16) * scale + b_ref[0, bidx(i)] - lse)
            p2 = jnp.exp(_dot_nt(q16, kc16) * scale - lse)
            delta = jnp.sum(do * o_ref[0, pl.ds(qs, qb), :], axis=1, keepdims=True)
            ds1 = p1 * (_dot_nt(do16, v16) - delta)
            ds2 = p2 * (_dot_nt(do16, vc16) - delta)
            if bias_grad:
                db_ref[0, bidx(i)] += ds1
            ds1 = (ds1 * scale).astype(BF16)
            ds2 = (ds2 * scale).astype(BF16)
            dq_ref[0, pl.ds(qs, qb), :] = (jnp.dot(ds1, k16, preferred_element_type=F32)
                                          + jnp.dot(ds2, kc16, preferred_element_type=F32))
            dk_ref[0, pl.ds(ks, wk), :] += _dot_tn(ds1, q16)
            dv_ref[0, pl.ds(ks, wk), :] += _dot_tn(p1.astype(BF16), do16)
            dkc_ref[0] += _dot_tn(ds2, q16)
            dvc_ref[0] += _dot_tn(p2.astype(BF16), do16)
            if has_sink:
                dsink = dsink - jnp.sum(jnp.exp(snk - lse) * delta)
            return dsink

        dsink = lax.fori_loop(0, n // qb, step, jnp.zeros((), F32), unroll=2)
        dsink_ref[...] = jnp.full(dsink_ref.shape, dsink, F32)

    per_head = lambda shape: pl.BlockSpec((1,) + shape[1:], lambda h: (h,) + (0,) * (len(shape) - 1))
    kq = (hq,) + k.shape[1:]
    cq = (hq,) + kc.shape[1:]
    bq = (hq,) + bias.shape[1:]
    in_specs = _win_specs(q, k, kc, bias) + [per_head(o.shape), per_head(lse.shape), per_head(do.shape)]
    out_shapes = [q.shape, kq, kq, cq, cq, bq, (hq, 8, 128)]
    return _carry_call(
        body, comm, kind + '_attn_bwd', (hq,), in_specs, [per_head(s) for s in out_shapes],
        [jax.ShapeDtypeStruct(s, F32) for s in out_shapes], [], ("parallel",),
        (q, k, v, kc, vc, bias, sink, o, lse, do))


@functools.partial(jax.custom_vjp, nondiff_argnums=(8,))
def win_attention(q, k, v, kc, vc, bias, sink, shard, kind):
    b16 = lambda t: t.astype(BF16)
    (o, _), gathered = _win_fwd_call(b16(q), b16(k), b16(v), b16(kc), b16(vc), bias, sink, kind, _gather_comm(shard))
    return o, gathered


def _win_attention_fwd(q, k, v, kc, vc, bias, sink, shard, kind):
    res = tuple(t.astype(BF16) for t in (q, k, v, kc, vc)) + (bias, sink)
    (o, lse), gathered = _win_fwd_call(*res, kind, _gather_comm(shard))
    return (o, gathered), res + (o, lse)


def _win_attention_bwd(kind, res, cts):
    q, k, v, kc, vc, bias, sink, o, lse = res
    (dq, dk, dv, dkc, dvc, db, dsink), parts = _win_bwd_call(
        q, k, v, kc, vc, bias, sink, o, lse, cts[0], kind, _exchange_comm(cts[1]))
    hkv = k.shape[0]
    fold = lambda t: t.reshape((hkv, -1) + t.shape[1:]).sum(axis=1)
    if bias.shape[0] == 1:
        db = jnp.zeros_like(bias)
    return (dq, fold(dk), fold(dv), fold(dkc), fold(dvc), db, dsink[:, 0, 0], _reduce_moved(parts, kind + '_attn_sum'))


win_attention.defvjp(_win_attention_fwd, _win_attention_bwd)


def _loss_head(y, target):
    n, d = y.shape
    tr = _pick(n, (512, 256, 128))
    nb = n // tr

    def body(y_ref, t_ref, dy_ref, part_ref):
        err = y_ref[...] - t_ref[...]
        dy_ref[...] = err * (1.0 / d)
        part_ref[...] = jnp.full(part_ref.shape, jnp.sum(err * err), F32)

    dy, part = _pcall(
        body, name='loss_head', grid=(nb,),
        in_specs=[pl.BlockSpec((tr, d), lambda i: (i, 0)), pl.BlockSpec((tr, d), lambda i: (i, 0))],
        out_specs=[pl.BlockSpec((tr, d), lambda i: (i, 0)), pl.BlockSpec((1, 8, 128), lambda i: (i, 0, 0))],
        out_shape=[jax.ShapeDtypeStruct((n, d), F32), jax.ShapeDtypeStruct((nb, 8, 128), F32)],
        compiler_params=pltpu.CompilerParams(dimension_semantics=("parallel",), vmem_limit_bytes=VMEM_LIMIT),
    )(y, target)
    return 0.5 * jnp.sum(part[:, 0, 0]) / d, dy


def _sum_parts(parts, name):
    P, R, C = parts.shape
    tr = _pick(R, (256, 128, 64, 32, 16, 8))

    def body(p_ref, o_ref):
        acc = p_ref[0].astype(F32)
        for i in range(1, P):
            acc = acc + p_ref[i].astype(F32)
        o_ref[...] = acc

    return _pcall(
        body, name=name, grid=(R // tr,),
        in_specs=[pl.BlockSpec((P, tr, C), lambda i: (0, i, 0))],
        out_specs=pl.BlockSpec((tr, C), lambda i: (i, 0)),
        out_shape=jax.ShapeDtypeStruct((R, C), F32),
        compiler_params=pltpu.CompilerParams(dimension_semantics=("parallel",), vmem_limit_bytes=VMEM_LIMIT),
    )(parts)


def _adamw(parts, w, m, v, name):
    P, R, C = parts.shape
    tr = _pick(R, (128, 64, 32, 16, 8))
    c1 = 1.0 / (1.0 - ADAM_B1 ** ADAM_STEP)
    c2 = 1.0 / (1.0 - ADAM_B2 ** ADAM_STEP)

    def body(p_ref, w_ref, m_ref, v_ref, g_out, d_out, m_out, v_out):
        g = p_ref[0].astype(F32)
        for i in range(1, P):
            g = g + p_ref[i].astype(F32)
        m_new = ADAM_B1 * m_ref[...] + (1.0 - ADAM_B1) * g
        v_new = ADAM_B2 * v_ref[...] + (1.0 - ADAM_B2) * (g * g)
        g_out[...] = g
        m_out[...] = m_new
        v_out[...] = v_new
        d_out[...] = -ADAM_LR * ((m_new * c1) / (jnp.sqrt(v_new * c2) + ADAM_EPS) + ADAM_WD * w_ref[...])

    blk = pl.BlockSpec((tr, C), lambda i: (i, 0))
    return _pcall(
        body, name=name, grid=(R // tr,),
        in_specs=[pl.BlockSpec((P, tr, C), lambda i: (0, i, 0)), blk, blk, blk],
        out_specs=[blk, blk, blk, blk],
        out_shape=[jax.ShapeDtypeStruct((R, C), F32)] * 4,
        compiler_params=pltpu.CompilerParams(dimension_semantics=("parallel",), vmem_limit_bytes=VMEM_LIMIT),
    )(parts, w, m, v)


def rms_norm(x, g):
    return x * lax.rsqrt(jnp.mean(x * x, axis=-1, keepdims=True) + NORM_EPS) * g


def modulate(h, shift, scale):
    return h * (1.0 + scale) + shift


def axial_angles(n, d_rot):
    t = jnp.arange(n)
    row = (t // GRID_W).astype(F32)
    col = (t % GRID_W).astype(F32)
    d_axis = d_rot // 2
    inv_freq = ROPE_BASE ** (-jnp.arange(0, d_axis, 2, dtype=F32) / d_axis)
    return (row[:, None] * inv_freq, col[:, None] * inv_freq)


def rope_segment(x, ang):
    cos = jnp.cos(ang)[:, None, :]
    sin = jnp.sin(ang)[:, None, :]
    x1, x2 = jnp.split(x, 2, axis=-1)
    return jnp.concatenate([x1 * cos - x2 * sin, x2 * cos + x1 * sin], axis=-1)


def axial_rope(x, ang):
    half = x.shape[-1] // 2
    return jnp.concatenate([rope_segment(x[..., :half], ang[0]), rope_segment(x[..., half:], ang[1])], axis=-1)


def rope_latent(t, n, ang):
    return jnp.concatenate([axial_rope(t[:n], ang), t[n:]], axis=0)


def mla_attend_ctx(q_nope, q_rope, k_nope, k_rope, v):
    s = (jnp.einsum('qhd,khd->hqk', q_nope, k_nope, preferred_element_type=F32)
         + jnp.einsum('qhr,kr->hqk', q_rope, k_rope, preferred_element_type=F32))
    p = jax.nn.softmax(s * MLA_SCALE, axis=-1)
    return jnp.einsum('hqk,khd->qhd', p, v)


def pool_mixer(u, w_pool, scale):
    n = u.shape[0]
    csum = jnp.pad(jnp.cumsum(u, axis=0), ((1, 0), (0, 0)))
    t = jnp.arange(n)
    diffs = []
    for g, w in enumerate(POOL_WINDOWS):
        sl = slice(g * POOL_GROUP, (g + 1) * POOL_GROUP)
        lo = jnp.clip(t - w // 2, 0, n)
        hi = jnp.clip(t + w // 2, 0, n)
        cs = csum[:, sl]
        mean = (cs[hi] - cs[lo]) / (hi - lo).astype(F32)[:, None]
        diffs.append(mean - u[:, sl])
    d = jnp.stack(diffs, axis=1)
    y = jnp.einsum('ngc,gcd->ngd', d, w_pool).reshape(n, POOL_WIDTH)
    return y * scale


def swa_latent(q, k, v, k_ctx, v_ctx, sink, shard):
    n, hq, d = q.shape
    blk = SWA_BLOCK
    a = jnp.arange(blk)[:, None]
    j = jnp.arange(3 * blk)[None, :]
    near = jnp.abs(j - blk - a) <= SWA_WINDOW
    tiles = jnp.stack([near & (j >= blk), near, near & (j < 2 * blk)])
    bias = jnp.where(tiles, 0.0, NEG_INF).astype(F32)[None]
    heads = lambda t: t.transpose(1, 0, 2)
    pad = lambda t: jnp.pad(heads(t), ((0, 0), (blk, blk), (0, 0)))
    o, gathered = win_attention(heads(q), pad(k), pad(v), heads(k_ctx), heads(v_ctx), bias, sink, shard, 'swa')
    return o.transpose(1, 0, 2).reshape(n, hq * d), gathered


def ctx_attention(q, k, v, sink):
    nq, hq, d = q.shape
    hkv = k.shape[1]
    grp = hq // hkv
    nk = k.shape[0]
    qg = q.reshape(nq, hkv, grp, d)
    s = jnp.einsum('qhgd,khd->hgqk', qg, k, preferred_element_type=F32) * (d ** -0.5)
    if sink is not None:
        s_sink = jnp.broadcast_to(sink.reshape(hkv, grp)[:, :, None, None], s.shape[:-1] + (1,))
        s = jnp.concatenate([s, s_sink], axis=-1)
    p = jax.nn.softmax(s, axis=-1)[..., :nk]
    o = jnp.einsum('hgqk,khd->qhgd', p, v)
    return o.reshape(nq, hq * d)


def na_bias_tiles(rpb, n):
    rows = n // GRID_W
    kh = min(NA_KH, rows)
    qc = jnp.arange(GRID_W)[:, None]
    kc = jnp.arange(GRID_W)[None, :]
    dc = jnp.clip(kc - qc, 1 - NA_KW, NA_KW - 1) + (NA_KW - 1)
    onehot = (dc[None] == jnp.arange(2 * NA_KW - 1)[:, None, None]).astype(F32)
    toeplitz = jnp.einsum('hdt,tqk->hdqk', rpb, onehot, precision=lax.Precision.HIGHEST)
    q_col0 = jnp.clip(qc - NA_KW // 2, 0, GRID_W - NA_KW)
    valid = (kc >= q_col0) & (kc < q_col0 + NA_KW)
    masked = jnp.where(valid, toeplitz, NEG_INF)
    return jnp.stack([jnp.concatenate([masked[:, off + j] for j in range(kh)], axis=-1) for off in range(NA_KH)], axis=1)


def na_latent(q, k, v, k_ctx, v_ctx, rpb, shard):
    n, h, d = q.shape
    heads = lambda t: t.transpose(1, 0, 2)
    o, gathered = win_attention(heads(q), heads(k), heads(v), heads(k_ctx), heads(v_ctx), na_bias_tiles(rpb, n),
                                jnp.zeros((h,), F32), shard, 'na')
    return o.transpose(1, 0, 2).reshape(n, h * d), gathered


def _assemble(g, name):
    _, r, c = g.shape
    w = g.reshape(N_DEV * r, c) if BIG[name] == 1 else g.transpose(1, 0, 2).reshape(r, N_DEV * c)
    if name == 'w_in':
        w = jnp.pad(w, ((0, 0), (0, IN_COLS_PAD - IN_COLS)))
    return w


LAYER0_CARRIERS = {'mla': 'ffn_w_up', 'na': 'ffn_w_down', 'swa': 'w_out'}


def _forward(x, mod_x, mod_c, gathered0, shards, ws, ctx):
    n = x.shape[0]
    depth = mod_x.shape[0]
    ang_mla = axial_angles(n, MLA_ROPE)
    ang_swa = axial_angles(n, SWA_HEAD_DIM)
    gathered = dict(gathered0)
    for l in range(depth):
        update_ctx = l < depth - 1
        cur, gathered = gathered, {}
        nxt = {name: shards[name][l + 1] if update_ctx else None for name in BIG}
        carried = {kind: shards[name][0] if l == 0 else None for kind, name in LAYER0_CARRIERS.items()}

        def weight(name):
            return _assemble(cur[name], name)

        def proj(a, name):
            y, gathered[name] = pmm(a, weight(name), nxt[name], name)
            return y

        sh_m, sc_m, gt_m, sh_f, sc_f, gt_f = jnp.split(mod_x[l], 6)
        csh_m, csc_m, cgt_m, csh_f, csc_f, cgt_f = jnp.split(mod_c[l], 6)

        def adaln_vec(g, *mods):
            return jnp.stack((g,) + mods + (jnp.zeros_like(g),) * (7 - len(mods)))

        p_all, gathered['w_in'] = norm_proj(x, ctx, adaln_vec(ws['g_mix'][l], sh_m, sc_m, csh_m, csc_m),
                                            weight('w_in'), nxt['w_in'], 'w_in')
        p_all = p_all[:, :IN_COLS]
        mla_p, pool_p, swa_p, na_p = jnp.split(p_all, IN_SPLITS, axis=-1)
        T = p_all.shape[0]

        cq, ckv, kr = jnp.split(mla_p, [MLA_Q_LORA, MLA_Q_LORA + MLA_KV_LORA], axis=-1)
        q = proj(rms_norm(cq, ws['mla_q_a_norm'][l]), 'mla_w_qb').reshape(T, MLA_HEADS, MLA_NOPE + MLA_ROPE)
        kv = proj(rms_norm(ckv, ws['mla_kv_a_norm'][l]), 'mla_w_kvb').reshape(T, MLA_HEADS, MLA_NOPE + MLA_V)
        q_nope = rms_norm(q[..., :MLA_NOPE], ws['mla_q_nope_norm'][l])
        q_rope = rope_latent(rms_norm(q[..., MLA_NOPE:], ws['mla_q_rope_norm'][l]), n, ang_mla)
        k_nope = rms_norm(kv[..., :MLA_NOPE], ws['mla_k_nope_norm'][l])
        v_mla = kv[..., MLA_NOPE:]
        k_rope = rope_latent(rms_norm(kr, ws['mla_k_rope_norm'][l])[:, None, :], n, ang_mla)
        q_cat = jnp.concatenate([q_nope, q_rope], axis=-1).transpose(1, 0, 2)
        k_cat = jnp.concatenate([k_nope, jnp.broadcast_to(k_rope, (T, MLA_HEADS, MLA_ROPE))], axis=-1).transpose(1, 0, 2)
        out_a, moved = attention(q_cat[:, :n], k_cat, v_mla.transpose(1, 0, 2), carried['mla'], MLA_SCALE)
        if l == 0:
            cur[LAYER0_CARRIERS['mla']] = moved

        out_b = pool_mixer(pool_p[:n], ws['pool_w'][l], ws['pool_scale'][l])

        sq, sk, sv = jnp.split(swa_p, [SWA_HEADS * SWA_HEAD_DIM, (SWA_HEADS + SWA_KV_HEADS) * SWA_HEAD_DIM], axis=-1)
        sq = rope_latent(rms_norm(sq.reshape(T, SWA_HEADS, SWA_HEAD_DIM), ws['swa_q_norm'][l]), n, ang_swa)
        sk = rope_latent(rms_norm(sk.reshape(T, SWA_KV_HEADS, SWA_HEAD_DIM), ws['swa_k_norm'][l]), n, ang_swa)
        sv = sv.reshape(T, SWA_KV_HEADS, SWA_HEAD_DIM)
        out_c, moved = swa_latent(sq[:n], sk[:n], sv[:n], sk[n:], sv[n:], ws['swa_sink'][l], carried['swa'])
        if l == 0:
            cur[LAYER0_CARRIERS['swa']] = moved

        nq_, nk_, nv_ = jnp.split(na_p, 3, axis=-1)
        nq_ = rms_norm(nq_.reshape(T, NA_HEADS, NA_HEAD_DIM), ws['na_q_norm'][l])
        nk_ = rms_norm(nk_.reshape(T, NA_HEADS, NA_HEAD_DIM), ws['na_k_norm'][l])
        nv_ = nv_.reshape(T, NA_HEADS, NA_HEAD_DIM)
        out_d, moved = na_latent(nq_[:n], nk_[:n], nv_[:n], nk_[n:], nv_[n:], ws['na_rpb'][l], carried['na'])
        if l == 0:
            cur[LAYER0_CARRIERS['na']] = moved

        mix_x = jnp.concatenate([out_a, out_b, out_c, out_d], axis=-1)

        def ffn(x_rows, ctx_rows):
            y, gathered['ffn_w_up'], gathered['ffn_w_down'] = conv_ffn(
                x_rows, ctx_rows, adaln_vec(ws['g_ffn'][l], sh_f, sc_f, csh_f, csc_f),
                weight('ffn_w_up'), ws['ffn_conv_w'][l], ws['ffn_conv_b'][l], weight('ffn_w_down'),
                nxt['ffn_w_up'], nxt['ffn_w_down'], n)
            return y

        if update_ctx:
            L = T - n
            mix_c = jnp.concatenate([
                mla_attend_ctx(q_nope[n:], q_rope[n:], k_nope[n:], k_rope[n:, 0], v_mla[n:]).reshape(L, MLA_HEADS * MLA_V),
                pool_mixer(pool_p[n:], ws['pool_w'][l], ws['pool_scale'][l]),
                ctx_attention(sq[n:], sk[n:], sv[n:], ws['swa_sink'][l]),
                ctx_attention(nq_[n:], nk_[n:], nv_[n:], None),
            ], axis=-1)
            o_all = proj(jnp.concatenate([mix_x, mix_c], axis=0), 'w_out')
            x = x + gt_m * o_all[:n]
            ctx = ctx + cgt_m * o_all[n:]
            f_all = ffn(x, ctx)
            x = x + gt_f * f_all[:n]
            ctx = ctx + cgt_f * f_all[n:]
        else:
            x = x + gt_m * proj(mix_x, 'w_out')
            x = x + gt_f * ffn(x, None)
    return x


def _pack_rows(vecs):
    flat = jnp.concatenate([v.reshape(-1).astype(F32) for v in vecs])
    pad = (-flat.shape[0]) % 1024
    return jnp.pad(flat, (0, pad)).reshape(-1, 128)


def _unpack(flat, shapes):
    out, off = [], 0
    for s in shapes:
        size = 1
        for d in s:
            size *= d
        out.append(flat[off:off + size].reshape(s))
        off += size
    return out


def _silu_grad(z):
    s = jax.nn.sigmoid(z)
    return s * (1.0 + z * (1.0 - s))


def kernel(x, c, ctx, c_ctx, w_mod, b_mod, g_mix, g_ffn, w_in, w_out, mla_q_a_norm, mla_w_qb, mla_kv_a_norm, mla_w_kvb, mla_q_nope_norm, mla_q_rope_norm, mla_k_nope_norm, mla_k_rope_norm, pool_w, pool_scale, swa_q_norm, swa_k_norm, swa_sink, na_q_norm, na_k_norm, na_rpb, ffn_w_up, ffn_conv_w, ffn_conv_b, ffn_w_down, loss_target, m_c_ctx, m_w_mod, m_b_mod, m_g_mix, m_g_ffn, m_w_in, m_w_out, m_mla_q_a_norm, m_mla_w_qb, m_mla_kv_a_norm, m_mla_w_kvb, m_mla_q_nope_norm, m_mla_q_rope_norm, m_mla_k_nope_norm, m_mla_k_rope_norm, m_pool_w, m_pool_scale, m_swa_q_norm, m_swa_k_norm, m_swa_sink, m_na_q_norm, m_na_k_norm, m_na_rpb, m_ffn_w_up, m_ffn_conv_w, m_ffn_conv_b, m_ffn_w_down, v_c_ctx, v_w_mod, v_b_mod, v_g_mix, v_g_ffn, v_w_in, v_w_out, v_mla_q_a_norm, v_mla_w_qb, v_mla_kv_a_norm, v_mla_w_kvb, v_mla_q_nope_norm, v_mla_q_rope_norm, v_mla_k_nope_norm, v_mla_k_rope_norm, v_pool_w, v_pool_scale, v_swa_q_norm, v_swa_k_norm, v_swa_sink, v_na_q_norm, v_na_k_norm, v_na_rpb, v_ffn_w_up, v_ffn_conv_w, v_ffn_conv_b, v_ffn_w_down):
    return _step(x, c, ctx, c_ctx, w_mod, b_mod, g_mix, g_ffn, w_in, w_out, mla_q_a_norm, mla_w_qb, mla_kv_a_norm, mla_w_kvb, mla_q_nope_norm, mla_q_rope_norm, mla_k_nope_norm, mla_k_rope_norm, pool_w, pool_scale, swa_q_norm, swa_k_norm, swa_sink, na_q_norm, na_k_norm, na_rpb, ffn_w_up, ffn_conv_w, ffn_conv_b, ffn_w_down, loss_target, m_c_ctx, m_w_mod, m_b_mod, m_g_mix, m_g_ffn, m_w_in, m_w_out, m_mla_q_a_norm, m_mla_w_qb, m_mla_kv_a_norm, m_mla_w_kvb, m_mla_q_nope_norm, m_mla_q_rope_norm, m_mla_k_nope_norm, m_mla_k_rope_norm, m_pool_w, m_pool_scale, m_swa_q_norm, m_swa_k_norm, m_swa_sink, m_na_q_norm, m_na_k_norm, m_na_rpb, m_ffn_w_up, m_ffn_conv_w, m_ffn_conv_b, m_ffn_w_down, v_c_ctx, v_w_mod, v_b_mod, v_g_mix, v_g_ffn, v_w_in, v_w_out, v_mla_q_a_norm, v_mla_w_qb, v_mla_kv_a_norm, v_mla_w_kvb, v_mla_q_nope_norm, v_mla_q_rope_norm, v_mla_k_nope_norm, v_mla_k_rope_norm, v_pool_w, v_pool_scale, v_swa_q_norm, v_swa_k_norm, v_swa_sink, v_na_q_norm, v_na_k_norm, v_na_rpb, v_ffn_w_up, v_ffn_conv_w, v_ffn_conv_b, v_ffn_w_down)


def _step(*args):
    n_in = len(ARG_NAMES)
    n_w = len(WEIGHTS)
    given = dict(zip(ARG_NAMES, args[:n_in]))
    mom = dict(zip(WEIGHTS, args[n_in:n_in + n_w]))
    var = dict(zip(WEIGHTS, args[n_in + n_w:n_in + 2 * n_w]))
    me = _my_index()

    x = given['x'][0]
    ctx = given['ctx'][0]
    target = given['loss_target'][0]
    n, D = x.shape
    depth = given['w_mod'].shape[0]
    mod_cols = given['w_mod'].shape[2]
    conv_cols = given['ffn_conv_w'].shape[2]

    late0 = tuple(LAYER0_CARRIERS.values())
    gathered0 = {name: _all_gather(given[name][0].astype(BF16), 'ag_' + name) for name in BIG if name not in late0}
    misc = _all_gather(_pack_rows([given['c'], given['ffn_conv_w']]), 'ag_cond')
    misc = misc.reshape(N_DEV, -1)
    c_all = misc[:, :D]
    conv_w = misc[:, D:D + depth * 3 * conv_cols].reshape(N_DEV, depth, 3, conv_cols)
    conv_w = conv_w.transpose(1, 2, 0, 3).reshape(depth, 3, N_DEV * conv_cols)

    cond = jnp.concatenate([c_all, given['c_ctx'][None], jnp.zeros((16 - N_DEV - 1, D), F32)], axis=0)
    s16 = jax.nn.silu(cond).astype(BF16)
    wm16 = given['w_mod'].astype(BF16)
    b_loc = lax.dynamic_slice_in_dim(given['b_mod'], me * mod_cols, mod_cols, axis=1)
    mod_part = jnp.stack([_mm(s16, wm16[l], 'nn', F32, 'mod_fwd') + b_loc[l] for l in range(depth)])
    mod_all = _all_gather(mod_part, 'ag_mod').transpose(1, 2, 0, 3).reshape(depth, 16, N_DEV * mod_cols)
    mod_x = lax.dynamic_index_in_dim(mod_all, me, axis=1, keepdims=False)
    mod_c = mod_all[:, N_DEV]

    ws = {name: given[name] for name in SMALL if name not in ('c_ctx', 'b_mod')}
    ws['ffn_conv_w'] = conv_w
    shards = {name: tuple(given[name][l] if l or name in late0 else None for l in range(depth)) for name in BIG}
    y, vjp = jax.vjp(lambda *d: _forward(*d, ctx), x, mod_x, mod_c, gathered0, shards, ws)
    loss_local, dy = _loss_head(y, target)
    g_x, g_mod_x, g_mod_c, g_gathered0, g_shards, g_ws = vjp(dy)

    g_big = {}
    for name in BIG:
        g0 = g_shards[name][0]
        if name not in late0:
            g0 = _sum_parts(_all_to_all(g_gathered0[name], 'a2a_' + name), 'sum_' + name)
        g_big[name] = jnp.stack((g0,) + tuple(g_shards[name][1:]))

    g_mod = jnp.zeros((depth, 16, N_DEV * mod_cols), F32)
    g_mod = lax.dynamic_update_slice_in_dim(g_mod, g_mod_x[:, None, :], me, axis=1)
    g_mod = g_mod.at[:, N_DEV].set(g_mod_c)
    g_mod_parts = g_mod.reshape(depth, 16, N_DEV, mod_cols).transpose(2, 0, 1, 3).reshape(N_DEV, -1)
    g_conv_parts = g_ws['ffn_conv_w'].reshape(depth, 3, N_DEV, conv_cols).transpose(2, 0, 1, 3).reshape(N_DEV, -1)
    n_mod = depth * 16 * mod_cols
    n_conv = depth * 3 * conv_cols
    f32_parts = jnp.concatenate([g_mod_parts, g_conv_parts], axis=1)
    f32_pad = (-f32_parts.shape[1]) % 1024
    f32_parts = jnp.pad(f32_parts, ((0, 0), (0, f32_pad))).reshape(N_DEV, -1, 128)
    f32_parts = _all_to_all(f32_parts, 'a2a_f32')
    f32_sum = _sum_parts(f32_parts, 'sum_f32').reshape(-1)
    g_mod_loc = f32_sum[:n_mod].reshape(depth, 16, mod_cols)
    g_conv_loc = f32_sum[n_mod:n_mod + n_conv].reshape(depth * 3, conv_cols)

    g_mod16 = g_mod_loc.astype(BF16)
    g_w_mod = jnp.stack([_mm(s16, g_mod16[l], 'tn', F32, 'mod_dw') for l in range(depth)])
    d_silu = sum(_mm(g_mod16[l], wm16[l], 'nt', F32, 'mod_dc') for l in range(depth))
    g_c_ctx_part = d_silu[N_DEV] * _silu_grad(given['c_ctx'])

    small_grads = {name: g_ws[name] for name in SMALL if name not in ('c_ctx', 'b_mod')}
    small_grads['c_ctx'] = g_c_ctx_part
    small_grads['b_mod'] = g_mod_x + g_mod_c
    small_shapes = [given[name].shape for name in SMALL]
    n_small = sum(int(given[name].size) for name in SMALL)
    packed = _pack_rows([small_grads[name] for name in SMALL] + [loss_local.reshape(1)])
    small_parts = _all_gather(packed, 'ag_small')

    out_g, out_d, out_m, out_v = {}, {}, {}, {}

    def update(name, parts):
        shape = given[name].shape
        flat = (shape[0] * shape[1], shape[2])
        res = _adamw(parts.reshape((-1,) + flat), given[name].reshape(flat), mom[name].reshape(flat),
                     var[name].reshape(flat), 'adamw_' + name)
        out_g[name], out_d[name], out_m[name], out_v[name] = (r.reshape(shape) for r in res)

    for name in BIG:
        update(name, g_big[name])
    update('w_mod', g_w_mod)
    update('ffn_conv_w', g_conv_loc)

    zero1 = jnp.zeros((1,), F32)
    res = _adamw(small_parts, _pack_rows([given[k] for k in SMALL] + [zero1]), _pack_rows([mom[k] for k in SMALL] + [zero1]),
                 _pack_rows([var[k] for k in SMALL] + [zero1 + 1.0]), 'adamw_small')
    flats = [r.reshape(-1) for r in res]
    for name, g_, d_, m_, v_ in zip(SMALL, *[_unpack(f, small_shapes) for f in flats]):
        out_g[name], out_d[name], out_m[name], out_v[name] = g_, d_, m_, v_
    loss = flats[0][n_small]

    return (loss, g_x[None], *[out_g[k] for k in WEIGHTS], *[out_d[k] for k in WEIGHTS],
            *[out_m[k] for k in WEIGHTS], *[out_v[k] for k in WEIGHTS])
```

```python
import functools

import jax
import jax.numpy as jnp
from jax import lax
from jax.experimental import pallas as pl
from jax.experimental.pallas import tpu as pltpu

F32 = jnp.float32
BF16 = jnp.bfloat16
N_DEV = 8
MESH_ID = pl.DeviceIdType.MESH

GRID_W = 64
ROPE_BASE = 10000.0
NORM_EPS = 1e-6
NEG_INF = -1e30

MLA_HEADS = 4
MLA_NOPE = 128
MLA_ROPE = 64
MLA_V = 128
MLA_Q_LORA = 512
MLA_KV_LORA = 256
MLA_SCALE = (MLA_NOPE + MLA_ROPE) ** -0.5
POOL_WINDOWS = (2, 4, 8, 16)
POOL_GROUP = 128
POOL_WIDTH = POOL_GROUP * len(POOL_WINDOWS)
SWA_HEADS = 8
SWA_KV_HEADS = 2
SWA_HEAD_DIM = 64
SWA_WINDOW = 128
SWA_BLOCK = 128
NA_HEADS = 8
NA_HEAD_DIM = 64
NA_KH = 8
NA_KW = 16
NA_QC = 16
NA_KC = NA_QC + NA_KW

A_COLS = MLA_Q_LORA + MLA_KV_LORA + MLA_ROPE
B_COLS = POOL_WIDTH
C_COLS = (SWA_HEADS + 2 * SWA_KV_HEADS) * SWA_HEAD_DIM
D_COLS = 3 * NA_HEADS * NA_HEAD_DIM
IN_COLS = A_COLS + B_COLS + C_COLS + D_COLS
IN_COLS_PAD = 3840
IN_SPLITS = (A_COLS, A_COLS + B_COLS, A_COLS + B_COLS + C_COLS)

ADAM_LR = 0.001
ADAM_B1 = 0.9
ADAM_B2 = 0.999
ADAM_EPS = 1e-08
ADAM_WD = 0.01
ADAM_STEP = 10

VMEM_LIMIT = 48 << 20

ARG_NAMES = ['x', 'c', 'ctx', 'c_ctx', 'w_mod', 'b_mod', 'g_mix', 'g_ffn', 'w_in', 'w_out', 'mla_q_a_norm', 'mla_w_qb', 'mla_kv_a_norm', 'mla_w_kvb', 'mla_q_nope_norm', 'mla_q_rope_norm', 'mla_k_nope_norm', 'mla_k_rope_norm', 'pool_w', 'pool_scale', 'swa_q_norm', 'swa_k_norm', 'swa_sink', 'na_q_norm', 'na_k_norm', 'na_rpb', 'ffn_w_up', 'ffn_conv_w', 'ffn_conv_b', 'ffn_w_down', 'loss_target']
WEIGHTS = ['c_ctx', 'w_mod', 'b_mod', 'g_mix', 'g_ffn', 'w_in', 'w_out', 'mla_q_a_norm', 'mla_w_qb', 'mla_kv_a_norm', 'mla_w_kvb', 'mla_q_nope_norm', 'mla_q_rope_norm', 'mla_k_nope_norm', 'mla_k_rope_norm', 'pool_w', 'pool_scale', 'swa_q_norm', 'swa_k_norm', 'swa_sink', 'na_q_norm', 'na_k_norm', 'na_rpb', 'ffn_w_up', 'ffn_conv_w', 'ffn_conv_b', 'ffn_w_down']
BIG = {'w_in': 2, 'w_out': 1, 'mla_w_qb': 2, 'mla_w_kvb': 2, 'ffn_w_up': 2, 'ffn_w_down': 1}
SMALL = ['c_ctx', 'b_mod', 'g_mix', 'g_ffn', 'mla_q_a_norm', 'mla_kv_a_norm', 'mla_q_nope_norm', 'mla_q_rope_norm', 'mla_k_nope_norm', 'mla_k_rope_norm', 'pool_w', 'pool_scale', 'swa_q_norm', 'swa_k_norm', 'swa_sink', 'na_q_norm', 'na_k_norm', 'na_rpb', 'ffn_conv_b']


def _pcall(body, **kw):
    return pl.pallas_call(body, **kw)


def _my_index():
    return 4 * lax.axis_index("x") + 2 * lax.axis_index("y") + lax.axis_index("c")


_COMM_SCRATCH = [pltpu.SemaphoreType.DMA((7,)), pltpu.SemaphoreType.DMA((7,)), pltpu.SemaphoreType.DMA(())]
_ANY = pl.BlockSpec(memory_space=pl.ANY)


def _gather_copies(x_ref, out_ref, send_sems, recv_sems, local_sem):
    x, y, c = lax.axis_index("x"), lax.axis_index("y"), lax.axis_index("c")
    me, sibling = (x, y, c), (x, y, 1 - c)
    chips = [(1 - x, y), (x, 1 - y), (1 - x, 1 - y)]

    def slot(px, py, pc):
        return out_ref.at[4 * px + 2 * py + pc]

    def copy(k, blk, to, src=None):
        return pltpu.make_async_remote_copy(
            src_ref=slot(*blk) if src is None else src, dst_ref=slot(*blk),
            send_sem=send_sems.at[k], recv_sem=recv_sems.at[k], device_id=to, device_id_type=MESH_ID)

    mine = pltpu.make_async_copy(x_ref, slot(*me), local_sem)
    first = [copy(0, me, sibling, src=x_ref)] + [copy(1 + j, me, (*chip, c), src=x_ref) for j, chip in enumerate(chips)]

    def start():
        mine.start()
        for cp in first:
            cp.start()

    def finish():
        passed = [copy(4 + j, (*chip, c), sibling) for j, chip in enumerate(chips)]
        for j, chip in enumerate(chips):
            copy(1 + j, (*chip, c), me).wait_recv()
            passed[j].start()
        copy(0, sibling, me).wait_recv()
        for j, chip in enumerate(chips):
            copy(4 + j, (*chip, 1 - c), me).wait_recv()
        for cp in first + passed:
            cp.wait_send()
        mine.wait()

    return start, finish


def _exchange_copies(t_ref, out_ref, send_sems, recv_sems, local_sem):
    x, y, c = lax.axis_index("x"), lax.axis_index("y"), lax.axis_index("c")
    me = 4 * x + 2 * y + c

    def peer(k):
        return (1 - x if k & 4 else x), (1 - y if k & 2 else y), (1 - c if k & 1 else c)

    def copy(k, landed):
        px, py, pc = peer(k)
        p = 4 * px + 2 * py + pc
        return pltpu.make_async_remote_copy(
            src_ref=t_ref.at[p], dst_ref=out_ref.at[p if landed else me],
            send_sem=send_sems.at[k - 1], recv_sem=recv_sems.at[k - 1], device_id=(px, py, pc), device_id_type=MESH_ID)

    mine = pltpu.make_async_copy(t_ref.at[me], out_ref.at[me], local_sem)
    sends = [copy(k, False) for k in range(1, N_DEV)]

    def start():
        mine.start()
        for cp in sends:
            cp.start()

    def finish():
        for k in range(1, N_DEV):
            copy(k, True).wait_recv()
        for cp in sends:
            cp.wait_send()
        mine.wait()

    return start, finish


_COMM = {'gather': _gather_copies, 'exchange': _exchange_copies}


def _comm_out_shape(kind, operand):
    shape = (N_DEV,) + operand.shape if kind == 'gather' else operand.shape
    return jax.ShapeDtypeStruct(shape, operand.dtype)


def _comm_call(kind, operand, name):
    def body(x_ref, out_ref, send_sems, recv_sems, local_sem):
        start, finish = _COMM[kind](x_ref, out_ref, send_sems, recv_sems, local_sem)
        start()
        finish()

    return _pcall(body, name=name, out_shape=_comm_out_shape(kind, operand), in_specs=[_ANY], out_specs=_ANY,
                  scratch_shapes=_COMM_SCRATCH)(operand)


def _all_gather(block, name):
    return _comm_call('gather', block, name)


def _all_to_all(parts, name):
    return _comm_call('exchange', parts, name)


def _carry_call(body, comm, name, grid, in_specs, out_specs, out_shape, scratch_shapes, semantics, operands):
    if comm is None:
        outs = _pcall(
            body, name=name, grid=grid, in_specs=in_specs, out_specs=out_specs, out_shape=out_shape,
            scratch_shapes=scratch_shapes,
            compiler_params=pltpu.CompilerParams(dimension_semantics=semantics, vmem_limit_bytes=VMEM_LIMIT),
        )(*operands)
        return outs, None

    kind, operand = comm
    n_in, n_out = len(in_specs), len(out_specs)
    steps = 1
    for g in grid:
        steps *= g

    def carrying(*refs):
        ins, x_ref = refs[:n_in], refs[n_in]
        outs, out_ref = refs[n_in + 1:n_in + 1 + n_out], refs[n_in + 1 + n_out]
        scratch, sems = refs[n_in + 2 + n_out:len(refs) - 3], refs[len(refs) - 3:]
        start, finish = _COMM[kind](x_ref, out_ref, *sems)
        step = pl.program_id(0)
        for axis in range(1, len(grid)):
            step = step * grid[axis] + pl.program_id(axis)

        @pl.when(step == 0)
        def _():
            start()

        body(*ins, *outs, *scratch)

        @pl.when(step == steps - 1)
        def _():
            finish()

    outs = _pcall(
        carrying, name=name, grid=grid, in_specs=list(in_specs) + [_ANY], out_specs=list(out_specs) + [_ANY],
        out_shape=list(out_shape) + [_comm_out_shape(kind, operand)], scratch_shapes=list(scratch_shapes) + _COMM_SCRATCH,
        compiler_params=pltpu.CompilerParams(dimension_semantics=("arbitrary",) * len(grid), vmem_limit_bytes=VMEM_LIMIT),
    )(*operands, operand)
    return outs[:-1], outs[-1]


_LANE_TILES = (1024, 768, 512, 384, 256, 128)
_ROW_TILES = (1088, 1024, 512, 256, 128)
_DEPTH_TILES = (2048, 1408) + _LANE_TILES
_TOKEN_DEPTH_TILES = (2176,) + _ROW_TILES


def _pick(dim, cands):
    for t in cands:
        if dim % t == 0:
            return t
    return dim


def _mm(a, b, mode, out_dtype, name, comm=None):
    if mode == 'nn':
        (M, K), (_, N) = a.shape, b.shape
        tm, tn, tk = _pick(M, _ROW_TILES), _pick(N, _LANE_TILES), _pick(K, _DEPTH_TILES)
        a_spec = pl.BlockSpec((tm, tk), lambda i, j, k: (i, k))
        b_spec = pl.BlockSpec((tk, tn), lambda i, j, k: (k, j))
        dn = (((1,), (0,)), ((), ()))
    elif mode == 'nt':
        (M, K), (N, _) = a.shape, b.shape
        tm, tn, tk = _pick(M, _ROW_TILES), _pick(N, _LANE_TILES), _pick(K, _DEPTH_TILES)
        a_spec = pl.BlockSpec((tm, tk), lambda i, j, k: (i, k))
        b_spec = pl.BlockSpec((tn, tk), lambda i, j, k: (j, k))
        dn = (((1,), (1,)), ((), ()))
    else:
        (K, M), (_, N) = a.shape, b.shape
        tm, tn, tk = _pick(M, _LANE_TILES), _pick(N, _LANE_TILES), _pick(K, _TOKEN_DEPTH_TILES)
        a_spec = pl.BlockSpec((tk, tm), lambda i, j, k: (k, i))
        b_spec = pl.BlockSpec((tk, tn), lambda i, j, k: (k, j))
        dn = (((0,), (0,)), ((), ()))
    grid = (M // tm, N // tn, K // tk)

    def matmul_step(a_ref, b_ref, o_ref, acc):
        @pl.when(pl.program_id(2) == 0)
        def _():
            acc[...] = jnp.zeros_like(acc)

        acc[...] += lax.dot_general(a_ref[...], b_ref[...], dn, preferred_element_type=F32)

        @pl.when(pl.program_id(2) == grid[2] - 1)
        def _():
            o_ref[...] = acc[...].astype(o_ref.dtype)

    outs, moved = _carry_call(
        matmul_step, comm, name, grid, [a_spec, b_spec], [pl.BlockSpec((tm, tn), lambda i, j, k: (i, j))],
        [jax.ShapeDtypeStruct((M, N), out_dtype)], [pltpu.VMEM((tm, tn), F32)], ("parallel", "parallel", "arbitrary"),
        (a, b))
    return outs[0] if comm is None else (outs[0], moved)


def _proj_fwd(a16, w, next_shard, name):
    if next_shard is None:
        return _mm(a16, w, 'nn', F32, name + '_nn'), None
    return _mm(a16, w, 'nn', F32, name + '_nn_gather', comm=('gather', next_shard.astype(BF16)))


def _proj_bwd(a16, w, dy, d_gathered, name):
    dy16 = dy.astype(BF16)
    if d_gathered is None:
        return _mm(dy16, w, 'nt', F32, name + '_nt'), _mm(a16, dy16, 'tn', BF16, name + '_tn'), None
    half = d_gathered.shape[1] // 2
    dw, top = _mm(a16, dy16, 'tn', BF16, name + '_tn_exchange', comm=('exchange', d_gathered[:, :half]))
    da, low = _mm(dy16, w, 'nt', F32, name + '_nt_exchange', comm=('exchange', d_gathered[:, half:]))
    d_shard = jnp.concatenate([_sum_parts(top, name + '_sum_top'), _sum_parts(low, name + '_sum_low')], axis=0)
    return da, dw, d_shard


@functools.partial(jax.custom_vjp, nondiff_argnums=(3,))
def pmm(a, w, next_shard, name):
    return _proj_fwd(a.astype(BF16), w, next_shard, name)


def _pmm_fwd(a, w, next_shard, name):
    a16 = a.astype(BF16)
    return _proj_fwd(a16, w, next_shard, name), (a16, w)


def _pmm_bwd(name, res, cts):
    return _proj_bwd(*res, cts[0], cts[1], name)


pmm.defvjp(_pmm_fwd, _pmm_bwd)


def _normmod_rows(x, ctx):
    n, d = x.shape
    rows = (n,) if ctx is None else (n, ctx.shape[0])
    tr = next(t for t in (256, 128, 64, 32, 16, 8) if all(r % t == 0 for r in rows))
    nx = n // tr
    specs = [pl.BlockSpec((tr, d), lambda i: (jnp.minimum(i, nx - 1), 0))]
    if ctx is not None:
        specs.append(pl.BlockSpec((tr, d), lambda i: (jnp.maximum(i - nx, 0), 0)))
    return tr, nx, sum(rows) // tr, specs


def _normmod_terms(refs, has_ctx, nx):
    is_x = pl.program_id(0) < nx
    vec_ref = refs[2] if has_ctx else refs[1]
    t = jnp.where(is_x, refs[0][...], refs[1][...]) if has_ctx else refs[0][...]
    shift = jnp.where(is_x, vec_ref[1:2, :], vec_ref[3:4, :])
    scale = jnp.where(is_x, vec_ref[2:3, :], vec_ref[4:5, :])
    r = lax.rsqrt(jnp.mean(t * t, axis=1, keepdims=True) + NORM_EPS)
    return is_x, t, r, vec_ref[0:1, :], shift, scale


def _normmod_fwd_call(x, ctx, vec):
    has_ctx = ctx is not None
    d = x.shape[1]
    tr, nx, blocks, specs = _normmod_rows(x, ctx)

    def body(*refs):
        _, t, r, g, shift, scale = _normmod_terms(refs, has_ctx, nx)
        refs[-1][...] = ((t * r * g) * (1.0 + scale) + shift).astype(BF16)

    operands = (x, ctx, vec) if has_ctx else (x, vec)
    return _pcall(
        body, name='normmod_fwd', grid=(blocks,), in_specs=specs + [pl.BlockSpec((8, d), lambda i: (0, 0))],
        out_specs=pl.BlockSpec((tr, d), lambda i: (i, 0)), out_shape=jax.ShapeDtypeStruct((blocks * tr, d), BF16),
        compiler_params=pltpu.CompilerParams(dimension_semantics=("parallel",), vmem_limit_bytes=VMEM_LIMIT),
    )(*operands)


def _normmod_bwd_call(x, ctx, vec, dh):
    has_ctx = ctx is not None
    d = x.shape[1]
    tr, nx, blocks, specs = _normmod_rows(x, ctx)
    n_in = len(specs) + 2

    def body(*refs):
        is_x, t, r, g, shift, scale = _normmod_terms(refs, has_ctx, nx)
        dh_ref, dx_ref, dvec_ref = refs[n_in - 1], refs[n_in], refs[-1]

        @pl.when(pl.program_id(0) == 0)
        def _():
            dvec_ref[...] = jnp.zeros_like(dvec_ref)

        xn = t * r
        dh = dh_ref[...]
        dy = dh * (1.0 + scale)
        dxn = dy * g
        dt = r * (dxn - xn * jnp.mean(dxn * xn, axis=1, keepdims=True))
        d_shift = jnp.sum(dh, axis=0, keepdims=True)
        d_scale = jnp.sum(dh * (xn * g), axis=0, keepdims=True)
        dvec_ref[0:1, :] += jnp.sum(dy * xn, axis=0, keepdims=True)

        @pl.when(is_x)
        def _():
            dx_ref[...] = dt
            dvec_ref[1:2, :] += d_shift
            dvec_ref[2:3, :] += d_scale

        if has_ctx:
            @pl.when(jnp.logical_not(is_x))
            def _():
                refs[n_in + 1][...] = dt
                dvec_ref[3:4, :] += d_shift
                dvec_ref[4:5, :] += d_scale

    vec_spec = pl.BlockSpec((8, d), lambda i: (0, 0))
    operands = (x, ctx, vec, dh) if has_ctx else (x, vec, dh)
    shapes = [jax.ShapeDtypeStruct(x.shape, F32)] + ([jax.ShapeDtypeStruct(ctx.shape, F32)] if has_ctx else [])
    outs = _pcall(
        body, name='normmod_bwd', grid=(blocks,),
        in_specs=specs + [vec_spec, pl.BlockSpec((tr, d), lambda i: (i, 0))],
        out_specs=specs + [vec_spec], out_shape=shapes + [jax.ShapeDtypeStruct((8, d), F32)],
        compiler_params=pltpu.CompilerParams(dimension_semantics=("arbitrary",), vmem_limit_bytes=VMEM_LIMIT),
    )(*operands)
    return (outs[0], outs[1], outs[2]) if has_ctx else (outs[0], None, outs[1])


@functools.partial(jax.custom_vjp, nondiff_argnums=(5,))
def norm_proj(x, ctx, vec, w, next_shard, name):
    return _norm_proj_fwd(x, ctx, vec, w, next_shard, name)[0]


def _norm_proj_fwd(x, ctx, vec, w, next_shard, name):
    h16 = _normmod_fwd_call(x, ctx, vec)
    return _proj_fwd(h16, w, next_shard, name), (x, ctx, vec, h16, w)


def _norm_proj_bwd(name, res, cts):
    x, ctx, vec, h16, w = res
    dh, dw, d_shard = _proj_bwd(h16, w, cts[0], cts[1], name)
    dx, dctx, dvec = _normmod_bwd_call(x, ctx, vec, dh)
    return dx, dctx, dvec, dw, d_shard


norm_proj.defvjp(_norm_proj_fwd, _norm_proj_bwd)


FFN_TILE = 128


def _neighbours(x, n):
    T = x.shape[0]
    t = lax.broadcasted_iota(jnp.int32, x.shape, 0)
    prev = jnp.where((t == 0) | (t == n), 0.0, pltpu.roll(x, 1, 0))
    nxt = jnp.where((t == n - 1) | (t == T - 1), 0.0, pltpu.roll(x, T - 1, 0))
    return prev, nxt


def _gate_specs(rows, f):
    tiles = f // FFN_TILE
    return [pl.BlockSpec((rows, FFN_TILE), lambda j: (0, j)), pl.BlockSpec((rows, FFN_TILE), lambda j: (0, j + tiles))]


def _conv3(x, cw_ref, cb_ref, n):
    prev, nxt = _neighbours(x, n)
    return prev * cw_ref[0:1, :] + x * cw_ref[1:2, :] + nxt * cw_ref[2:3, :] + cb_ref[...], prev, nxt


def _gate_fwd_call(a, cw, cb, n):
    T, f = a.shape[0], a.shape[1] // 2

    def body(ag_ref, av_ref, wg_ref, wv_ref, bg_ref, bv_ref, u_ref):
        g = _conv3(ag_ref[...], wg_ref, bg_ref, n)[0]
        v = _conv3(av_ref[...], wv_ref, bv_ref, n)[0]
        u_ref[...] = (g * jax.nn.sigmoid(g) * v).astype(u_ref.dtype)

    return _pcall(
        body, name='ffn_gate_fwd', grid=(f // FFN_TILE,),
        in_specs=_gate_specs(T, f) + _gate_specs(3, f) + _gate_specs(1, f),
        out_specs=pl.BlockSpec((T, FFN_TILE), lambda j: (0, j)),
        out_shape=jax.ShapeDtypeStruct((T, f), BF16),
        compiler_params=pltpu.CompilerParams(dimension_semantics=("parallel",), vmem_limit_bytes=VMEM_LIMIT),
    )(a, a, cw, cw, cb, cb)


def _gate_bwd_call(a, cw, cb, du, n):
    T, f = a.shape[0], a.shape[1] // 2

    def half(dz, x, prev, nxt, w_ref, da_ref, dw_ref):
        t = lax.broadcasted_iota(jnp.int32, dz.shape, 0)
        from_next = pltpu.roll(jnp.where((t == 0) | (t == n), 0.0, dz), T - 1, 0)
        from_prev = pltpu.roll(jnp.where((t == n - 1) | (t == T - 1), 0.0, dz), 1, 0)
        da_ref[...] = (dz * w_ref[1:2, :] + from_next * w_ref[0:1, :] + from_prev * w_ref[2:3, :]).astype(da_ref.dtype)
        dw_ref[0:1, :] = jnp.sum(dz * prev, axis=0, keepdims=True)
        dw_ref[1:2, :] = jnp.sum(dz * x, axis=0, keepdims=True)
        dw_ref[2:3, :] = jnp.sum(dz * nxt, axis=0, keepdims=True)
        dw_ref[3:4, :] = jnp.sum(dz, axis=0, keepdims=True)
        dw_ref[4:8, :] = jnp.zeros((4, FFN_TILE), F32)

    def body(ag_ref, av_ref, wg_ref, wv_ref, bg_ref, bv_ref, du_ref, dag_ref, dav_ref, dwg_ref, dwv_ref):
        xg, xv = ag_ref[...], av_ref[...]
        g, g_prev, g_next = _conv3(xg, wg_ref, bg_ref, n)
        v, v_prev, v_next = _conv3(xv, wv_ref, bv_ref, n)
        sg = jax.nn.sigmoid(g)
        du = du_ref[...]
        half(du * v * (sg * (1.0 + g * (1.0 - sg))), xg, g_prev, g_next, wg_ref, dag_ref, dwg_ref)
        half(du * (g * sg), xv, v_prev, v_next, wv_ref, dav_ref, dwv_ref)

    tile = lambda rows: pl.BlockSpec((rows, FFN_TILE), lambda j: (0, j))
    return _pcall(
        body, name='ffn_gate_bwd', grid=(f // FFN_TILE,),
        in_specs=_gate_specs(T, f) + _gate_specs(3, f) + _gate_specs(1, f) + [tile(T)],
        out_specs=[tile(T), tile(T), tile(8), tile(8)],
        out_shape=[jax.ShapeDtypeStruct((T, f), BF16), jax.ShapeDtypeStruct((T, f), BF16),
                   jax.ShapeDtypeStruct((8, f), F32), jax.ShapeDtypeStruct((8, f), F32)],
        compiler_params=pltpu.CompilerParams(dimension_semantics=("parallel",), vmem_limit_bytes=VMEM_LIMIT),
    )(a, a, cw, cw, cb, cb, du)


@functools.partial(jax.custom_vjp, nondiff_argnums=(9,))
def conv_ffn(x, ctx, vec, w_up, cw, cb, w_down, next_up, next_down, n):
    return _conv_ffn_fwd(x, ctx, vec, w_up, cw, cb, w_down, next_up, next_down, n)[0]


def _conv_ffn_fwd(x, ctx, vec, w_up, cw, cb, w_down, next_up, next_down, n):
    h16 = _normmod_fwd_call(x, ctx, vec)
    a, g_up = _proj_fwd(h16, w_up, next_up, 'ffn_up')
    u16 = _gate_fwd_call(a, cw, cb.reshape(1, -1), n)
    y, g_down = _proj_fwd(u16, w_down, next_down, 'ffn_down')
    return (y, g_up, g_down), (x, ctx, vec, h16, w_up, cw, cb, w_down, a, u16)


def _conv_ffn_bwd(n, res, cts):
    x, ctx, vec, h16, w_up, cw, cb, w_down, a, u16 = res
    dy, d_g_up, d_g_down = cts
    du, dw_down, d_next_down = _proj_bwd(u16, w_down, dy, d_g_down, 'ffn_down')
    dag, dav, dwg, dwv = _gate_bwd_call(a, cw, cb.reshape(1, -1), du, n)
    dcw = jnp.concatenate([dwg, dwv], axis=1)
    dh, dw_up, d_next_up = _proj_bwd(h16, w_up, jnp.concatenate([dag, dav], axis=1), d_g_up, 'ffn_up')
    dx, dctx, dvec = _normmod_bwd_call(x, ctx, vec, dh)
    return dx, dctx, dvec, dw_up, dcw[:3], dcw[3], dw_down, d_next_up, d_next_down


conv_ffn.defvjp(_conv_ffn_fwd, _conv_ffn_bwd)


def _attn_fwd_call(q, k, v, scale, comm=None):
    H, nq, dq = q.shape
    nk, dv = v.shape[1], v.shape[2]
    tq = _pick(nq, (256, 128))

    def body(q_ref, k_ref, v_ref, o_ref, lse_ref):
        s = lax.dot_general(q_ref[0], k_ref[0], (((1,), (1,)), ((), ())), preferred_element_type=F32) * scale
        m = jnp.max(s, axis=1, keepdims=True)
        p = jnp.exp(s - m)
        l = jnp.sum(p, axis=1, keepdims=True)
        pn = (p * (1.0 / l)).astype(BF16)
        o_ref[...] = jnp.dot(pn, v_ref[0], preferred_element_type=F32)
        lse_ref[0] = m + jnp.log(l)

    return _carry_call(
        body, comm, 'mla_attn_fwd', (H, nq // tq),
        [pl.BlockSpec((1, tq, dq), lambda h, i: (h, i, 0)),
         pl.BlockSpec((1, nk, dq), lambda h, i: (h, 0, 0)),
         pl.BlockSpec((1, nk, dv), lambda h, i: (h, 0, 0))],
        [pl.BlockSpec((tq, dv), lambda h, i: (i, h)), pl.BlockSpec((1, tq, 1), lambda h, i: (h, i, 0))],
        [jax.ShapeDtypeStruct((nq, H * dv), F32), jax.ShapeDtypeStruct((H, nq, 1), F32)], [],
        ("parallel", "parallel"), (q, k, v))


def _attn_bwd_call(q, k, v, o, lse, do, scale, comm=None):
    H, nq, dq = q.shape
    nk, dv = v.shape[1], v.shape[2]
    tq = _pick(nq, (128,))

    def body(q_ref, k_ref, v_ref, o_ref, lse_ref, do_ref, dq_ref, dk_ref, dv_ref):
        @pl.when(pl.program_id(1) == 0)
        def _():
            dk_ref[...] = jnp.zeros_like(dk_ref)
            dv_ref[...] = jnp.zeros_like(dv_ref)

        q16, k16, v16 = q_ref[0], k_ref[0], v_ref[0]
        do = do_ref[...]
        do16 = do.astype(BF16)
        s = lax.dot_general(q16, k16, (((1,), (1,)), ((), ())), preferred_element_type=F32) * scale
        p = jnp.exp(s - lse_ref[0])
        dv_ref[0] += lax.dot_general(p.astype(BF16), do16, (((0,), (0,)), ((), ())), preferred_element_type=F32)
        dp = lax.dot_general(do16, v16, (((1,), (1,)), ((), ())), preferred_element_type=F32)
        delta = jnp.sum(do * o_ref[...], axis=1, keepdims=True)
        ds16 = (p * (dp - delta) * scale).astype(BF16)
        dq_ref[0] = jnp.dot(ds16, k16, preferred_element_type=F32)
        dk_ref[0] += lax.dot_general(ds16, q16, (((0,), (0,)), ((), ())), preferred_element_type=F32)

    return _carry_call(
        body, comm, 'mla_attn_bwd', (H, nq // tq),
        [pl.BlockSpec((1, tq, dq), lambda h, i: (h, i, 0)),
         pl.BlockSpec((1, nk, dq), lambda h, i: (h, 0, 0)),
         pl.BlockSpec((1, nk, dv), lambda h, i: (h, 0, 0)),
         pl.BlockSpec((tq, dv), lambda h, i: (i, h)),
         pl.BlockSpec((1, tq, 1), lambda h, i: (h, i, 0)),
         pl.BlockSpec((tq, dv), lambda h, i: (i, h))],
        [pl.BlockSpec((1, tq, dq), lambda h, i: (h, i, 0)),
         pl.BlockSpec((1, nk, dq), lambda h, i: (h, 0, 0)),
         pl.BlockSpec((1, nk, dv), lambda h, i: (h, 0, 0))],
        [jax.ShapeDtypeStruct((H, nq, dq), F32), jax.ShapeDtypeStruct((H, nk, dq), F32),
         jax.ShapeDtypeStruct((H, nk, dv), F32)], [], ("parallel", "arbitrary"), (q, k, v, o, lse, do))


def _gather_comm(shard):
    return None if shard is None else ('gather', shard.astype(BF16))


def _exchange_comm(d_gathered):
    return None if d_gathered is None else ('exchange', d_gathered)


def _reduce_moved(parts, name):
    return None if parts is None else _sum_parts(parts, name)


@functools.partial(jax.custom_vjp, nondiff_argnums=(4,))
def attention(q, k, v, shard, scale):
    (o, _), gathered = _attn_fwd_call(q.astype(BF16), k.astype(BF16), v.astype(BF16), scale, _gather_comm(shard))
    return o, gathered


def _attention_fwd(q, k, v, shard, scale):
    q16, k16, v16 = q.astype(BF16), k.astype(BF16), v.astype(BF16)
    (o, lse), gathered = _attn_fwd_call(q16, k16, v16, scale, _gather_comm(shard))
    return (o, gathered), (q16, k16, v16, o, lse)


def _attention_bwd(scale, res, cts):
    q16, k16, v16, o, lse = res
    grads, parts = _attn_bwd_call(q16, k16, v16, o, lse, cts[0], scale, _exchange_comm(cts[1]))
    return tuple(grads) + (_reduce_moved(parts, 'mla_attn_sum'),)


attention.defvjp(_attention_fwd, _attention_bwd)


def _win_geometry(kind, n):
    if kind == 'na':
        rows = n // GRID_W
        kh = min(NA_KH, rows)

        def start(i):
            return jnp.clip(i - kh // 2, 0, rows - kh) * GRID_W

        def bidx(i):
            return jnp.clip(i - kh // 2, 0, rows - kh) - i + (NA_KH - 1)

        return GRID_W, kh * GRID_W, start, bidx
    nb = n // SWA_BLOCK

    def start(i):
        return i * SWA_BLOCK

    def bidx(i):
        return jnp.where(i == 0, 0, jnp.where(i == nb - 1, 2, 1))

    return SWA_BLOCK, 3 * SWA_BLOCK, start, bidx


def _dot_nt(a, b):
    return lax.dot_general(a, b, (((1,), (1,)), ((), ())), preferred_element_type=F32)


def _dot_tn(a, b):
    return lax.dot_general(a, b, (((0,), (0,)), ((), ())), preferred_element_type=F32)


def _win_specs(q, k, kc, bias):
    hq, n, d = q.shape
    grp = hq // k.shape[0]
    hb = bias.shape[0]
    return [
        pl.BlockSpec((1, n, d), lambda h: (h, 0, 0)),
        pl.BlockSpec((1,) + k.shape[1:], lambda h: (h // grp, 0, 0)),
        pl.BlockSpec((1,) + k.shape[1:], lambda h: (h // grp, 0, 0)),
        pl.BlockSpec((1,) + kc.shape[1:], lambda h: (h // grp, 0, 0)),
        pl.BlockSpec((1,) + kc.shape[1:], lambda h: (h // grp, 0, 0)),
        pl.BlockSpec((1,) + bias.shape[1:], (lambda h: (h, 0, 0, 0)) if hb > 1 else (lambda h: (0, 0, 0, 0))),
        pl.BlockSpec(memory_space=pltpu.SMEM),
    ]


def _win_fwd_call(q, k, v, kc, vc, bias, sink, kind, comm=None):
    hq, n, d = q.shape
    scale = d ** -0.5
    qb, wk, start, bidx = _win_geometry(kind, n)
    has_sink = kind == 'swa'

    def body(q_ref, k_ref, v_ref, kc_ref, vc_ref, b_ref, sink_ref, o_ref, lse_ref):
        kc16, vc16 = kc_ref[0], vc_ref[0]
        snk = sink_ref[pl.program_id(0)]

        def step(i, carry):
            qs = pl.multiple_of(i * qb, qb)
            ks = pl.multiple_of(start(i), GRID_W)
            q16 = q_ref[0, pl.ds(qs, qb), :]
            s1 = _dot_nt(q16, k_ref[0, pl.ds(ks, wk), :]) * scale + b_ref[0, bidx(i)]
            s2 = _dot_nt(q16, kc16) * scale
            m = jnp.maximum(jnp.max(s1, axis=1, keepdims=True), jnp.max(s2, axis=1, keepdims=True))
            if has_sink:
                m = jnp.maximum(m, snk)
            p1 = jnp.exp(s1 - m)
            p2 = jnp.exp(s2 - m)
            l = jnp.sum(p1, axis=1, keepdims=True) + jnp.sum(p2, axis=1, keepdims=True)
            if has_sink:
                l = l + jnp.exp(snk - m)
            inv = 1.0 / l
            o = (jnp.dot((p1 * inv).astype(BF16), v_ref[0, pl.ds(ks, wk), :], preferred_element_type=F32)
                 + jnp.dot((p2 * inv).astype(BF16), vc16, preferred_element_type=F32))
            o_ref[0, pl.ds(qs, qb), :] = o
            lse_ref[0, pl.ds(qs, qb), :] = m + jnp.log(l)
            return carry

        lax.fori_loop(0, n // qb, step, 0, unroll=4)

    return _carry_call(
        body, comm, kind + '_attn_fwd', (hq,), _win_specs(q, k, kc, bias),
        [pl.BlockSpec((1, n, d), lambda h: (h, 0, 0)), pl.BlockSpec((1, n, 1), lambda h: (h, 0, 0))],
        [jax.ShapeDtypeStruct((hq, n, d), F32), jax.ShapeDtypeStruct((hq, n, 1), F32)], [], ("parallel",),
        (q, k, v, kc, vc, bias, sink))


def _win_bwd_call(q, k, v, kc, vc, bias, sink, o, lse, do, kind, comm=None):
    hq, n, d = q.shape
    scale = d ** -0.5
    qb, wk, start, bidx = _win_geometry(kind, n)
    has_sink = kind == 'swa'
    bias_grad = kind == 'na'

    def body(q_ref, k_ref, v_ref, kc_ref, vc_ref, b_ref, sink_ref, o_ref, lse_ref, do_ref,
             dq_ref, dk_ref, dv_ref, dkc_ref, dvc_ref, db_ref, dsink_ref):
        kc16, vc16 = kc_ref[0], vc_ref[0]
        snk = sink_ref[pl.program_id(0)]
        dk_ref[...] = jnp.zeros_like(dk_ref)
        dv_ref[...] = jnp.zeros_like(dv_ref)
        dkc_ref[...] = jnp.zeros_like(dkc_ref)
        dvc_ref[...] = jnp.zeros_like(dvc_ref)
        db_ref[...] = jnp.zeros_like(db_ref)

        def step(i, dsink):
            qs = pl.multiple_of(i * qb, qb)
            ks = pl.multiple_of(start(i), GRID_W)
            q16 = q_ref[0, pl.ds(qs, qb), :]
            k16 = k_ref[0, pl.ds(ks, wk), :]
            v16 = v_ref[0, pl.ds(ks, wk), :]
            lse = lse_ref[0, pl.ds(qs, qb), :]
            do = do_ref[0, pl.ds(qs, qb), :]
            do16 = do.astype(BF16)
            p1 = jnp.exp(_dot_nt(q16, k16) * scale + b_ref[0, bidx(i)] - lse)
            p2 = jnp.exp(_dot_nt(q16, kc16) * scale - lse)
            delta = jnp.sum(do * o_ref[0, pl.ds(qs, qb), :], axis=1, keepdims=True)
            ds1 = p1 * (_dot_nt(do16, v16) - delta)
            ds2 = p2 * (_dot_nt(do16, vc16) - delta)
            if bias_grad:
                db_ref[0, bidx(i)] += ds1
            ds1 = (ds1 * scale).astype(BF16)
            ds2 = (ds2 * scale).astype(BF16)
            dq_ref[0, pl.ds(qs, qb), :] = (jnp.dot(ds1, k16, preferred_element_type=F32)
                                          + jnp.dot(ds2, kc16, preferred_element_type=F32))
            dk_ref[0, pl.ds(ks, wk), :] += _dot_tn(ds1, q16)
            dv_ref[0, pl.ds(ks, wk), :] += _dot_tn(p1.astype(BF16), do16)
            dkc_ref[0] += _dot_tn(ds2, q16)
            dvc_ref[0] += _dot_tn(p2.astype(BF16), do16)
            if has_sink:
                dsink = dsink - jnp.sum(jnp.exp(snk - lse) * delta)
            return dsink

        dsink = lax.fori_loop(0, n // qb, step, jnp.zeros((), F32), unroll=2)
        dsink_ref[...] = jnp.full(dsink_ref.shape, dsink, F32)

    per_head = lambda shape: pl.BlockSpec((1,) + shape[1:], lambda h: (h,) + (0,) * (len(shape) - 1))
    kq = (hq,) + k.shape[1:]
    cq = (hq,) + kc.shape[1:]
    bq = (hq,) + bias.shape[1:]
    in_specs = _win_specs(q, k, kc, bias) + [per_head(o.shape), per_head(lse.shape), per_head(do.shape)]
    out_shapes = [q.shape, kq, kq, cq, cq, bq, (hq, 8, 128)]
    return _carry_call(
        body, comm, kind + '_attn_bwd', (hq,), in_specs, [per_head(s) for s in out_shapes],
        [jax.ShapeDtypeStruct(s, F32) for s in out_shapes], [], ("parallel",),
        (q, k, v, kc, vc, bias, sink, o, lse, do))


@functools.partial(jax.custom_vjp, nondiff_argnums=(8,))
def win_attention(q, k, v, kc, vc, bias, sink, shard, kind):
    b16 = lambda t: t.astype(BF16)
    (o, _), gathered = _win_fwd_call(b16(q), b16(k), b16(v), b16(kc), b16(vc), bias, sink, kind, _gather_comm(shard))
    return o, gathered


def _win_attention_fwd(q, k, v, kc, vc, bias, sink, shard, kind):
    res = tuple(t.astype(BF16) for t in (q, k, v, kc, vc)) + (bias, sink)
    (o, lse), gathered = _win_fwd_call(*res, kind, _gather_comm(shard))
    return (o, gathered), res + (o, lse)


def _win_attention_bwd(kind, res, cts):
    q, k, v, kc, vc, bias, sink, o, lse = res
    (dq, dk, dv, dkc, dvc, db, dsink), parts = _win_bwd_call(
        q, k, v, kc, vc, bias, sink, o, lse, cts[0], kind, _exchange_comm(cts[1]))
    hkv = k.shape[0]
    fold = lambda t: t.reshape((hkv, -1) + t.shape[1:]).sum(axis=1)
    if bias.shape[0] == 1:
        db = jnp.zeros_like(bias)
    return (dq, fold(dk), fold(dv), fold(dkc), fold(dvc), db, dsink[:, 0, 0], _reduce_moved(parts, kind + '_attn_sum'))


win_attention.defvjp(_win_attention_fwd, _win_attention_bwd)


def _loss_head(y, target):
    n, d = y.shape
    tr = _pick(n, (512, 256, 128))
    nb = n // tr

    def body(y_ref, t_ref, dy_ref, part_ref):
        err = y_ref[...] - t_ref[...]
        dy_ref[...] = err * (1.0 / d)
        part_ref[...] = jnp.full(part_ref.shape, jnp.sum(err * err), F32)

    dy, part = _pcall(
        body, name='loss_head', grid=(nb,),
        in_specs=[pl.BlockSpec((tr, d), lambda i: (i, 0)), pl.BlockSpec((tr, d), lambda i: (i, 0))],
        out_specs=[pl.BlockSpec((tr, d), lambda i: (i, 0)), pl.BlockSpec((1, 8, 128), lambda i: (i, 0, 0))],
        out_shape=[jax.ShapeDtypeStruct((n, d), F32), jax.ShapeDtypeStruct((nb, 8, 128), F32)],
        compiler_params=pltpu.CompilerParams(dimension_semantics=("parallel",), vmem_limit_bytes=VMEM_LIMIT),
    )(y, target)
    return 0.5 * jnp.sum(part[:, 0, 0]) / d, dy


def _sum_parts(parts, name):
    P, R, C = parts.shape
    tr = _pick(R, (256, 128, 64, 32, 16, 8))

    def body(p_ref, o_ref):
        acc = p_ref[0].astype(F32)
        for i in range(1, P):
            acc = acc + p_ref[i].astype(F32)
        o_ref[...] = acc

    return _pcall(
        body, name=name, grid=(R // tr,),
        in_specs=[pl.BlockSpec((P, tr, C), lambda i: (0, i, 0))],
        out_specs=pl.BlockSpec((tr, C), lambda i: (i, 0)),
        out_shape=jax.ShapeDtypeStruct((R, C), F32),
        compiler_params=pltpu.CompilerParams(dimension_semantics=("parallel",), vmem_limit_bytes=VMEM_LIMIT),
    )(parts)


def _adamw(parts, w, m, v, name):
    P, R, C = parts.shape
    tr = _pick(R, (128, 64, 32, 16, 8))
    c1 = 1.0 / (1.0 - ADAM_B1 ** ADAM_STEP)
    c2 = 1.0 / (1.0 - ADAM_B2 ** ADAM_STEP)

    def body(p_ref, w_ref, m_ref, v_ref, g_out, d_out, m_out, v_out):
        g = p_ref[0].astype(F32)
        for i in range(1, P):
            g = g + p_ref[i].astype(F32)
        m_new = ADAM_B1 * m_ref[...] + (1.0 - ADAM_B1) * g
        v_new = ADAM_B2 * v_ref[...] + (1.0 - ADAM_B2) * (g * g)
        g_out[...] = g
        m_out[...] = m_new
        v_out[...] = v_new
        d_out[...] = -ADAM_LR * ((m_new * c1) / (jnp.sqrt(v_new * c2) + ADAM_EPS) + ADAM_WD * w_ref[...])

    blk = pl.BlockSpec((tr, C), lambda i: (i, 0))
    return _pcall(
        body, name=name, grid=(R // tr,),
        in_specs=[pl.BlockSpec((P, tr, C), lambda i: (0, i, 0)), blk, blk, blk],
        out_specs=[blk, blk, blk, blk],
        out_shape=[jax.ShapeDtypeStruct((R, C), F32)] * 4,
        compiler_params=pltpu.CompilerParams(dimension_semantics=("parallel",), vmem_limit_bytes=VMEM_LIMIT),
    )(parts, w, m, v)


def rms_norm(x, g):
    return x * lax.rsqrt(jnp.mean(x * x, axis=-1, keepdims=True) + NORM_EPS) * g


def modulate(h, shift, scale):
    return h * (1.0 + scale) + shift


def axial_angles(n, d_rot):
    t = jnp.arange(n)
    row = (t // GRID_W).astype(F32)
    col = (t % GRID_W).astype(F32)
    d_axis = d_rot // 2
    inv_freq = ROPE_BASE ** (-jnp.arange(0, d_axis, 2, dtype=F32) / d_axis)
    return (row[:, None] * inv_freq, col[:, None] * inv_freq)


def rope_segment(x, ang):
    cos = jnp.cos(ang)[:, None, :]
    sin = jnp.sin(ang)[:, None, :]
    x1, x2 = jnp.split(x, 2, axis=-1)
    return jnp.concatenate([x1 * cos - x2 * sin, x2 * cos + x1 * sin], axis=-1)


def axial_rope(x, ang):
    half = x.shape[-1] // 2
    return jnp.concatenate([rope_segment(x[..., :half], ang[0]), rope_segment(x[..., half:], ang[1])], axis=-1)


def rope_latent(t, n, ang):
    return jnp.concatenate([axial_rope(t[:n], ang), t[n:]], axis=0)


def mla_attend_ctx(q_nope, q_rope, k_nope, k_rope, v):
    s = (jnp.einsum('qhd,khd->hqk', q_nope, k_nope, preferred_element_type=F32)
         + jnp.einsum('qhr,kr->hqk', q_rope, k_rope, preferred_element_type=F32))
    p = jax.nn.softmax(s * MLA_SCALE, axis=-1)
    return jnp.einsum('hqk,khd->qhd', p, v)


def pool_mixer(u, w_pool, scale):
    n = u.shape[0]
    t = jnp.arange(n)
    diffs = []
    for g, w in enumerate(POOL_WINDOWS):
        ug = u[:, g * POOL_GROUP:(g + 1) * POOL_GROUP]
        padded = jnp.pad(ug, ((w // 2, w // 2), (0, 0)))
        total = padded[0:n]
        for j in range(1, w):
            total = total + padded[j:j + n]
        count = (jnp.clip(t + w // 2, 0, n) - jnp.clip(t - w // 2, 0, n)).astype(F32)
        diffs.append(total / count[:, None] - ug)
    d = jnp.stack(diffs, axis=1)
    y = jnp.einsum('ngc,gcd->ngd', d, w_pool).reshape(n, POOL_WIDTH)
    return y * scale


def swa_latent(q, k, v, k_ctx, v_ctx, sink, shard):
    n, hq, d = q.shape
    blk = SWA_BLOCK
    a = jnp.arange(blk)[:, None]
    j = jnp.arange(3 * blk)[None, :]
    near = jnp.abs(j - blk - a) <= SWA_WINDOW
    tiles = jnp.stack([near & (j >= blk), near, near & (j < 2 * blk)])
    bias = jnp.where(tiles, 0.0, NEG_INF).astype(F32)[None]
    heads = lambda t: t.transpose(1, 0, 2)
    pad = lambda t: jnp.pad(heads(t), ((0, 0), (blk, blk), (0, 0)))
    o, gathered = win_attention(heads(q), pad(k), pad(v), heads(k_ctx), heads(v_ctx), bias, sink, shard, 'swa')
    return o.transpose(1, 0, 2).reshape(n, hq * d), gathered


def ctx_attention(q, k, v, sink):
    nq, hq, d = q.shape
    hkv = k.shape[1]
    grp = hq // hkv
    nk = k.shape[0]
    qg = q.reshape(nq, hkv, grp, d)
    s = jnp.einsum('qhgd,khd->hgqk', qg, k, preferred_element_type=F32) * (d ** -0.5)
    if sink is not None:
        s_sink = jnp.broadcast_to(sink.reshape(hkv, grp)[:, :, None, None], s.shape[:-1] + (1,))
        s = jnp.concatenate([s, s_sink], axis=-1)
    p = jax.nn.softmax(s, axis=-1)[..., :nk]
    o = jnp.einsum('hgqk,khd->qhgd', p, v)
    return o.reshape(nq, hq * d)


def na_bias_tiles(rpb, n):
    rows = n // GRID_W
    kh = min(NA_KH, rows)
    qc = jnp.arange(GRID_W)[:, None]
    kc = jnp.arange(GRID_W)[None, :]
    dc = jnp.clip(kc - qc, 1 - NA_KW, NA_KW - 1) + (NA_KW - 1)
    onehot = (dc[None] == jnp.arange(2 * NA_KW - 1)[:, None, None]).astype(F32)
    toeplitz = jnp.einsum('hdt,tqk->hdqk', rpb, onehot, precision=lax.Precision.HIGHEST)
    q_col0 = jnp.clip(qc - NA_KW // 2, 0, GRID_W - NA_KW)
    valid = (kc >= q_col0) & (kc < q_col0 + NA_KW)
    masked = jnp.where(valid, toeplitz, NEG_INF)
    return jnp.stack([jnp.concatenate([masked[:, off + j] for j in range(kh)], axis=-1) for off in range(NA_KH)], axis=1)


def na_latent(q, k, v, k_ctx, v_ctx, rpb, shard):
    n, h, d = q.shape
    heads = lambda t: t.transpose(1, 0, 2)
    o, gathered = win_attention(heads(q), heads(k), heads(v), heads(k_ctx), heads(v_ctx), na_bias_tiles(rpb, n),
                                jnp.zeros((h,), F32), shard, 'na')
    return o.transpose(1, 0, 2).reshape(n, h * d), gathered


def _assemble(g, name):
    _, r, c = g.shape
    w = g.reshape(N_DEV * r, c) if BIG[name] == 1 else g.transpose(1, 0, 2).reshape(r, N_DEV * c)
    if name == 'w_in':
        w = jnp.pad(w, ((0, 0), (0, IN_COLS_PAD - IN_COLS)))
    return w


LAYER0_CARRIERS = {'mla': 'ffn_w_up', 'na': 'ffn_w_down', 'swa': 'w_out'}


def _forward(x, mod_x, mod_c, gathered0, shards, ws, ctx):
    n = x.shape[0]
    depth = mod_x.shape[0]
    ang_mla = axial_angles(n, MLA_ROPE)
    ang_swa = axial_angles(n, SWA_HEAD_DIM)
    gathered = dict(gathered0)
    for l in range(depth):
        update_ctx = l < depth - 1
        cur, gathered = gathered, {}
        nxt = {name: shards[name][l + 1] if update_ctx else None for name in BIG}
        carried = {kind: shards[name][0] if l == 0 else None for kind, name in LAYER0_CARRIERS.items()}

        def weight(name):
            return _assemble(cur[name], name)

        def proj(a, name):
            y, gathered[name] = pmm(a, weight(name), nxt[name], name)
            return y

        sh_m, sc_m, gt_m, sh_f, sc_f, gt_f = jnp.split(mod_x[l], 6)
        csh_m, csc_m, cgt_m, csh_f, csc_f, cgt_f = jnp.split(mod_c[l], 6)

        def adaln_vec(g, *mods):
            return jnp.stack((g,) + mods + (jnp.zeros_like(g),) * (7 - len(mods)))

        p_all, gathered['w_in'] = norm_proj(x, ctx, adaln_vec(ws['g_mix'][l], sh_m, sc_m, csh_m, csc_m),
                                            weight('w_in'), nxt['w_in'], 'w_in')
        p_all = p_all[:, :IN_COLS]
        mla_p, pool_p, swa_p, na_p = jnp.split(p_all, IN_SPLITS, axis=-1)
        T = p_all.shape[0]

        cq, ckv, kr = jnp.split(mla_p, [MLA_Q_LORA, MLA_Q_LORA + MLA_KV_LORA], axis=-1)
        q = proj(rms_norm(cq, ws['mla_q_a_norm'][l]), 'mla_w_qb').reshape(T, MLA_HEADS, MLA_NOPE + MLA_ROPE)
        kv = proj(rms_norm(ckv, ws['mla_kv_a_norm'][l]), 'mla_w_kvb').reshape(T, MLA_HEADS, MLA_NOPE + MLA_V)
        q_nope = rms_norm(q[..., :MLA_NOPE], ws['mla_q_nope_norm'][l])
        q_rope = rope_latent(rms_norm(q[..., MLA_NOPE:], ws['mla_q_rope_norm'][l]), n, ang_mla)
        k_nope = rms_norm(kv[..., :MLA_NOPE], ws['mla_k_nope_norm'][l])
        v_mla = kv[..., MLA_NOPE:]
        k_rope = rope_latent(rms_norm(kr, ws['mla_k_rope_norm'][l])[:, None, :], n, ang_mla)
        q_cat = jnp.concatenate([q_nope, q_rope], axis=-1).transpose(1, 0, 2)
        k_cat = jnp.concatenate([k_nope, jnp.broadcast_to(k_rope, (T, MLA_HEADS, MLA_ROPE))], axis=-1).transpose(1, 0, 2)
        out_a, moved = attention(q_cat[:, :n], k_cat, v_mla.transpose(1, 0, 2), carried['mla'], MLA_SCALE)
        if l == 0:
            cur[LAYER0_CARRIERS['mla']] = moved

        out_b = pool_mixer(pool_p[:n], ws['pool_w'][l], ws['pool_scale'][l])

        sq, sk, sv = jnp.split(swa_p, [SWA_HEADS * SWA_HEAD_DIM, (SWA_HEADS + SWA_KV_HEADS) * SWA_HEAD_DIM], axis=-1)
        sq = rope_latent(rms_norm(sq.reshape(T, SWA_HEADS, SWA_HEAD_DIM), ws['swa_q_norm'][l]), n, ang_swa)
        sk = rope_latent(rms_norm(sk.reshape(T, SWA_KV_HEADS, SWA_HEAD_DIM), ws['swa_k_norm'][l]), n, ang_swa)
        sv = sv.reshape(T, SWA_KV_HEADS, SWA_HEAD_DIM)
        out_c, moved = swa_latent(sq[:n], sk[:n], sv[:n], sk[n:], sv[n:], ws['swa_sink'][l], carried['swa'])
        if l == 0:
            cur[LAYER0_CARRIERS['swa']] = moved

        nq_, nk_, nv_ = jnp.split(na_p, 3, axis=-1)
        nq_ = rms_norm(nq_.reshape(T, NA_HEADS, NA_HEAD_DIM), ws['na_q_norm'][l])
        nk_ = rms_norm(nk_.reshape(T, NA_HEADS, NA_HEAD_DIM), ws['na_k_norm'][l])
        nv_ = nv_.reshape(T, NA_HEADS, NA_HEAD_DIM)
        out_d, moved = na_latent(nq_[:n], nk_[:n], nv_[:n], nk_[n:], nv_[n:], ws['na_rpb'][l], carried['na'])
        if l == 0:
            cur[LAYER0_CARRIERS['na']] = moved

        mix_x = jnp.concatenate([out_a, out_b, out_c, out_d], axis=-1)

        def ffn(x_rows, ctx_rows):
            y, gathered['ffn_w_up'], gathered['ffn_w_down'] = conv_ffn(
                x_rows, ctx_rows, adaln_vec(ws['g_ffn'][l], sh_f, sc_f, csh_f, csc_f),
                weight('ffn_w_up'), ws['ffn_conv_w'][l], ws['ffn_conv_b'][l], weight('ffn_w_down'),
                nxt['ffn_w_up'], nxt['ffn_w_down'], n)
            return y

        if update_ctx:
            L = T - n
            mix_c = jnp.concatenate([
                mla_attend_ctx(q_nope[n:], q_rope[n:], k_nope[n:], k_rope[n:, 0], v_mla[n:]).reshape(L, MLA_HEADS * MLA_V),
                pool_mixer(pool_p[n:], ws['pool_w'][l], ws['pool_scale'][l]),
                ctx_attention(sq[n:], sk[n:], sv[n:], ws['swa_sink'][l]),
                ctx_attention(nq_[n:], nk_[n:], nv_[n:], None),
            ], axis=-1)
            o_all = proj(jnp.concatenate([mix_x, mix_c], axis=0), 'w_out')
            x = x + gt_m * o_all[:n]
            ctx = ctx + cgt_m * o_all[n:]
            f_all = ffn(x, ctx)
            x = x + gt_f * f_all[:n]
            ctx = ctx + cgt_f * f_all[n:]
        else:
            x = x + gt_m * proj(mix_x, 'w_out')
            x = x + gt_f * ffn(x, None)
    return x


PACK_ROWS = 128


def _pack_rows(vecs):
    flat = jnp.concatenate([v.reshape(-1).astype(F32) for v in vecs])
    pad = (-flat.shape[0]) % (PACK_ROWS * 128)
    return jnp.pad(flat, (0, pad)).reshape(-1, 128)


def _unpack(flat, shapes):
    out, off = [], 0
    for s in shapes:
        size = 1
        for d in s:
            size *= d
        out.append(flat[off:off + size].reshape(s))
        off += size
    return out


def _silu_grad(z):
    s = jax.nn.sigmoid(z)
    return s * (1.0 + z * (1.0 - s))


def kernel(x, c, ctx, c_ctx, w_mod, b_mod, g_mix, g_ffn, w_in, w_out, mla_q_a_norm, mla_w_qb, mla_kv_a_norm, mla_w_kvb, mla_q_nope_norm, mla_q_rope_norm, mla_k_nope_norm, mla_k_rope_norm, pool_w, pool_scale, swa_q_norm, swa_k_norm, swa_sink, na_q_norm, na_k_norm, na_rpb, ffn_w_up, ffn_conv_w, ffn_conv_b, ffn_w_down, loss_target, m_c_ctx, m_w_mod, m_b_mod, m_g_mix, m_g_ffn, m_w_in, m_w_out, m_mla_q_a_norm, m_mla_w_qb, m_mla_kv_a_norm, m_mla_w_kvb, m_mla_q_nope_norm, m_mla_q_rope_norm, m_mla_k_nope_norm, m_mla_k_rope_norm, m_pool_w, m_pool_scale, m_swa_q_norm, m_swa_k_norm, m_swa_sink, m_na_q_norm, m_na_k_norm, m_na_rpb, m_ffn_w_up, m_ffn_conv_w, m_ffn_conv_b, m_ffn_w_down, v_c_ctx, v_w_mod, v_b_mod, v_g_mix, v_g_ffn, v_w_in, v_w_out, v_mla_q_a_norm, v_mla_w_qb, v_mla_kv_a_norm, v_mla_w_kvb, v_mla_q_nope_norm, v_mla_q_rope_norm, v_mla_k_nope_norm, v_mla_k_rope_norm, v_pool_w, v_pool_scale, v_swa_q_norm, v_swa_k_norm, v_swa_sink, v_na_q_norm, v_na_k_norm, v_na_rpb, v_ffn_w_up, v_ffn_conv_w, v_ffn_conv_b, v_ffn_w_down):
    return _step(x, c, ctx, c_ctx, w_mod, b_mod, g_mix, g_ffn, w_in, w_out, mla_q_a_norm, mla_w_qb, mla_kv_a_norm, mla_w_kvb, mla_q_nope_norm, mla_q_rope_norm, mla_k_nope_norm, mla_k_rope_norm, pool_w, pool_scale, swa_q_norm, swa_k_norm, swa_sink, na_q_norm, na_k_norm, na_rpb, ffn_w_up, ffn_conv_w, ffn_conv_b, ffn_w_down, loss_target, m_c_ctx, m_w_mod, m_b_mod, m_g_mix, m_g_ffn, m_w_in, m_w_out, m_mla_q_a_norm, m_mla_w_qb, m_mla_kv_a_norm, m_mla_w_kvb, m_mla_q_nope_norm, m_mla_q_rope_norm, m_mla_k_nope_norm, m_mla_k_rope_norm, m_pool_w, m_pool_scale, m_swa_q_norm, m_swa_k_norm, m_swa_sink, m_na_q_norm, m_na_k_norm, m_na_rpb, m_ffn_w_up, m_ffn_conv_w, m_ffn_conv_b, m_ffn_w_down, v_c_ctx, v_w_mod, v_b_mod, v_g_mix, v_g_ffn, v_w_in, v_w_out, v_mla_q_a_norm, v_mla_w_qb, v_mla_kv_a_norm, v_mla_w_kvb, v_mla_q_nope_norm, v_mla_q_rope_norm, v_mla_k_nope_norm, v_mla_k_rope_norm, v_pool_w, v_pool_scale, v_swa_q_norm, v_swa_k_norm, v_swa_sink, v_na_q_norm, v_na_k_norm, v_na_rpb, v_ffn_w_up, v_ffn_conv_w, v_ffn_conv_b, v_ffn_w_down)


def _step(*args):
    n_in = len(ARG_NAMES)
    n_w = len(WEIGHTS)
    given = dict(zip(ARG_NAMES, args[:n_in]))
    mom = dict(zip(WEIGHTS, args[n_in:n_in + n_w]))
    var = dict(zip(WEIGHTS, args[n_in + n_w:n_in + 2 * n_w]))
    me = _my_index()

    x = given['x'][0]
    ctx = given['ctx'][0]
    target = given['loss_target'][0]
    n, D = x.shape
    depth = given['w_mod'].shape[0]
    mod_cols = given['w_mod'].shape[2]
    conv_cols = given['ffn_conv_w'].shape[2]

    late0 = tuple(LAYER0_CARRIERS.values())
    gathered0 = {name: _all_gather(given[name][0].astype(BF16), 'ag_' + name) for name in BIG if name not in late0}
    misc = _all_gather(_pack_rows([given['c'], given['ffn_conv_w']]), 'ag_cond')
    misc = misc.reshape(N_DEV, -1)
    c_all = misc[:, :D]
    conv_w = misc[:, D:D + depth * 3 * conv_cols].reshape(N_DEV, depth, 3, conv_cols)
    conv_w = conv_w.transpose(1, 2, 0, 3).reshape(depth, 3, N_DEV * conv_cols)

    cond = jnp.concatenate([c_all, given['c_ctx'][None], jnp.zeros((16 - N_DEV - 1, D), F32)], axis=0)
    s16 = jax.nn.silu(cond).astype(BF16)
    wm16 = given['w_mod'].astype(BF16)
    b_loc = lax.dynamic_slice_in_dim(given['b_mod'], me * mod_cols, mod_cols, axis=1)
    mod_part = jnp.stack([_mm(s16, wm16[l], 'nn', F32, 'mod_fwd') + b_loc[l] for l in range(depth)])
    mod_all = _all_gather(mod_part, 'ag_mod').transpose(1, 2, 0, 3).reshape(depth, 16, N_DEV * mod_cols)
    mod_x = lax.dynamic_index_in_dim(mod_all, me, axis=1, keepdims=False)
    mod_c = mod_all[:, N_DEV]

    ws = {name: given[name] for name in SMALL if name not in ('c_ctx', 'b_mod')}
    ws['ffn_conv_w'] = conv_w
    shards = {name: tuple(given[name][l] if l or name in late0 else None for l in range(depth)) for name in BIG}
    y, vjp = jax.vjp(lambda *d: _forward(*d, ctx), x, mod_x, mod_c, gathered0, shards, ws)
    loss_local, dy = _loss_head(y, target)
    g_x, g_mod_x, g_mod_c, g_gathered0, g_shards, g_ws = vjp(dy)

    g_big = {}
    for name in BIG:
        g0 = g_shards[name][0]
        if name not in late0:
            g0 = _sum_parts(_all_to_all(g_gathered0[name], 'a2a_' + name), 'sum_' + name)
        g_big[name] = jnp.stack((g0,) + tuple(g_shards[name][1:]))

    g_mod = jnp.zeros((depth, 16, N_DEV * mod_cols), F32)
    g_mod = lax.dynamic_update_slice_in_dim(g_mod, g_mod_x[:, None, :], me, axis=1)
    g_mod = g_mod.at[:, N_DEV].set(g_mod_c)
    g_mod_parts = g_mod.reshape(depth, 16, N_DEV, mod_cols).transpose(2, 0, 1, 3).reshape(N_DEV, -1)
    g_conv_parts = g_ws['ffn_conv_w'].reshape(depth, 3, N_DEV, conv_cols).transpose(2, 0, 1, 3).reshape(N_DEV, -1)
    n_mod = depth * 16 * mod_cols
    n_conv = depth * 3 * conv_cols
    f32_parts = jnp.concatenate([g_mod_parts, g_conv_parts], axis=1)
    f32_pad = (-f32_parts.shape[1]) % (PACK_ROWS * 128)
    f32_parts = jnp.pad(f32_parts, ((0, 0), (0, f32_pad))).reshape(N_DEV, -1, 128)
    f32_parts = _all_to_all(f32_parts, 'a2a_f32')
    f32_sum = _sum_parts(f32_parts, 'sum_f32').reshape(-1)
    g_mod_loc = f32_sum[:n_mod].reshape(depth, 16, mod_cols)
    g_conv_loc = f32_sum[n_mod:n_mod + n_conv].reshape(depth * 3, conv_cols)

    g_mod16 = g_mod_loc.astype(BF16)
    g_w_mod = jnp.stack([_mm(s16, g_mod16[l], 'tn', F32, 'mod_dw') for l in range(depth)])
    d_silu = sum(_mm(g_mod16[l], wm16[l], 'nt', F32, 'mod_dc') for l in range(depth))
    g_c_ctx_part = d_silu[N_DEV] * _silu_grad(given['c_ctx'])

    small_grads = {name: g_ws[name] for name in SMALL if name not in ('c_ctx', 'b_mod')}
    small_grads['c_ctx'] = g_c_ctx_part
    small_grads['b_mod'] = g_mod_x + g_mod_c
    small_shapes = [given[name].shape for name in SMALL]
    n_small = sum(int(given[name].size) for name in SMALL)
    packed = _pack_rows([small_grads[name] for name in SMALL] + [loss_local.reshape(1)])
    small_parts = _all_gather(packed, 'ag_small')

    out_g, out_d, out_m, out_v = {}, {}, {}, {}

    def update(name, parts):
        shape = given[name].shape
        flat = (shape[0] * shape[1], shape[2])
        res = _adamw(parts.reshape((-1,) + flat), given[name].reshape(flat), mom[name].reshape(flat),
                     var[name].reshape(flat), 'adamw_' + name)
        out_g[name], out_d[name], out_m[name], out_v[name] = (r.reshape(shape) for r in res)

    for name in BIG:
        update(name, g_big[name])
    update('w_mod', g_w_mod)
    update('ffn_conv_w', g_conv_loc)

    zero1 = jnp.zeros((1,), F32)
    res = _adamw(small_parts, _pack_rows([given[k] for k in SMALL] + [zero1]), _pack_rows([mom[k] for k in SMALL] + [zero1]),
                 _pack_rows([var[k] for k in SMALL] + [zero1 + 1.0]), 'adamw_small')
    flats = [r.reshape(-1) for r in res]
    for name, g_, d_, m_, v_ in zip(SMALL, *[_unpack(f, small_shapes) for f in flats]):
        out_g[name], out_d[name], out_m[name], out_v[name] = g_, d_, m_, v_
    loss = flats[0][n_small]

    return (loss, g_x[None], *[out_g[k] for k in WEIGHTS], *[out_d[k] for k in WEIGHTS],
            *[out_m[k] for k in WEIGHTS], *[out_v[k] for k in WEIGHTS])
```

```python
import functools

import jax
import jax.numpy as jnp
from jax import lax
from jax.experimental import pallas as pl
from jax.experimental.pallas import tpu as pltpu

F32 = jnp.float32
BF16 = jnp.bfloat16
N_DEV = 8
MESH_ID = pl.DeviceIdType.MESH

GRID_W = 64
ROPE_BASE = 10000.0
NORM_EPS = 1e-6
NEG_INF = -1e30

MLA_HEADS = 4
MLA_NOPE = 128
MLA_ROPE = 64
MLA_V = 128
MLA_Q_LORA = 512
MLA_KV_LORA = 256
MLA_SCALE = (MLA_NOPE + MLA_ROPE) ** -0.5
POOL_WINDOWS = (2, 4, 8, 16)
POOL_GROUP = 128
POOL_WIDTH = POOL_GROUP * len(POOL_WINDOWS)
SWA_HEADS = 8
SWA_KV_HEADS = 2
SWA_HEAD_DIM = 64
SWA_WINDOW = 128
SWA_BLOCK = 128
NA_HEADS = 8
NA_HEAD_DIM = 64
NA_KH = 8
NA_KW = 16
NA_QC = 16
NA_KC = NA_QC + NA_KW

A_COLS = MLA_Q_LORA + MLA_KV_LORA + MLA_ROPE
B_COLS = POOL_WIDTH
C_COLS = (SWA_HEADS + 2 * SWA_KV_HEADS) * SWA_HEAD_DIM
D_COLS = 3 * NA_HEADS * NA_HEAD_DIM
IN_COLS = A_COLS + B_COLS + C_COLS + D_COLS
IN_COLS_PAD = 3840
IN_SPLITS = (A_COLS, A_COLS + B_COLS, A_COLS + B_COLS + C_COLS)

ADAM_LR = 0.001
ADAM_B1 = 0.9
ADAM_B2 = 0.999
ADAM_EPS = 1e-08
ADAM_WD = 0.01
ADAM_STEP = 10

VMEM_LIMIT = 48 << 20

ARG_NAMES = ['x', 'c', 'ctx', 'c_ctx', 'w_mod', 'b_mod', 'g_mix', 'g_ffn', 'w_in', 'w_out', 'mla_q_a_norm', 'mla_w_qb', 'mla_kv_a_norm', 'mla_w_kvb', 'mla_q_nope_norm', 'mla_q_rope_norm', 'mla_k_nope_norm', 'mla_k_rope_norm', 'pool_w', 'pool_scale', 'swa_q_norm', 'swa_k_norm', 'swa_sink', 'na_q_norm', 'na_k_norm', 'na_rpb', 'ffn_w_up', 'ffn_conv_w', 'ffn_conv_b', 'ffn_w_down', 'loss_target']
WEIGHTS = ['c_ctx', 'w_mod', 'b_mod', 'g_mix', 'g_ffn', 'w_in', 'w_out', 'mla_q_a_norm', 'mla_w_qb', 'mla_kv_a_norm', 'mla_w_kvb', 'mla_q_nope_norm', 'mla_q_rope_norm', 'mla_k_nope_norm', 'mla_k_rope_norm', 'pool_w', 'pool_scale', 'swa_q_norm', 'swa_k_norm', 'swa_sink', 'na_q_norm', 'na_k_norm', 'na_rpb', 'ffn_w_up', 'ffn_conv_w', 'ffn_conv_b', 'ffn_w_down']
BIG = {'w_in': 2, 'w_out': 1, 'mla_w_qb': 2, 'mla_w_kvb': 2, 'ffn_w_up': 2, 'ffn_w_down': 1}
SMALL = ['c_ctx', 'b_mod', 'g_mix', 'g_ffn', 'mla_q_a_norm', 'mla_kv_a_norm', 'mla_q_nope_norm', 'mla_q_rope_norm', 'mla_k_nope_norm', 'mla_k_rope_norm', 'pool_w', 'pool_scale', 'swa_q_norm', 'swa_k_norm', 'swa_sink', 'na_q_norm', 'na_k_norm', 'na_rpb', 'ffn_conv_b']


def _pcall(body, **kw):
    return pl.pallas_call(body, **kw)


def _my_index():
    return 4 * lax.axis_index("x") + 2 * lax.axis_index("y") + lax.axis_index("c")


_COMM_SCRATCH = [pltpu.SemaphoreType.DMA((7,)), pltpu.SemaphoreType.DMA((7,)), pltpu.SemaphoreType.DMA(())]
_ANY = pl.BlockSpec(memory_space=pl.ANY)


def _gather_copies(x_ref, out_ref, send_sems, recv_sems, local_sem):
    x, y, c = lax.axis_index("x"), lax.axis_index("y"), lax.axis_index("c")
    me, sibling = (x, y, c), (x, y, 1 - c)
    chips = [(1 - x, y), (x, 1 - y), (1 - x, 1 - y)]

    def slot(px, py, pc):
        return out_ref.at[4 * px + 2 * py + pc]

    def copy(k, blk, to, src=None):
        return pltpu.make_async_remote_copy(
            src_ref=slot(*blk) if src is None else src, dst_ref=slot(*blk),
            send_sem=send_sems.at[k], recv_sem=recv_sems.at[k], device_id=to, device_id_type=MESH_ID)

    mine = pltpu.make_async_copy(x_ref, slot(*me), local_sem)
    first = [copy(0, me, sibling, src=x_ref)] + [copy(1 + j, me, (*chip, c), src=x_ref) for j, chip in enumerate(chips)]

    def start():
        mine.start()
        for cp in first:
            cp.start()

    def finish():
        passed = [copy(4 + j, (*chip, c), sibling) for j, chip in enumerate(chips)]
        for j, chip in enumerate(chips):
            copy(1 + j, (*chip, c), me).wait_recv()
            passed[j].start()
        copy(0, sibling, me).wait_recv()
        for j, chip in enumerate(chips):
            copy(4 + j, (*chip, 1 - c), me).wait_recv()
        for cp in first + passed:
            cp.wait_send()
        mine.wait()

    return start, finish


def _exchange_copies(t_ref, out_ref, send_sems, recv_sems, local_sem):
    x, y, c = lax.axis_index("x"), lax.axis_index("y"), lax.axis_index("c")
    me = 4 * x + 2 * y + c

    def peer(k):
        return (1 - x if k & 4 else x), (1 - y if k & 2 else y), (1 - c if k & 1 else c)

    def copy(k, landed):
        px, py, pc = peer(k)
        p = 4 * px + 2 * py + pc
        return pltpu.make_async_remote_copy(
            src_ref=t_ref.at[p], dst_ref=out_ref.at[p if landed else me],
            send_sem=send_sems.at[k - 1], recv_sem=recv_sems.at[k - 1], device_id=(px, py, pc), device_id_type=MESH_ID)

    mine = pltpu.make_async_copy(t_ref.at[me], out_ref.at[me], local_sem)
    sends = [copy(k, False) for k in range(1, N_DEV)]

    def start():
        mine.start()
        for cp in sends:
            cp.start()

    def finish():
        for k in range(1, N_DEV):
            copy(k, True).wait_recv()
        for cp in sends:
            cp.wait_send()
        mine.wait()

    return start, finish


_COMM = {'gather': _gather_copies, 'exchange': _exchange_copies}


def _comm_out_shape(kind, operand):
    shape = (N_DEV,) + operand.shape if kind == 'gather' else operand.shape
    return jax.ShapeDtypeStruct(shape, operand.dtype)


def _comm_call(kind, operand, name):
    def body(x_ref, out_ref, send_sems, recv_sems, local_sem):
        start, finish = _COMM[kind](x_ref, out_ref, send_sems, recv_sems, local_sem)
        start()
        finish()

    return _pcall(body, name=name, out_shape=_comm_out_shape(kind, operand), in_specs=[_ANY], out_specs=_ANY,
                  scratch_shapes=_COMM_SCRATCH)(operand)


def _all_gather(block, name):
    return _comm_call('gather', block, name)


def _all_to_all(parts, name):
    return _comm_call('exchange', parts, name)


def _carry_call(body, comm, name, grid, in_specs, out_specs, out_shape, scratch_shapes, semantics, operands):
    if comm is None:
        outs = _pcall(
            body, name=name, grid=grid, in_specs=in_specs, out_specs=out_specs, out_shape=out_shape,
            scratch_shapes=scratch_shapes,
            compiler_params=pltpu.CompilerParams(dimension_semantics=semantics, vmem_limit_bytes=VMEM_LIMIT),
        )(*operands)
        return outs, None

    kind, operand = comm
    n_in, n_out = len(in_specs), len(out_specs)
    steps = 1
    for g in grid:
        steps *= g

    def carrying(*refs):
        ins, x_ref = refs[:n_in], refs[n_in]
        outs, out_ref = refs[n_in + 1:n_in + 1 + n_out], refs[n_in + 1 + n_out]
        scratch, sems = refs[n_in + 2 + n_out:len(refs) - 3], refs[len(refs) - 3:]
        start, finish = _COMM[kind](x_ref, out_ref, *sems)
        step = pl.program_id(0)
        for axis in range(1, len(grid)):
            step = step * grid[axis] + pl.program_id(axis)

        @pl.when(step == 0)
        def _():
            start()

        body(*ins, *outs, *scratch)

        @pl.when(step == steps - 1)
        def _():
            finish()

    outs = _pcall(
        carrying, name=name, grid=grid, in_specs=list(in_specs) + [_ANY], out_specs=list(out_specs) + [_ANY],
        out_shape=list(out_shape) + [_comm_out_shape(kind, operand)], scratch_shapes=list(scratch_shapes) + _COMM_SCRATCH,
        compiler_params=pltpu.CompilerParams(dimension_semantics=("arbitrary",) * len(grid), vmem_limit_bytes=VMEM_LIMIT),
    )(*operands, operand)
    return outs[:-1], outs[-1]


_LANE_TILES = (1024, 768, 512, 384, 256, 128)
_ROW_TILES = (1088, 1024, 512, 256, 128)
_DEPTH_TILES = (2048, 1408) + _LANE_TILES
_TOKEN_DEPTH_TILES = (2176,) + _ROW_TILES


def _pick(dim, cands):
    for t in cands:
        if dim % t == 0:
            return t
    return dim


def _mm(a, b, mode, out_dtype, name, comm=None):
    if mode == 'nn':
        (M, K), (_, N) = a.shape, b.shape
        tm, tn, tk = _pick(M, _ROW_TILES), _pick(N, _LANE_TILES), _pick(K, _DEPTH_TILES)
        a_spec = pl.BlockSpec((tm, tk), lambda i, j, k: (i, k))
        b_spec = pl.BlockSpec((tk, tn), lambda i, j, k: (k, j))
        dn = (((1,), (0,)), ((), ()))
    elif mode == 'nt':
        (M, K), (N, _) = a.shape, b.shape
        tm, tn, tk = _pick(M, _ROW_TILES), _pick(N, _LANE_TILES), _pick(K, _DEPTH_TILES)
        a_spec = pl.BlockSpec((tm, tk), lambda i, j, k: (i, k))
        b_spec = pl.BlockSpec((tn, tk), lambda i, j, k: (j, k))
        dn = (((1,), (1,)), ((), ()))
    else:
        (K, M), (_, N) = a.shape, b.shape
        tm, tn, tk = _pick(M, _LANE_TILES), _pick(N, _LANE_TILES), _pick(K, _TOKEN_DEPTH_TILES)
        a_spec = pl.BlockSpec((tk, tm), lambda i, j, k: (k, i))
        b_spec = pl.BlockSpec((tk, tn), lambda i, j, k: (k, j))
        dn = (((0,), (0,)), ((), ()))
    grid = (M // tm, N // tn, K // tk)

    def matmul_step(a_ref, b_ref, o_ref, acc):
        @pl.when(pl.program_id(2) == 0)
        def _():
            acc[...] = jnp.zeros_like(acc)

        acc[...] += lax.dot_general(a_ref[...], b_ref[...], dn, preferred_element_type=F32)

        @pl.when(pl.program_id(2) == grid[2] - 1)
        def _():
            o_ref[...] = acc[...].astype(o_ref.dtype)

    outs, moved = _carry_call(
        matmul_step, comm, name, grid, [a_spec, b_spec], [pl.BlockSpec((tm, tn), lambda i, j, k: (i, j))],
        [jax.ShapeDtypeStruct((M, N), out_dtype)], [pltpu.VMEM((tm, tn), F32)], ("parallel", "parallel", "arbitrary"),
        (a, b))
    return outs[0] if comm is None else (outs[0], moved)


def _proj_fwd(a16, w, next_shard, name):
    if next_shard is None:
        return _mm(a16, w, 'nn', F32, name + '_nn'), None
    return _mm(a16, w, 'nn', F32, name + '_nn_gather', comm=('gather', next_shard.astype(BF16)))


def _proj_bwd(a16, w, dy, d_gathered, name):
    dy16 = dy.astype(BF16)
    if d_gathered is None:
        return _mm(dy16, w, 'nt', F32, name + '_nt'), _mm(a16, dy16, 'tn', BF16, name + '_tn'), None
    half = d_gathered.shape[1] // 2
    dw, top = _mm(a16, dy16, 'tn', BF16, name + '_tn_exchange', comm=('exchange', d_gathered[:, :half]))
    da, low = _mm(dy16, w, 'nt', F32, name + '_nt_exchange', comm=('exchange', d_gathered[:, half:]))
    d_shard = jnp.concatenate([_sum_parts(top, name + '_sum_top'), _sum_parts(low, name + '_sum_low')], axis=0)
    return da, dw, d_shard


@functools.partial(jax.custom_vjp, nondiff_argnums=(3,))
def pmm(a, w, next_shard, name):
    return _proj_fwd(a.astype(BF16), w, next_shard, name)


def _pmm_fwd(a, w, next_shard, name):
    a16 = a.astype(BF16)
    return _proj_fwd(a16, w, next_shard, name), (a16, w)


def _pmm_bwd(name, res, cts):
    return _proj_bwd(*res, cts[0], cts[1], name)


pmm.defvjp(_pmm_fwd, _pmm_bwd)


def _normmod_rows(x, ctx):
    n, d = x.shape
    rows = (n,) if ctx is None else (n, ctx.shape[0])
    tr = next(t for t in (256, 128, 64, 32, 16, 8) if all(r % t == 0 for r in rows))
    nx = n // tr
    specs = [pl.BlockSpec((tr, d), lambda i: (jnp.minimum(i, nx - 1), 0))]
    if ctx is not None:
        specs.append(pl.BlockSpec((tr, d), lambda i: (jnp.maximum(i - nx, 0), 0)))
    return tr, nx, sum(rows) // tr, specs


def _normmod_terms(refs, has_ctx, nx):
    is_x = pl.program_id(0) < nx
    vec_ref = refs[2] if has_ctx else refs[1]
    t = jnp.where(is_x, refs[0][...], refs[1][...]) if has_ctx else refs[0][...]
    shift = jnp.where(is_x, vec_ref[1:2, :], vec_ref[3:4, :])
    scale = jnp.where(is_x, vec_ref[2:3, :], vec_ref[4:5, :])
    r = lax.rsqrt(jnp.mean(t * t, axis=1, keepdims=True) + NORM_EPS)
    return is_x, t, r, vec_ref[0:1, :], shift, scale


def _normmod_fwd_call(x, ctx, vec):
    has_ctx = ctx is not None
    d = x.shape[1]
    tr, nx, blocks, specs = _normmod_rows(x, ctx)

    def body(*refs):
        _, t, r, g, shift, scale = _normmod_terms(refs, has_ctx, nx)
        refs[-1][...] = ((t * r * g) * (1.0 + scale) + shift).astype(BF16)

    operands = (x, ctx, vec) if has_ctx else (x, vec)
    return _pcall(
        body, name='normmod_fwd', grid=(blocks,), in_specs=specs + [pl.BlockSpec((8, d), lambda i: (0, 0))],
        out_specs=pl.BlockSpec((tr, d), lambda i: (i, 0)), out_shape=jax.ShapeDtypeStruct((blocks * tr, d), BF16),
        compiler_params=pltpu.CompilerParams(dimension_semantics=("parallel",), vmem_limit_bytes=VMEM_LIMIT),
    )(*operands)


def _normmod_bwd_call(x, ctx, vec, dh):
    has_ctx = ctx is not None
    d = x.shape[1]
    tr, nx, blocks, specs = _normmod_rows(x, ctx)
    n_in = len(specs) + 2

    def body(*refs):
        is_x, t, r, g, shift, scale = _normmod_terms(refs, has_ctx, nx)
        dh_ref, dx_ref, dvec_ref = refs[n_in - 1], refs[n_in], refs[-1]

        @pl.when(pl.program_id(0) == 0)
        def _():
            dvec_ref[...] = jnp.zeros_like(dvec_ref)

        xn = t * r
        dh = dh_ref[...]
        dy = dh * (1.0 + scale)
        dxn = dy * g
        dt = r * (dxn - xn * jnp.mean(dxn * xn, axis=1, keepdims=True))
        d_shift = jnp.sum(dh, axis=0, keepdims=True)
        d_scale = jnp.sum(dh * (xn * g), axis=0, keepdims=True)
        dvec_ref[0:1, :] += jnp.sum(dy * xn, axis=0, keepdims=True)

        @pl.when(is_x)
        def _():
            dx_ref[...] = dt
            dvec_ref[1:2, :] += d_shift
            dvec_ref[2:3, :] += d_scale

        if has_ctx:
            @pl.when(jnp.logical_not(is_x))
            def _():
                refs[n_in + 1][...] = dt
                dvec_ref[3:4, :] += d_shift
                dvec_ref[4:5, :] += d_scale

    vec_spec = pl.BlockSpec((8, d), lambda i: (0, 0))
    operands = (x, ctx, vec, dh) if has_ctx else (x, vec, dh)
    shapes = [jax.ShapeDtypeStruct(x.shape, F32)] + ([jax.ShapeDtypeStruct(ctx.shape, F32)] if has_ctx else [])
    outs = _pcall(
        body, name='normmod_bwd', grid=(blocks,),
        in_specs=specs + [vec_spec, pl.BlockSpec((tr, d), lambda i: (i, 0))],
        out_specs=specs + [vec_spec], out_shape=shapes + [jax.ShapeDtypeStruct((8, d), F32)],
        compiler_params=pltpu.CompilerParams(dimension_semantics=("arbitrary",), vmem_limit_bytes=VMEM_LIMIT),
    )(*operands)
    return (outs[0], outs[1], outs[2]) if has_ctx else (outs[0], None, outs[1])


@functools.partial(jax.custom_vjp, nondiff_argnums=(5,))
def norm_proj(x, ctx, vec, w, next_shard, name):
    return _norm_proj_fwd(x, ctx, vec, w, next_shard, name)[0]


def _norm_proj_fwd(x, ctx, vec, w, next_shard, name):
    h16 = _normmod_fwd_call(x, ctx, vec)
    return _proj_fwd(h16, w, next_shard, name), (x, ctx, vec, h16, w)


def _norm_proj_bwd(name, res, cts):
    x, ctx, vec, h16, w = res
    dh, dw, d_shard = _proj_bwd(h16, w, cts[0], cts[1], name)
    dx, dctx, dvec = _normmod_bwd_call(x, ctx, vec, dh)
    return dx, dctx, dvec, dw, d_shard


norm_proj.defvjp(_norm_proj_fwd, _norm_proj_bwd)


FFN_TILE = 128


def _neighbours(x, n):
    T = x.shape[0]
    t = lax.broadcasted_iota(jnp.int32, x.shape, 0)
    prev = jnp.where((t == 0) | (t == n), 0.0, pltpu.roll(x, 1, 0))
    nxt = jnp.where((t == n - 1) | (t == T - 1), 0.0, pltpu.roll(x, T - 1, 0))
    return prev, nxt


def _gate_specs(rows, f):
    tiles = f // FFN_TILE
    return [pl.BlockSpec((rows, FFN_TILE), lambda j: (0, j)), pl.BlockSpec((rows, FFN_TILE), lambda j: (0, j + tiles))]


def _conv3(x, cw_ref, cb_ref, n):
    prev, nxt = _neighbours(x, n)
    return prev * cw_ref[0:1, :] + x * cw_ref[1:2, :] + nxt * cw_ref[2:3, :] + cb_ref[...], prev, nxt


def _gate_fwd_call(a, cw, cb, n):
    T, f = a.shape[0], a.shape[1] // 2

    def body(ag_ref, av_ref, wg_ref, wv_ref, bg_ref, bv_ref, u_ref):
        g = _conv3(ag_ref[...], wg_ref, bg_ref, n)[0]
        v = _conv3(av_ref[...], wv_ref, bv_ref, n)[0]
        u_ref[...] = (g * jax.nn.sigmoid(g) * v).astype(u_ref.dtype)

    return _pcall(
        body, name='ffn_gate_fwd', grid=(f // FFN_TILE,),
        in_specs=_gate_specs(T, f) + _gate_specs(3, f) + _gate_specs(1, f),
        out_specs=pl.BlockSpec((T, FFN_TILE), lambda j: (0, j)),
        out_shape=jax.ShapeDtypeStruct((T, f), BF16),
        compiler_params=pltpu.CompilerParams(dimension_semantics=("parallel",), vmem_limit_bytes=VMEM_LIMIT),
    )(a, a, cw, cw, cb, cb)


def _gate_bwd_call(a, cw, cb, du, n):
    T, f = a.shape[0], a.shape[1] // 2

    def half(dz, x, prev, nxt, w_ref, da_ref, dw_ref):
        t = lax.broadcasted_iota(jnp.int32, dz.shape, 0)
        from_next = pltpu.roll(jnp.where((t == 0) | (t == n), 0.0, dz), T - 1, 0)
        from_prev = pltpu.roll(jnp.where((t == n - 1) | (t == T - 1), 0.0, dz), 1, 0)
        da_ref[...] = (dz * w_ref[1:2, :] + from_next * w_ref[0:1, :] + from_prev * w_ref[2:3, :]).astype(da_ref.dtype)
        dw_ref[0:1, :] = jnp.sum(dz * prev, axis=0, keepdims=True)
        dw_ref[1:2, :] = jnp.sum(dz * x, axis=0, keepdims=True)
        dw_ref[2:3, :] = jnp.sum(dz * nxt, axis=0, keepdims=True)
        dw_ref[3:4, :] = jnp.sum(dz, axis=0, keepdims=True)
        dw_ref[4:8, :] = jnp.zeros((4, FFN_TILE), F32)

    def body(ag_ref, av_ref, wg_ref, wv_ref, bg_ref, bv_ref, du_ref, dag_ref, dav_ref, dwg_ref, dwv_ref):
        xg, xv = ag_ref[...], av_ref[...]
        g, g_prev, g_next = _conv3(xg, wg_ref, bg_ref, n)
        v, v_prev, v_next = _conv3(xv, wv_ref, bv_ref, n)
        sg = jax.nn.sigmoid(g)
        du = du_ref[...]
        half(du * v * (sg * (1.0 + g * (1.0 - sg))), xg, g_prev, g_next, wg_ref, dag_ref, dwg_ref)
        half(du * (g * sg), xv, v_prev, v_next, wv_ref, dav_ref, dwv_ref)

    tile = lambda rows: pl.BlockSpec((rows, FFN_TILE), lambda j: (0, j))
    return _pcall(
        body, name='ffn_gate_bwd', grid=(f // FFN_TILE,),
        in_specs=_gate_specs(T, f) + _gate_specs(3, f) + _gate_specs(1, f) + [tile(T)],
        out_specs=[tile(T), tile(T), tile(8), tile(8)],
        out_shape=[jax.ShapeDtypeStruct((T, f), BF16), jax.ShapeDtypeStruct((T, f), BF16),
                   jax.ShapeDtypeStruct((8, f), F32), jax.ShapeDtypeStruct((8, f), F32)],
        compiler_params=pltpu.CompilerParams(dimension_semantics=("parallel",), vmem_limit_bytes=VMEM_LIMIT),
    )(a, a, cw, cw, cb, cb, du)


@functools.partial(jax.custom_vjp, nondiff_argnums=(9,))
def conv_ffn(x, ctx, vec, w_up, cw, cb, w_down, next_up, next_down, n):
    return _conv_ffn_fwd(x, ctx, vec, w_up, cw, cb, w_down, next_up, next_down, n)[0]


def _conv_ffn_fwd(x, ctx, vec, w_up, cw, cb, w_down, next_up, next_down, n):
    h16 = _normmod_fwd_call(x, ctx, vec)
    a, g_up = _proj_fwd(h16, w_up, next_up, 'ffn_up')
    u16 = _gate_fwd_call(a, cw, cb.reshape(1, -1), n)
    y, g_down = _proj_fwd(u16, w_down, next_down, 'ffn_down')
    return (y, g_up, g_down), (x, ctx, vec, h16, w_up, cw, cb, w_down, a, u16)


def _conv_ffn_bwd(n, res, cts):
    x, ctx, vec, h16, w_up, cw, cb, w_down, a, u16 = res
    dy, d_g_up, d_g_down = cts
    du, dw_down, d_next_down = _proj_bwd(u16, w_down, dy, d_g_down, 'ffn_down')
    dag, dav, dwg, dwv = _gate_bwd_call(a, cw, cb.reshape(1, -1), du, n)
    dcw = jnp.concatenate([dwg, dwv], axis=1)
    dh, dw_up, d_next_up = _proj_bwd(h16, w_up, jnp.concatenate([dag, dav], axis=1), d_g_up, 'ffn_up')
    dx, dctx, dvec = _normmod_bwd_call(x, ctx, vec, dh)
    return dx, dctx, dvec, dw_up, dcw[:3], dcw[3], dw_down, d_next_up, d_next_down


conv_ffn.defvjp(_conv_ffn_fwd, _conv_ffn_bwd)


def _attn_fwd_call(q, k, v, scale, comm=None):
    H, nq, dq = q.shape
    nk, dv = v.shape[1], v.shape[2]
    tq = _pick(nq, (256, 128))

    def body(q_ref, k_ref, v_ref, o_ref, lse_ref):
        s = lax.dot_general(q_ref[0], k_ref[0], (((1,), (1,)), ((), ())), preferred_element_type=F32) * scale
        m = jnp.max(s, axis=1, keepdims=True)
        p = jnp.exp(s - m)
        l = jnp.sum(p, axis=1, keepdims=True)
        pn = (p * (1.0 / l)).astype(BF16)
        o_ref[...] = jnp.dot(pn, v_ref[0], preferred_element_type=F32)
        lse_ref[0] = m + jnp.log(l)

    return _carry_call(
        body, comm, 'mla_attn_fwd', (H, nq // tq),
        [pl.BlockSpec((1, tq, dq), lambda h, i: (h, i, 0)),
         pl.BlockSpec((1, nk, dq), lambda h, i: (h, 0, 0)),
         pl.BlockSpec((1, nk, dv), lambda h, i: (h, 0, 0))],
        [pl.BlockSpec((tq, dv), lambda h, i: (i, h)), pl.BlockSpec((1, tq, 1), lambda h, i: (h, i, 0))],
        [jax.ShapeDtypeStruct((nq, H * dv), F32), jax.ShapeDtypeStruct((H, nq, 1), F32)], [],
        ("parallel", "parallel"), (q, k, v))


def _attn_bwd_call(q, k, v, o, lse, do, scale, comm=None):
    H, nq, dq = q.shape
    nk, dv = v.shape[1], v.shape[2]
    tq = _pick(nq, (128,))

    def body(q_ref, k_ref, v_ref, o_ref, lse_ref, do_ref, dq_ref, dk_ref, dv_ref):
        @pl.when(pl.program_id(1) == 0)
        def _():
            dk_ref[...] = jnp.zeros_like(dk_ref)
            dv_ref[...] = jnp.zeros_like(dv_ref)

        q16, k16, v16 = q_ref[0], k_ref[0], v_ref[0]
        do = do_ref[...]
        do16 = do.astype(BF16)
        s = lax.dot_general(q16, k16, (((1,), (1,)), ((), ())), preferred_element_type=F32) * scale
        p = jnp.exp(s - lse_ref[0])
        dv_ref[0] += lax.dot_general(p.astype(BF16), do16, (((0,), (0,)), ((), ())), preferred_element_type=F32)
        dp = lax.dot_general(do16, v16, (((1,), (1,)), ((), ())), preferred_element_type=F32)
        delta = jnp.sum(do * o_ref[...], axis=1, keepdims=True)
        ds16 = (p * (dp - delta) * scale).astype(BF16)
        dq_ref[0] = jnp.dot(ds16, k16, preferred_element_type=F32)
        dk_ref[0] += lax.dot_general(ds16, q16, (((0,), (0,)), ((), ())), preferred_element_type=F32)

    return _carry_call(
        body, comm, 'mla_attn_bwd', (H, nq // tq),
        [pl.BlockSpec((1, tq, dq), lambda h, i: (h, i, 0)),
         pl.BlockSpec((1, nk, dq), lambda h, i: (h, 0, 0)),
         pl.BlockSpec((1, nk, dv), lambda h, i: (h, 0, 0)),
         pl.BlockSpec((tq, dv), lambda h, i: (i, h)),
         pl.BlockSpec((1, tq, 1), lambda h, i: (h, i, 0)),
         pl.BlockSpec((tq, dv), lambda h, i: (i, h))],
        [pl.BlockSpec((1, tq, dq), lambda h, i: (h, i, 0)),
         pl.BlockSpec((1, nk, dq), lambda h, i: (h, 0, 0)),
         pl.BlockSpec((1, nk, dv), lambda h, i: (h, 0, 0))],
        [jax.ShapeDtypeStruct((H, nq, dq), F32), jax.ShapeDtypeStruct((H, nk, dq), F32),
         jax.ShapeDtypeStruct((H, nk, dv), F32)], [], ("parallel", "arbitrary"), (q, k, v, o, lse, do))


def _gather_comm(shard):
    return None if shard is None else ('gather', shard.astype(BF16))


def _exchange_comm(d_gathered):
    return None if d_gathered is None else ('exchange', d_gathered)


def _reduce_moved(parts, name):
    return None if parts is None else _sum_parts(parts, name)


@functools.partial(jax.custom_vjp, nondiff_argnums=(4,))
def attention(q, k, v, shard, scale):
    (o, _), gathered = _attn_fwd_call(q.astype(BF16), k.astype(BF16), v.astype(BF16), scale, _gather_comm(shard))
    return o, gathered


def _attention_fwd(q, k, v, shard, scale):
    q16, k16, v16 = q.astype(BF16), k.astype(BF16), v.astype(BF16)
    (o, lse), gathered = _attn_fwd_call(q16, k16, v16, scale, _gather_comm(shard))
    return (o, gathered), (q16, k16, v16, o, lse)


def _attention_bwd(scale, res, cts):
    q16, k16, v16, o, lse = res
    grads, parts = _attn_bwd_call(q16, k16, v16, o, lse, cts[0], scale, _exchange_comm(cts[1]))
    return tuple(grads) + (_reduce_moved(parts, 'mla_attn_sum'),)


attention.defvjp(_attention_fwd, _attention_bwd)


def _win_geometry(kind, n):
    if kind == 'na':
        rows = n // GRID_W
        kh = min(NA_KH, rows)

        def start(i):
            return jnp.clip(i - kh // 2, 0, rows - kh) * GRID_W

        def bidx(i):
            return jnp.clip(i - kh // 2, 0, rows - kh) - i + (NA_KH - 1)

        return GRID_W, kh * GRID_W, start, bidx
    nb = n // SWA_BLOCK

    def start(i):
        return i * SWA_BLOCK

    def bidx(i):
        return jnp.where(i == 0, 0, jnp.where(i == nb - 1, 2, 1))

    return SWA_BLOCK, 3 * SWA_BLOCK, start, bidx


def _dot_nt(a, b):
    return lax.dot_general(a, b, (((1,), (1,)), ((), ())), preferred_element_type=F32)


def _dot_tn(a, b):
    return lax.dot_general(a, b, (((0,), (0,)), ((), ())), preferred_element_type=F32)


def _win_specs(q, k, kc, bias):
    hq, n, d = q.shape
    grp = hq // k.shape[0]
    hb = bias.shape[0]
    return [
        pl.BlockSpec((1, n, d), lambda h: (h, 0, 0)),
        pl.BlockSpec((1,) + k.shape[1:], lambda h: (h // grp, 0, 0)),
        pl.BlockSpec((1,) + k.shape[1:], lambda h: (h // grp, 0, 0)),
        pl.BlockSpec((1,) + kc.shape[1:], lambda h: (h // grp, 0, 0)),
        pl.BlockSpec((1,) + kc.shape[1:], lambda h: (h // grp, 0, 0)),
        pl.BlockSpec((1,) + bias.shape[1:], (lambda h: (h, 0, 0, 0)) if hb > 1 else (lambda h: (0, 0, 0, 0))),
        pl.BlockSpec(memory_space=pltpu.SMEM),
    ]


def _win_fwd_call(q, k, v, kc, vc, bias, sink, kind, comm=None):
    hq, n, d = q.shape
    scale = d ** -0.5
    qb, wk, start, bidx = _win_geometry(kind, n)
    has_sink = kind == 'swa'

    def body(q_ref, k_ref, v_ref, kc_ref, vc_ref, b_ref, sink_ref, o_ref, lse_ref):
        kc16, vc16 = kc_ref[0], vc_ref[0]
        snk = sink_ref[pl.program_id(0)]

        def step(i, carry):
            qs = pl.multiple_of(i * qb, qb)
            ks = pl.multiple_of(start(i), GRID_W)
            q16 = q_ref[0, pl.ds(qs, qb), :]
            s1 = _dot_nt(q16, k_ref[0, pl.ds(ks, wk), :]) * scale + b_ref[0, bidx(i)]
            s2 = _dot_nt(q16, kc16) * scale
            m = jnp.maximum(jnp.max(s1, axis=1, keepdims=True), jnp.max(s2, axis=1, keepdims=True))
            if has_sink:
                m = jnp.maximum(m, snk)
            p1 = jnp.exp(s1 - m)
            p2 = jnp.exp(s2 - m)
            l = jnp.sum(p1, axis=1, keepdims=True) + jnp.sum(p2, axis=1, keepdims=True)
            if has_sink:
                l = l + jnp.exp(snk - m)
            inv = 1.0 / l
            o = (jnp.dot((p1 * inv).astype(BF16), v_ref[0, pl.ds(ks, wk), :], preferred_element_type=F32)
                 + jnp.dot((p2 * inv).astype(BF16), vc16, preferred_element_type=F32))
            o_ref[0, pl.ds(qs, qb), :] = o
            lse_ref[0, pl.ds(qs, qb), :] = m + jnp.log(l)
            return carry

        lax.fori_loop(0, n // qb, step, 0, unroll=8)

    return _carry_call(
        body, comm, kind + '_attn_fwd', (hq,), _win_specs(q, k, kc, bias),
        [pl.BlockSpec((1, n, d), lambda h: (h, 0, 0)), pl.BlockSpec((1, n, 1), lambda h: (h, 0, 0))],
        [jax.ShapeDtypeStruct((hq, n, d), F32), jax.ShapeDtypeStruct((hq, n, 1), F32)], [], ("parallel",),
        (q, k, v, kc, vc, bias, sink))


def _win_bwd_call(q, k, v, kc, vc, bias, sink, o, lse, do, kind, comm=None):
    hq, n, d = q.shape
    scale = d ** -0.5
    qb, wk, start, bidx = _win_geometry(kind, n)
    has_sink = kind == 'swa'
    bias_grad = kind == 'na'

    def body(q_ref, k_ref, v_ref, kc_ref, vc_ref, b_ref, sink_ref, o_ref, lse_ref, do_ref,
             dq_ref, dk_ref, dv_ref, dkc_ref, dvc_ref, db_ref, dsink_ref):
        kc16, vc16 = kc_ref[0], vc_ref[0]
        snk = sink_ref[pl.program_id(0)]
        dk_ref[...] = jnp.zeros_like(dk_ref)
        dv_ref[...] = jnp.zeros_like(dv_ref)
        dkc_ref[...] = jnp.zeros_like(dkc_ref)
        dvc_ref[...] = jnp.zeros_like(dvc_ref)
        db_ref[...] = jnp.zeros_like(db_ref)

        def step(i, dsink):
            qs = pl.multiple_of(i * qb, qb)
            ks = pl.multiple_of(start(i), GRID_W)
            q16 = q_ref[0, pl.ds(qs, qb), :]
            k16 = k_ref[0, pl.ds(ks, wk), :]
            v16 = v_ref[0, pl.ds(ks, wk), :]
            lse = lse_ref[0, pl.ds(qs, qb), :]
            do = do_ref[0, pl.ds(qs, qb), :]
            do16 = do.astype(BF16)
            p1 = jnp.exp(_dot_nt(q16, k16) * scale + b_ref[0, bidx(i)] - lse)
            p2 = jnp.exp(_dot_nt(q16, kc16) * scale - lse)
            delta = jnp.sum(do * o_ref[0, pl.ds(qs, qb), :], axis=1, keepdims=True)
            ds1 = p1 * (_dot_nt(do16, v16) - delta)
            ds2 = p2 * (_dot_nt(do16, vc16) - delta)
            if bias_grad:
                db_ref[0, bidx(i)] += ds1
            ds1 = (ds1 * scale).astype(BF16)
            ds2 = (ds2 * scale).astype(BF16)
            dq_ref[0, pl.ds(qs, qb), :] = (jnp.dot(ds1, k16, preferred_element_type=F32)
                                          + jnp.dot(ds2, kc16, preferred_element_type=F32))
            dk_ref[0, pl.ds(ks, wk), :] += _dot_tn(ds1, q16)
            dv_ref[0, pl.ds(ks, wk), :] += _dot_tn(p1.astype(BF16), do16)
            dkc_ref[0] += _dot_tn(ds2, q16)
            dvc_ref[0] += _dot_tn(p2.astype(BF16), do16)
            if has_sink:
                dsink = dsink - jnp.sum(jnp.exp(snk - lse) * delta)
            return dsink

        dsink = lax.fori_loop(0, n // qb, step, jnp.zeros((), F32), unroll=4)
        dsink_ref[...] = jnp.full(dsink_ref.shape, dsink, F32)

    per_head = lambda shape: pl.BlockSpec((1,) + shape[1:], lambda h: (h,) + (0,) * (len(shape) - 1))
    kq = (hq,) + k.shape[1:]
    cq = (hq,) + kc.shape[1:]
    bq = (hq,) + bias.shape[1:]
    in_specs = _win_specs(q, k, kc, bias) + [per_head(o.shape), per_head(lse.shape), per_head(do.shape)]
    out_shapes = [q.shape, kq, kq, cq, cq, bq, (hq, 8, 128)]
    return _carry_call(
        body, comm, kind + '_attn_bwd', (hq,), in_specs, [per_head(s) for s in out_shapes],
        [jax.ShapeDtypeStruct(s, F32) for s in out_shapes], [], ("parallel",),
        (q, k, v, kc, vc, bias, sink, o, lse, do))


@functools.partial(jax.custom_vjp, nondiff_argnums=(8,))
def win_attention(q, k, v, kc, vc, bias, sink, shard, kind):
    b16 = lambda t: t.astype(BF16)
    (o, _), gathered = _win_fwd_call(b16(q), b16(k), b16(v), b16(kc), b16(vc), bias, sink, kind, _gather_comm(shard))
    return o, gathered


def _win_attention_fwd(q, k, v, kc, vc, bias, sink, shard, kind):
    res = tuple(t.astype(BF16) for t in (q, k, v, kc, vc)) + (bias, sink)
    (o, lse), gathered = _win_fwd_call(*res, kind, _gather_comm(shard))
    return (o, gathered), res + (o, lse)


def _win_attention_bwd(kind, res, cts):
    q, k, v, kc, vc, bias, sink, o, lse = res
    (dq, dk, dv, dkc, dvc, db, dsink), parts = _win_bwd_call(
        q, k, v, kc, vc, bias, sink, o, lse, cts[0], kind, _exchange_comm(cts[1]))
    hkv = k.shape[0]
    fold = lambda t: t.reshape((hkv, -1) + t.shape[1:]).sum(axis=1)
    if bias.shape[0] == 1:
        db = jnp.zeros_like(bias)
    return (dq, fold(dk), fold(dv), fold(dkc), fold(dvc), db, dsink[:, 0, 0], _reduce_moved(parts, kind + '_attn_sum'))


win_attention.defvjp(_win_attention_fwd, _win_attention_bwd)


def _loss_head(y, target):
    n, d = y.shape
    tr = _pick(n, (512, 256, 128))
    nb = n // tr

    def body(y_ref, t_ref, dy_ref, part_ref):
        err = y_ref[...] - t_ref[...]
        dy_ref[...] = err * (1.0 / d)
        part_ref[...] = jnp.full(part_ref.shape, jnp.sum(err * err), F32)

    dy, part = _pcall(
        body, name='loss_head', grid=(nb,),
        in_specs=[pl.BlockSpec((tr, d), lambda i: (i, 0)), pl.BlockSpec((tr, d), lambda i: (i, 0))],
        out_specs=[pl.BlockSpec((tr, d), lambda i: (i, 0)), pl.BlockSpec((1, 8, 128), lambda i: (i, 0, 0))],
        out_shape=[jax.ShapeDtypeStruct((n, d), F32), jax.ShapeDtypeStruct((nb, 8, 128), F32)],
        compiler_params=pltpu.CompilerParams(dimension_semantics=("parallel",), vmem_limit_bytes=VMEM_LIMIT),
    )(y, target)
    return 0.5 * jnp.sum(part[:, 0, 0]) / d, dy


def _sum_parts(parts, name):
    P, R, C = parts.shape
    tr = _pick(R, (256, 128, 64, 32, 16, 8))

    def body(p_ref, o_ref):
        acc = p_ref[0].astype(F32)
        for i in range(1, P):
            acc = acc + p_ref[i].astype(F32)
        o_ref[...] = acc

    return _pcall(
        body, name=name, grid=(R // tr,),
        in_specs=[pl.BlockSpec((P, tr, C), lambda i: (0, i, 0))],
        out_specs=pl.BlockSpec((tr, C), lambda i: (i, 0)),
        out_shape=jax.ShapeDtypeStruct((R, C), F32),
        compiler_params=pltpu.CompilerParams(dimension_semantics=("parallel",), vmem_limit_bytes=VMEM_LIMIT),
    )(parts)


def _adamw(parts, w, m, v, name):
    P, R, C = parts.shape
    tr = _pick(R, (128, 64, 32, 16, 8))
    c1 = 1.0 / (1.0 - ADAM_B1 ** ADAM_STEP)
    c2 = 1.0 / (1.0 - ADAM_B2 ** ADAM_STEP)

    def body(p_ref, w_ref, m_ref, v_ref, g_out, d_out, m_out, v_out):
        g = p_ref[0].astype(F32)
        for i in range(1, P):
            g = g + p_ref[i].astype(F32)
        m_new = ADAM_B1 * m_ref[...] + (1.0 - ADAM_B1) * g
        v_new = ADAM_B2 * v_ref[...] + (1.0 - ADAM_B2) * (g * g)
        g_out[...] = g
        m_out[...] = m_new
        v_out[...] = v_new
        d_out[...] = -ADAM_LR * ((m_new * c1) / (jnp.sqrt(v_new * c2) + ADAM_EPS) + ADAM_WD * w_ref[...])

    blk = pl.BlockSpec((tr, C), lambda i: (i, 0))
    return _pcall(
        body, name=name, grid=(R // tr,),
        in_specs=[pl.BlockSpec((P, tr, C), lambda i: (0, i, 0)), blk, blk, blk],
        out_specs=[blk, blk, blk, blk],
        out_shape=[jax.ShapeDtypeStruct((R, C), F32)] * 4,
        compiler_params=pltpu.CompilerParams(dimension_semantics=("parallel",), vmem_limit_bytes=VMEM_LIMIT),
    )(parts, w, m, v)


def rms_norm(x, g):
    return x * lax.rsqrt(jnp.mean(x * x, axis=-1, keepdims=True) + NORM_EPS) * g


def modulate(h, shift, scale):
    return h * (1.0 + scale) + shift


def axial_angles(n, d_rot):
    t = jnp.arange(n)
    row = (t // GRID_W).astype(F32)
    col = (t % GRID_W).astype(F32)
    d_axis = d_rot // 2
    inv_freq = ROPE_BASE ** (-jnp.arange(0, d_axis, 2, dtype=F32) / d_axis)
    return (row[:, None] * inv_freq, col[:, None] * inv_freq)


def rope_segment(x, ang):
    cos = jnp.cos(ang)[:, None, :]
    sin = jnp.sin(ang)[:, None, :]
    x1, x2 = jnp.split(x, 2, axis=-1)
    return jnp.concatenate([x1 * cos - x2 * sin, x2 * cos + x1 * sin], axis=-1)


def axial_rope(x, ang):
    half = x.shape[-1] // 2
    return jnp.concatenate([rope_segment(x[..., :half], ang[0]), rope_segment(x[..., half:], ang[1])], axis=-1)


def rope_latent(t, n, ang):
    return jnp.concatenate([axial_rope(t[:n], ang), t[n:]], axis=0)


def mla_attend_ctx(q_nope, q_rope, k_nope, k_rope, v):
    s = (jnp.einsum('qhd,khd->hqk', q_nope, k_nope, preferred_element_type=F32)
         + jnp.einsum('qhr,kr->hqk', q_rope, k_rope, preferred_element_type=F32))
    p = jax.nn.softmax(s * MLA_SCALE, axis=-1)
    return jnp.einsum('hqk,khd->qhd', p, v)


def pool_mixer(u, w_pool, scale):
    n = u.shape[0]
    t = jnp.arange(n)
    diffs = []
    for g, w in enumerate(POOL_WINDOWS):
        ug = u[:, g * POOL_GROUP:(g + 1) * POOL_GROUP]
        padded = jnp.pad(ug, ((w // 2, w // 2), (0, 0)))
        total = padded[0:n]
        for j in range(1, w):
            total = total + padded[j:j + n]
        count = (jnp.clip(t + w // 2, 0, n) - jnp.clip(t - w // 2, 0, n)).astype(F32)
        diffs.append(total / count[:, None] - ug)
    d = jnp.stack(diffs, axis=1)
    y = jnp.einsum('ngc,gcd->ngd', d, w_pool).reshape(n, POOL_WIDTH)
    return y * scale


def swa_latent(q, k, v, k_ctx, v_ctx, sink, shard):
    n, hq, d = q.shape
    blk = SWA_BLOCK
    a = jnp.arange(blk)[:, None]
    j = jnp.arange(3 * blk)[None, :]
    near = jnp.abs(j - blk - a) <= SWA_WINDOW
    tiles = jnp.stack([near & (j >= blk), near, near & (j < 2 * blk)])
    bias = jnp.where(tiles, 0.0, NEG_INF).astype(F32)[None]
    heads = lambda t: t.transpose(1, 0, 2)
    pad = lambda t: jnp.pad(heads(t), ((0, 0), (blk, blk), (0, 0)))
    o, gathered = win_attention(heads(q), pad(k), pad(v), heads(k_ctx), heads(v_ctx), bias, sink, shard, 'swa')
    return o.transpose(1, 0, 2).reshape(n, hq * d), gathered


def ctx_attention(q, k, v, sink):
    nq, hq, d = q.shape
    hkv = k.shape[1]
    grp = hq // hkv
    nk = k.shape[0]
    qg = q.reshape(nq, hkv, grp, d)
    s = jnp.einsum('qhgd,khd->hgqk', qg, k, preferred_element_type=F32) * (d ** -0.5)
    if sink is not None:
        s_sink = jnp.broadcast_to(sink.reshape(hkv, grp)[:, :, None, None], s.shape[:-1] + (1,))
        s = jnp.concatenate([s, s_sink], axis=-1)
    p = jax.nn.softmax(s, axis=-1)[..., :nk]
    o = jnp.einsum('hgqk,khd->qhgd', p, v)
    return o.reshape(nq, hq * d)


def na_bias_tiles(rpb, n):
    rows = n // GRID_W
    kh = min(NA_KH, rows)
    qc = jnp.arange(GRID_W)[:, None]
    kc = jnp.arange(GRID_W)[None, :]
    dc = jnp.clip(kc - qc, 1 - NA_KW, NA_KW - 1) + (NA_KW - 1)
    onehot = (dc[None] == jnp.arange(2 * NA_KW - 1)[:, None, None]).astype(F32)
    toeplitz = jnp.einsum('hdt,tqk->hdqk', rpb, onehot, precision=lax.Precision.HIGHEST)
    q_col0 = jnp.clip(qc - NA_KW // 2, 0, GRID_W - NA_KW)
    valid = (kc >= q_col0) & (kc < q_col0 + NA_KW)
    masked = jnp.where(valid, toeplitz, NEG_INF)
    return jnp.stack([jnp.concatenate([masked[:, off + j] for j in range(kh)], axis=-1) for off in range(NA_KH)], axis=1)


def na_latent(q, k, v, k_ctx, v_ctx, rpb, shard):
    n, h, d = q.shape
    heads = lambda t: t.transpose(1, 0, 2)
    o, gathered = win_attention(heads(q), heads(k), heads(v), heads(k_ctx), heads(v_ctx), na_bias_tiles(rpb, n),
                                jnp.zeros((h,), F32), shard, 'na')
    return o.transpose(1, 0, 2).reshape(n, h * d), gathered


def _assemble(g, name):
    _, r, c = g.shape
    w = g.reshape(N_DEV * r, c) if BIG[name] == 1 else g.transpose(1, 0, 2).reshape(r, N_DEV * c)
    if name == 'w_in':
        w = jnp.pad(w, ((0, 0), (0, IN_COLS_PAD - IN_COLS)))
    return w


LAYER0_CARRIERS = {'mla': 'ffn_w_up', 'na': 'ffn_w_down', 'swa': 'w_out'}


def _forward(x, mod_x, mod_c, gathered0, shards, ws, ctx):
    n = x.shape[0]
    depth = mod_x.shape[0]
    ang_mla = axial_angles(n, MLA_ROPE)
    ang_swa = axial_angles(n, SWA_HEAD_DIM)
    gathered = dict(gathered0)
    for l in range(depth):
        update_ctx = l < depth - 1
        cur, gathered = gathered, {}
        nxt = {name: shards[name][l + 1] if update_ctx else None for name in BIG}
        carried = {kind: shards[name][0] if l == 0 else None for kind, name in LAYER0_CARRIERS.items()}

        def weight(name):
            return _assemble(cur[name], name)

        def proj(a, name):
            y, gathered[name] = pmm(a, weight(name), nxt[name], name)
            return y

        sh_m, sc_m, gt_m, sh_f, sc_f, gt_f = jnp.split(mod_x[l], 6)
        csh_m, csc_m, cgt_m, csh_f, csc_f, cgt_f = jnp.split(mod_c[l], 6)

        def adaln_vec(g, *mods):
            return jnp.stack((g,) + mods + (jnp.zeros_like(g),) * (7 - len(mods)))

        p_all, gathered['w_in'] = norm_proj(x, ctx, adaln_vec(ws['g_mix'][l], sh_m, sc_m, csh_m, csc_m),
                                            weight('w_in'), nxt['w_in'], 'w_in')
        p_all = p_all[:, :IN_COLS]
        mla_p, pool_p, swa_p, na_p = jnp.split(p_all, IN_SPLITS, axis=-1)
        T = p_all.shape[0]

        cq, ckv, kr = jnp.split(mla_p, [MLA_Q_LORA, MLA_Q_LORA + MLA_KV_LORA], axis=-1)
        q = proj(rms_norm(cq, ws['mla_q_a_norm'][l]), 'mla_w_qb').reshape(T, MLA_HEADS, MLA_NOPE + MLA_ROPE)
        kv = proj(rms_norm(ckv, ws['mla_kv_a_norm'][l]), 'mla_w_kvb').reshape(T, MLA_HEADS, MLA_NOPE + MLA_V)
        q_nope = rms_norm(q[..., :MLA_NOPE], ws['mla_q_nope_norm'][l])
        q_rope = rope_latent(rms_norm(q[..., MLA_NOPE:], ws['mla_q_rope_norm'][l]), n, ang_mla)
        k_nope = rms_norm(kv[..., :MLA_NOPE], ws['mla_k_nope_norm'][l])
        v_mla = kv[..., MLA_NOPE:]
        k_rope = rope_latent(rms_norm(kr, ws['mla_k_rope_norm'][l])[:, None, :], n, ang_mla)
        q_cat = jnp.concatenate([q_nope, q_rope], axis=-1).transpose(1, 0, 2)
        k_cat = jnp.concatenate([k_nope, jnp.broadcast_to(k_rope, (T, MLA_HEADS, MLA_ROPE))], axis=-1).transpose(1, 0, 2)
        out_a, moved = attention(q_cat[:, :n], k_cat, v_mla.transpose(1, 0, 2), carried['mla'], MLA_SCALE)
        if l == 0:
            cur[LAYER0_CARRIERS['mla']] = moved

        out_b = pool_mixer(pool_p[:n], ws['pool_w'][l], ws['pool_scale'][l])

        sq, sk, sv = jnp.split(swa_p, [SWA_HEADS * SWA_HEAD_DIM, (SWA_HEADS + SWA_KV_HEADS) * SWA_HEAD_DIM], axis=-1)
        sq = rope_latent(rms_norm(sq.reshape(T, SWA_HEADS, SWA_HEAD_DIM), ws['swa_q_norm'][l]), n, ang_swa)
        sk = rope_latent(rms_norm(sk.reshape(T, SWA_KV_HEADS, SWA_HEAD_DIM), ws['swa_k_norm'][l]), n, ang_swa)
        sv = sv.reshape(T, SWA_KV_HEADS, SWA_HEAD_DIM)
        out_c, moved = swa_latent(sq[:n], sk[:n], sv[:n], sk[n:], sv[n:], ws['swa_sink'][l], carried['swa'])
        if l == 0:
            cur[LAYER0_CARRIERS['swa']] = moved

        nq_, nk_, nv_ = jnp.split(na_p, 3, axis=-1)
        nq_ = rms_norm(nq_.reshape(T, NA_HEADS, NA_HEAD_DIM), ws['na_q_norm'][l])
        nk_ = rms_norm(nk_.reshape(T, NA_HEADS, NA_HEAD_DIM), ws['na_k_norm'][l])
        nv_ = nv_.reshape(T, NA_HEADS, NA_HEAD_DIM)
        out_d, moved = na_latent(nq_[:n], nk_[:n], nv_[:n], nk_[n:], nv_[n:], ws['na_rpb'][l], carried['na'])
        if l == 0:
            cur[LAYER0_CARRIERS['na']] = moved

        mix_x = jnp.concatenate([out_a, out_b, out_c, out_d], axis=-1)

        def ffn(x_rows, ctx_rows):
            y, gathered['ffn_w_up'], gathered['ffn_w_down'] = conv_ffn(
                x_rows, ctx_rows, adaln_vec(ws['g_ffn'][l], sh_f, sc_f, csh_f, csc_f),
                weight('ffn_w_up'), ws['ffn_conv_w'][l], ws['ffn_conv_b'][l], weight('ffn_w_down'),
                nxt['ffn_w_up'], nxt['ffn_w_down'], n)
            return y

        if update_ctx:
            L = T - n
            mix_c = jnp.concatenate([
                mla_attend_ctx(q_nope[n:], q_rope[n:], k_nope[n:], k_rope[n:, 0], v_mla[n:]).reshape(L, MLA_HEADS * MLA_V),
                pool_mixer(pool_p[n:], ws['pool_w'][l], ws['pool_scale'][l]),
                ctx_attention(sq[n:], sk[n:], sv[n:], ws['swa_sink'][l]),
                ctx_attention(nq_[n:], nk_[n:], nv_[n:], None),
            ], axis=-1)
            o_all = proj(jnp.concatenate([mix_x, mix_c], axis=0), 'w_out')
            x = x + gt_m * o_all[:n]
            ctx = ctx + cgt_m * o_all[n:]
            f_all = ffn(x, ctx)
            x = x + gt_f * f_all[:n]
            ctx = ctx + cgt_f * f_all[n:]
        else:
            x = x + gt_m * proj(mix_x, 'w_out')
            x = x + gt_f * ffn(x, None)
    return x


PACK_ROWS = 128


def _pack_rows(vecs):
    flat = jnp.concatenate([v.reshape(-1).astype(F32) for v in vecs])
    pad = (-flat.shape[0]) % (PACK_ROWS * 128)
    return jnp.pad(flat, (0, pad)).reshape(-1, 128)


def _unpack(flat, shapes):
    out, off = [], 0
    for s in shapes:
        size = 1
        for d in s:
            size *= d
        out.append(flat[off:off + size].reshape(s))
        off += size
    return out


def _silu_grad(z):
    s = jax.nn.sigmoid(z)
    return s * (1.0 + z * (1.0 - s))


def kernel(x, c, ctx, c_ctx, w_mod, b_mod, g_mix, g_ffn, w_in, w_out, mla_q_a_norm, mla_w_qb, mla_kv_a_norm, mla_w_kvb, mla_q_nope_norm, mla_q_rope_norm, mla_k_nope_norm, mla_k_rope_norm, pool_w, pool_scale, swa_q_norm, swa_k_norm, swa_sink, na_q_norm, na_k_norm, na_rpb, ffn_w_up, ffn_conv_w, ffn_conv_b, ffn_w_down, loss_target, m_c_ctx, m_w_mod, m_b_mod, m_g_mix, m_g_ffn, m_w_in, m_w_out, m_mla_q_a_norm, m_mla_w_qb, m_mla_kv_a_norm, m_mla_w_kvb, m_mla_q_nope_norm, m_mla_q_rope_norm, m_mla_k_nope_norm, m_mla_k_rope_norm, m_pool_w, m_pool_scale, m_swa_q_norm, m_swa_k_norm, m_swa_sink, m_na_q_norm, m_na_k_norm, m_na_rpb, m_ffn_w_up, m_ffn_conv_w, m_ffn_conv_b, m_ffn_w_down, v_c_ctx, v_w_mod, v_b_mod, v_g_mix, v_g_ffn, v_w_in, v_w_out, v_mla_q_a_norm, v_mla_w_qb, v_mla_kv_a_norm, v_mla_w_kvb, v_mla_q_nope_norm, v_mla_q_rope_norm, v_mla_k_nope_norm, v_mla_k_rope_norm, v_pool_w, v_pool_scale, v_swa_q_norm, v_swa_k_norm, v_swa_sink, v_na_q_norm, v_na_k_norm, v_na_rpb, v_ffn_w_up, v_ffn_conv_w, v_ffn_conv_b, v_ffn_w_down):
    return _step(x, c, ctx, c_ctx, w_mod, b_mod, g_mix, g_ffn, w_in, w_out, mla_q_a_norm, mla_w_qb, mla_kv_a_norm, mla_w_kvb, mla_q_nope_norm, mla_q_rope_norm, mla_k_nope_norm, mla_k_rope_norm, pool_w, pool_scale, swa_q_norm, swa_k_norm, swa_sink, na_q_norm, na_k_norm, na_rpb, ffn_w_up, ffn_conv_w, ffn_conv_b, ffn_w_down, loss_target, m_c_ctx, m_w_mod, m_b_mod, m_g_mix, m_g_ffn, m_w_in, m_w_out, m_mla_q_a_norm, m_mla_w_qb, m_mla_kv_a_norm, m_mla_w_kvb, m_mla_q_nope_norm, m_mla_q_rope_norm, m_mla_k_nope_norm, m_mla_k_rope_norm, m_pool_w, m_pool_scale, m_swa_q_norm, m_swa_k_norm, m_swa_sink, m_na_q_norm, m_na_k_norm, m_na_rpb, m_ffn_w_up, m_ffn_conv_w, m_ffn_conv_b, m_ffn_w_down, v_c_ctx, v_w_mod, v_b_mod, v_g_mix, v_g_ffn, v_w_in, v_w_out, v_mla_q_a_norm, v_mla_w_qb, v_mla_kv_a_norm, v_mla_w_kvb, v_mla_q_nope_norm, v_mla_q_rope_norm, v_mla_k_nope_norm, v_mla_k_rope_norm, v_pool_w, v_pool_scale, v_swa_q_norm, v_swa_k_norm, v_swa_sink, v_na_q_norm, v_na_k_norm, v_na_rpb, v_ffn_w_up, v_ffn_conv_w, v_ffn_conv_b, v_ffn_w_down)


def _step(*args):
    n_in = len(ARG_NAMES)
    n_w = len(WEIGHTS)
    given = dict(zip(ARG_NAMES, args[:n_in]))
    mom = dict(zip(WEIGHTS, args[n_in:n_in + n_w]))
    var = dict(zip(WEIGHTS, args[n_in + n_w:n_in + 2 * n_w]))
    me = _my_index()

    x = given['x'][0]
    ctx = given['ctx'][0]
    target = given['loss_target'][0]
    n, D = x.shape
    depth = given['w_mod'].shape[0]
    mod_cols = given['w_mod'].shape[2]
    conv_cols = given['ffn_conv_w'].shape[2]

    late0 = tuple(LAYER0_CARRIERS.values())
    gathered0 = {name: _all_gather(given[name][0].astype(BF16), 'ag_' + name) for name in BIG if name not in late0}
    misc = _all_gather(_pack_rows([given['c'], given['ffn_conv_w']]), 'ag_cond')
    misc = misc.reshape(N_DEV, -1)
    c_all = misc[:, :D]
    conv_w = misc[:, D:D + depth * 3 * conv_cols].reshape(N_DEV, depth, 3, conv_cols)
    conv_w = conv_w.transpose(1, 2, 0, 3).reshape(depth, 3, N_DEV * conv_cols)

    cond = jnp.concatenate([c_all, given['c_ctx'][None], jnp.zeros((16 - N_DEV - 1, D), F32)], axis=0)
    s16 = jax.nn.silu(cond).astype(BF16)
    wm16 = given['w_mod'].astype(BF16)
    b_loc = lax.dynamic_slice_in_dim(given['b_mod'], me * mod_cols, mod_cols, axis=1)
    mod_part = jnp.stack([_mm(s16, wm16[l], 'nn', F32, 'mod_fwd') + b_loc[l] for l in range(depth)])
    mod_all = _all_gather(mod_part, 'ag_mod').transpose(1, 2, 0, 3).reshape(depth, 16, N_DEV * mod_cols)
    mod_x = lax.dynamic_index_in_dim(mod_all, me, axis=1, keepdims=False)
    mod_c = mod_all[:, N_DEV]

    ws = {name: given[name] for name in SMALL if name not in ('c_ctx', 'b_mod')}
    ws['ffn_conv_w'] = conv_w
    shards = {name: tuple(given[name][l] if l or name in late0 else None for l in range(depth)) for name in BIG}
    y, vjp = jax.vjp(lambda *d: _forward(*d, ctx), x, mod_x, mod_c, gathered0, shards, ws)
    loss_local, dy = _loss_head(y, target)
    g_x, g_mod_x, g_mod_c, g_gathered0, g_shards, g_ws = vjp(dy)

    g_big = {}
    for name in BIG:
        g0 = g_shards[name][0]
        if name not in late0:
            g0 = _sum_parts(_all_to_all(g_gathered0[name], 'a2a_' + name), 'sum_' + name)
        g_big[name] = jnp.stack((g0,) + tuple(g_shards[name][1:]))

    g_mod = jnp.zeros((depth, 16, N_DEV * mod_cols), F32)
    g_mod = lax.dynamic_update_slice_in_dim(g_mod, g_mod_x[:, None, :], me, axis=1)
    g_mod = g_mod.at[:, N_DEV].set(g_mod_c)
    g_mod_parts = g_mod.reshape(depth, 16, N_DEV, mod_cols).transpose(2, 0, 1, 3).reshape(N_DEV, -1)
    g_conv_parts = g_ws['ffn_conv_w'].reshape(depth, 3, N_DEV, conv_cols).transpose(2, 0, 1, 3).reshape(N_DEV, -1)
    n_mod = depth * 16 * mod_cols
    n_conv = depth * 3 * conv_cols
    f32_parts = jnp.concatenate([g_mod_parts, g_conv_parts], axis=1)
    f32_pad = (-f32_parts.shape[1]) % (PACK_ROWS * 128)
    f32_parts = jnp.pad(f32_parts, ((0, 0), (0, f32_pad))).reshape(N_DEV, -1, 128)
    f32_parts = _all_to_all(f32_parts, 'a2a_f32')
    f32_sum = _sum_parts(f32_parts, 'sum_f32').reshape(-1)
    g_mod_loc = f32_sum[:n_mod].reshape(depth, 16, mod_cols)
    g_conv_loc = f32_sum[n_mod:n_mod + n_conv].reshape(depth * 3, conv_cols)

    g_mod16 = g_mod_loc.astype(BF16)
    g_w_mod = jnp.stack([_mm(s16, g_mod16[l], 'tn', F32, 'mod_dw') for l in range(depth)])
    d_silu = sum(_mm(g_mod16[l], wm16[l], 'nt', F32, 'mod_dc') for l in range(depth))
    g_c_ctx_part = d_silu[N_DEV] * _silu_grad(given['c_ctx'])

    small_grads = {name: g_ws[name] for name in SMALL if name not in ('c_ctx', 'b_mod')}
    small_grads['c_ctx'] = g_c_ctx_part
    small_grads['b_mod'] = g_mod_x + g_mod_c
    small_shapes = [given[name].shape for name in SMALL]
    n_small = sum(int(given[name].size) for name in SMALL)
    packed = _pack_rows([small_grads[name] for name in SMALL] + [loss_local.reshape(1)])
    small_parts = _all_gather(packed, 'ag_small')

    out_g, out_d, out_m, out_v = {}, {}, {}, {}

    def update(name, parts):
        shape = given[name].shape
        flat = (shape[0] * shape[1], shape[2])
        res = _adamw(parts.reshape((-1,) + flat), given[name].reshape(flat), mom[name].reshape(flat),
                     var[name].reshape(flat), 'adamw_' + name)
        out_g[name], out_d[name], out_m[name], out_v[name] = (r.reshape(shape) for r in res)

    for name in BIG:
        update(name, g_big[name])
    update('w_mod', g_w_mod)
    update('ffn_conv_w', g_conv_loc)

    zero1 = jnp.zeros((1,), F32)
    res = _adamw(small_parts, _pack_rows([given[k] for k in SMALL] + [zero1]), _pack_rows([mom[k] for k in SMALL] + [zero1]),
                 _pack_rows([var[k] for k in SMALL] + [zero1 + 1.0]), 'adamw_small')
    flats = [r.reshape(-1) for r in res]
    for name, g_, d_, m_, v_ in zip(SMALL, *[_unpack(f, small_shapes) for f in flats]):
        out_g[name], out_d[name], out_m[name], out_v[name] = g_, d_, m_, v_
    loss = flats[0][n_small]

    return (loss, g_x[None], *[out_g[k] for k in WEIGHTS], *[out_d[k] for k in WEIGHTS],
            *[out_m[k] for k in WEIGHTS], *[out_v[k] for k in WEIGHTS])
```

```python
import functools

import jax
import jax.numpy as jnp
from jax import lax
from jax.experimental import pallas as pl
from jax.experimental.pallas import tpu as pltpu

F32 = jnp.float32
BF16 = jnp.bfloat16
N_DEV = 8
MESH_ID = pl.DeviceIdType.MESH

GRID_W = 64
ROPE_BASE = 10000.0
NORM_EPS = 1e-6
NEG_INF = -1e30

MLA_HEADS = 4
MLA_NOPE = 128
MLA_ROPE = 64
MLA_V = 128
MLA_Q_LORA = 512
MLA_KV_LORA = 256
MLA_SCALE = (MLA_NOPE + MLA_ROPE) ** -0.5
POOL_WINDOWS = (2, 4, 8, 16)
POOL_GROUP = 128
POOL_WIDTH = POOL_GROUP * len(POOL_WINDOWS)
SWA_HEADS = 8
SWA_KV_HEADS = 2
SWA_HEAD_DIM = 64
SWA_WINDOW = 128
SWA_BLOCK = 128
NA_HEADS = 8
NA_HEAD_DIM = 64
NA_KH = 8
NA_KW = 16
NA_QC = 16
NA_KC = NA_QC + NA_KW

A_COLS = MLA_Q_LORA + MLA_KV_LORA + MLA_ROPE
B_COLS = POOL_WIDTH
C_COLS = (SWA_HEADS + 2 * SWA_KV_HEADS) * SWA_HEAD_DIM
D_COLS = 3 * NA_HEADS * NA_HEAD_DIM
IN_COLS = A_COLS + B_COLS + C_COLS + D_COLS
IN_COLS_PAD = 3840
IN_SPLITS = (A_COLS, A_COLS + B_COLS, A_COLS + B_COLS + C_COLS)

ADAM_LR = 0.001
ADAM_B1 = 0.9
ADAM_B2 = 0.999
ADAM_EPS = 1e-08
ADAM_WD = 0.01
ADAM_STEP = 10

VMEM_LIMIT = 48 << 20

ARG_NAMES = ['x', 'c', 'ctx', 'c_ctx', 'w_mod', 'b_mod', 'g_mix', 'g_ffn', 'w_in', 'w_out', 'mla_q_a_norm', 'mla_w_qb', 'mla_kv_a_norm', 'mla_w_kvb', 'mla_q_nope_norm', 'mla_q_rope_norm', 'mla_k_nope_norm', 'mla_k_rope_norm', 'pool_w', 'pool_scale', 'swa_q_norm', 'swa_k_norm', 'swa_sink', 'na_q_norm', 'na_k_norm', 'na_rpb', 'ffn_w_up', 'ffn_conv_w', 'ffn_conv_b', 'ffn_w_down', 'loss_target']
WEIGHTS = ['c_ctx', 'w_mod', 'b_mod', 'g_mix', 'g_ffn', 'w_in', 'w_out', 'mla_q_a_norm', 'mla_w_qb', 'mla_kv_a_norm', 'mla_w_kvb', 'mla_q_nope_norm', 'mla_q_rope_norm', 'mla_k_nope_norm', 'mla_k_rope_norm', 'pool_w', 'pool_scale', 'swa_q_norm', 'swa_k_norm', 'swa_sink', 'na_q_norm', 'na_k_norm', 'na_rpb', 'ffn_w_up', 'ffn_conv_w', 'ffn_conv_b', 'ffn_w_down']
BIG = {'w_in': 2, 'w_out': 1, 'mla_w_qb': 2, 'mla_w_kvb': 2, 'ffn_w_up': 2, 'ffn_w_down': 1}
SMALL = ['c_ctx', 'b_mod', 'g_mix', 'g_ffn', 'mla_q_a_norm', 'mla_kv_a_norm', 'mla_q_nope_norm', 'mla_q_rope_norm', 'mla_k_nope_norm', 'mla_k_rope_norm', 'pool_w', 'pool_scale', 'swa_q_norm', 'swa_k_norm', 'swa_sink', 'na_q_norm', 'na_k_norm', 'na_rpb', 'ffn_conv_b']


def _pcall(body, **kw):
    return pl.pallas_call(body, **kw)


def _my_index():
    return 4 * lax.axis_index("x") + 2 * lax.axis_index("y") + lax.axis_index("c")


_COMM_SCRATCH = [pltpu.SemaphoreType.DMA((7,)), pltpu.SemaphoreType.DMA((7,)), pltpu.SemaphoreType.DMA(())]
_ANY = pl.BlockSpec(memory_space=pl.ANY)


def _gather_copies(x_ref, out_ref, send_sems, recv_sems, local_sem):
    x, y, c = lax.axis_index("x"), lax.axis_index("y"), lax.axis_index("c")
    me, sibling = (x, y, c), (x, y, 1 - c)
    chips = [(1 - x, y), (x, 1 - y), (1 - x, 1 - y)]

    def slot(px, py, pc):
        return out_ref.at[4 * px + 2 * py + pc]

    def copy(k, blk, to, src=None):
        return pltpu.make_async_remote_copy(
            src_ref=slot(*blk) if src is None else src, dst_ref=slot(*blk),
            send_sem=send_sems.at[k], recv_sem=recv_sems.at[k], device_id=to, device_id_type=MESH_ID)

    mine = pltpu.make_async_copy(x_ref, slot(*me), local_sem)
    first = [copy(0, me, sibling, src=x_ref)] + [copy(1 + j, me, (*chip, c), src=x_ref) for j, chip in enumerate(chips)]

    def start():
        mine.start()
        for cp in first:
            cp.start()

    def finish():
        passed = [copy(4 + j, (*chip, c), sibling) for j, chip in enumerate(chips)]
        for j, chip in enumerate(chips):
            copy(1 + j, (*chip, c), me).wait_recv()
            passed[j].start()
        copy(0, sibling, me).wait_recv()
        for j, chip in enumerate(chips):
            copy(4 + j, (*chip, 1 - c), me).wait_recv()
        for cp in first + passed:
            cp.wait_send()
        mine.wait()

    return start, finish


def _exchange_copies(t_ref, out_ref, send_sems, recv_sems, local_sem):
    x, y, c = lax.axis_index("x"), lax.axis_index("y"), lax.axis_index("c")
    me = 4 * x + 2 * y + c

    def peer(k):
        return (1 - x if k & 4 else x), (1 - y if k & 2 else y), (1 - c if k & 1 else c)

    def copy(k, landed):
        px, py, pc = peer(k)
        p = 4 * px + 2 * py + pc
        return pltpu.make_async_remote_copy(
            src_ref=t_ref.at[p], dst_ref=out_ref.at[p if landed else me],
            send_sem=send_sems.at[k - 1], recv_sem=recv_sems.at[k - 1], device_id=(px, py, pc), device_id_type=MESH_ID)

    mine = pltpu.make_async_copy(t_ref.at[me], out_ref.at[me], local_sem)
    sends = [copy(k, False) for k in range(1, N_DEV)]

    def start():
        mine.start()
        for cp in sends:
            cp.start()

    def finish():
        for k in range(1, N_DEV):
            copy(k, True).wait_recv()
        for cp in sends:
            cp.wait_send()
        mine.wait()

    return start, finish


_COMM = {'gather': _gather_copies, 'exchange': _exchange_copies}


def _comm_out_shape(kind, operand):
    shape = (N_DEV,) + operand.shape if kind == 'gather' else operand.shape
    return jax.ShapeDtypeStruct(shape, operand.dtype)


def _comm_call(kind, operand, name):
    def body(x_ref, out_ref, send_sems, recv_sems, local_sem):
        start, finish = _COMM[kind](x_ref, out_ref, send_sems, recv_sems, local_sem)
        start()
        finish()

    return _pcall(body, name=name, out_shape=_comm_out_shape(kind, operand), in_specs=[_ANY], out_specs=_ANY,
                  scratch_shapes=_COMM_SCRATCH)(operand)


def _all_gather(block, name):
    return _comm_call('gather', block, name)


def _all_to_all(parts, name):
    return _comm_call('exchange', parts, name)


def _carry_call(body, comm, name, grid, in_specs, out_specs, out_shape, scratch_shapes, semantics, operands):
    if comm is None:
        outs = _pcall(
            body, name=name, grid=grid, in_specs=in_specs, out_specs=out_specs, out_shape=out_shape,
            scratch_shapes=scratch_shapes,
            compiler_params=pltpu.CompilerParams(dimension_semantics=semantics, vmem_limit_bytes=VMEM_LIMIT),
        )(*operands)
        return outs, None

    kind, operand = comm
    n_in, n_out = len(in_specs), len(out_specs)
    steps = 1
    for g in grid:
        steps *= g

    def carrying(*refs):
        ins, x_ref = refs[:n_in], refs[n_in]
        outs, out_ref = refs[n_in + 1:n_in + 1 + n_out], refs[n_in + 1 + n_out]
        scratch, sems = refs[n_in + 2 + n_out:len(refs) - 3], refs[len(refs) - 3:]
        start, finish = _COMM[kind](x_ref, out_ref, *sems)
        step = pl.program_id(0)
        for axis in range(1, len(grid)):
            step = step * grid[axis] + pl.program_id(axis)

        @pl.when(step == 0)
        def _():
            start()

        body(*ins, *outs, *scratch)

        @pl.when(step == steps - 1)
        def _():
            finish()

    outs = _pcall(
        carrying, name=name, grid=grid, in_specs=list(in_specs) + [_ANY], out_specs=list(out_specs) + [_ANY],
        out_shape=list(out_shape) + [_comm_out_shape(kind, operand)], scratch_shapes=list(scratch_shapes) + _COMM_SCRATCH,
        compiler_params=pltpu.CompilerParams(dimension_semantics=("arbitrary",) * len(grid), vmem_limit_bytes=VMEM_LIMIT),
    )(*operands, operand)
    return outs[:-1], outs[-1]


_LANE_TILES = (1024, 768, 512, 384, 256, 128)
_ROW_TILES = (1088, 1024, 512, 256, 128)
_DEPTH_TILES = (2048, 1408) + _LANE_TILES
_TOKEN_DEPTH_TILES = (2176,) + _ROW_TILES


def _pick(dim, cands):
    for t in cands:
        if dim % t == 0:
            return t
    return dim


def _mm(a, b, mode, out_dtype, name, comm=None):
    if mode == 'nn':
        (M, K), (_, N) = a.shape, b.shape
        tm, tn, tk = _pick(M, _ROW_TILES), _pick(N, _LANE_TILES), _pick(K, _DEPTH_TILES)
        a_spec = pl.BlockSpec((tm, tk), lambda i, j, k: (i, k))
        b_spec = pl.BlockSpec((tk, tn), lambda i, j, k: (k, j))
        dn = (((1,), (0,)), ((), ()))
    elif mode == 'nt':
        (M, K), (N, _) = a.shape, b.shape
        tm, tn, tk = _pick(M, _ROW_TILES), _pick(N, _LANE_TILES), _pick(K, _DEPTH_TILES)
        a_spec = pl.BlockSpec((tm, tk), lambda i, j, k: (i, k))
        b_spec = pl.BlockSpec((tn, tk), lambda i, j, k: (j, k))
        dn = (((1,), (1,)), ((), ()))
    else:
        (K, M), (_, N) = a.shape, b.shape
        tm, tn, tk = _pick(M, _LANE_TILES), _pick(N, _LANE_TILES), _pick(K, _TOKEN_DEPTH_TILES)
        a_spec = pl.BlockSpec((tk, tm), lambda i, j, k: (k, i))
        b_spec = pl.BlockSpec((tk, tn), lambda i, j, k: (k, j))
        dn = (((0,), (0,)), ((), ()))
    grid = (M // tm, N // tn, K // tk)

    def matmul_step(a_ref, b_ref, o_ref, acc):
        @pl.when(pl.program_id(2) == 0)
        def _():
            acc[...] = jnp.zeros_like(acc)

        acc[...] += lax.dot_general(a_ref[...], b_ref[...], dn, preferred_element_type=F32)

        @pl.when(pl.program_id(2) == grid[2] - 1)
        def _():
            o_ref[...] = acc[...].astype(o_ref.dtype)

    outs, moved = _carry_call(
        matmul_step, comm, name, grid, [a_spec, b_spec], [pl.BlockSpec((tm, tn), lambda i, j, k: (i, j))],
        [jax.ShapeDtypeStruct((M, N), out_dtype)], [pltpu.VMEM((tm, tn), F32)], ("parallel", "parallel", "arbitrary"),
        (a, b))
    return outs[0] if comm is None else (outs[0], moved)


def _proj_fwd(a16, w, next_shard, name):
    if next_shard is None:
        return _mm(a16, w, 'nn', F32, name + '_nn'), None
    return _mm(a16, w, 'nn', F32, name + '_nn_gather', comm=('gather', next_shard.astype(BF16)))


def _proj_bwd(a16, w, dy, d_gathered, name):
    dy16 = dy.astype(BF16)
    if d_gathered is None:
        return _mm(dy16, w, 'nt', F32, name + '_nt'), _mm(a16, dy16, 'tn', BF16, name + '_tn'), None
    half = d_gathered.shape[1] // 2
    dw, top = _mm(a16, dy16, 'tn', BF16, name + '_tn_exchange', comm=('exchange', d_gathered[:, :half]))
    da, low = _mm(dy16, w, 'nt', F32, name + '_nt_exchange', comm=('exchange', d_gathered[:, half:]))
    d_shard = jnp.concatenate([_sum_parts(top, name + '_sum_top'), _sum_parts(low, name + '_sum_low')], axis=0)
    return da, dw, d_shard


@functools.partial(jax.custom_vjp, nondiff_argnums=(3,))
def pmm(a, w, next_shard, name):
    return _proj_fwd(a.astype(BF16), w, next_shard, name)


def _pmm_fwd(a, w, next_shard, name):
    a16 = a.astype(BF16)
    return _proj_fwd(a16, w, next_shard, name), (a16, w)


def _pmm_bwd(name, res, cts):
    return _proj_bwd(*res, cts[0], cts[1], name)


pmm.defvjp(_pmm_fwd, _pmm_bwd)


def _normmod_rows(x, ctx):
    n, d = x.shape
    rows = (n,) if ctx is None else (n, ctx.shape[0])
    tr = next(t for t in (256, 128, 64, 32, 16, 8) if all(r % t == 0 for r in rows))
    nx = n // tr
    specs = [pl.BlockSpec((tr, d), lambda i: (jnp.minimum(i, nx - 1), 0))]
    if ctx is not None:
        specs.append(pl.BlockSpec((tr, d), lambda i: (jnp.maximum(i - nx, 0), 0)))
    return tr, nx, sum(rows) // tr, specs


def _normmod_terms(refs, has_ctx, nx):
    is_x = pl.program_id(0) < nx
    vec_ref = refs[2] if has_ctx else refs[1]
    t = jnp.where(is_x, refs[0][...], refs[1][...]) if has_ctx else refs[0][...]
    shift = jnp.where(is_x, vec_ref[1:2, :], vec_ref[3:4, :])
    scale = jnp.where(is_x, vec_ref[2:3, :], vec_ref[4:5, :])
    r = lax.rsqrt(jnp.mean(t * t, axis=1, keepdims=True) + NORM_EPS)
    return is_x, t, r, vec_ref[0:1, :], shift, scale


def _normmod_fwd_call(x, ctx, vec):
    has_ctx = ctx is not None
    d = x.shape[1]
    tr, nx, blocks, specs = _normmod_rows(x, ctx)

    def body(*refs):
        _, t, r, g, shift, scale = _normmod_terms(refs, has_ctx, nx)
        refs[-1][...] = ((t * r * g) * (1.0 + scale) + shift).astype(BF16)

    operands = (x, ctx, vec) if has_ctx else (x, vec)
    return _pcall(
        body, name='normmod_fwd', grid=(blocks,), in_specs=specs + [pl.BlockSpec((8, d), lambda i: (0, 0))],
        out_specs=pl.BlockSpec((tr, d), lambda i: (i, 0)), out_shape=jax.ShapeDtypeStruct((blocks * tr, d), BF16),
        compiler_params=pltpu.CompilerParams(dimension_semantics=("parallel",), vmem_limit_bytes=VMEM_LIMIT),
    )(*operands)


def _normmod_bwd_call(x, ctx, vec, dh):
    has_ctx = ctx is not None
    d = x.shape[1]
    tr, nx, blocks, specs = _normmod_rows(x, ctx)
    n_in = len(specs) + 2

    def body(*refs):
        is_x, t, r, g, shift, scale = _normmod_terms(refs, has_ctx, nx)
        dh_ref, dx_ref, dvec_ref = refs[n_in - 1], refs[n_in], refs[-1]

        @pl.when(pl.program_id(0) == 0)
        def _():
            dvec_ref[...] = jnp.zeros_like(dvec_ref)

        xn = t * r
        dh = dh_ref[...]
        dy = dh * (1.0 + scale)
        dxn = dy * g
        dt = r * (dxn - xn * jnp.mean(dxn * xn, axis=1, keepdims=True))
        d_shift = jnp.sum(dh, axis=0, keepdims=True)
        d_scale = jnp.sum(dh * (xn * g), axis=0, keepdims=True)
        dvec_ref[0:1, :] += jnp.sum(dy * xn, axis=0, keepdims=True)

        @pl.when(is_x)
        def _():
            dx_ref[...] = dt
            dvec_ref[1:2, :] += d_shift
            dvec_ref[2:3, :] += d_scale

        if has_ctx:
            @pl.when(jnp.logical_not(is_x))
            def _():
                refs[n_in + 1][...] = dt
                dvec_ref[3:4, :] += d_shift
                dvec_ref[4:5, :] += d_scale

    vec_spec = pl.BlockSpec((8, d), lambda i: (0, 0))
    operands = (x, ctx, vec, dh) if has_ctx else (x, vec, dh)
    shapes = [jax.ShapeDtypeStruct(x.shape, F32)] + ([jax.ShapeDtypeStruct(ctx.shape, F32)] if has_ctx else [])
    outs = _pcall(
        body, name='normmod_bwd', grid=(blocks,),
        in_specs=specs + [vec_spec, pl.BlockSpec((tr, d), lambda i: (i, 0))],
        out_specs=specs + [vec_spec], out_shape=shapes + [jax.ShapeDtypeStruct((8, d), F32)],
        compiler_params=pltpu.CompilerParams(dimension_semantics=("arbitrary",), vmem_limit_bytes=VMEM_LIMIT),
    )(*operands)
    return (outs[0], outs[1], outs[2]) if has_ctx else (outs[0], None, outs[1])


@functools.partial(jax.custom_vjp, nondiff_argnums=(5,))
def norm_proj(x, ctx, vec, w, next_shard, name):
    return _norm_proj_fwd(x, ctx, vec, w, next_shard, name)[0]


def _norm_proj_fwd(x, ctx, vec, w, next_shard, name):
    h16 = _normmod_fwd_call(x, ctx, vec)
    return _proj_fwd(h16, w, next_shard, name), (x, ctx, vec, h16, w)


def _norm_proj_bwd(name, res, cts):
    x, ctx, vec, h16, w = res
    dh, dw, d_shard = _proj_bwd(h16, w, cts[0], cts[1], name)
    dx, dctx, dvec = _normmod_bwd_call(x, ctx, vec, dh)
    return dx, dctx, dvec, dw, d_shard


norm_proj.defvjp(_norm_proj_fwd, _norm_proj_bwd)


FFN_TILE = 128


def _neighbours(x, n):
    T = x.shape[0]
    t = lax.broadcasted_iota(jnp.int32, x.shape, 0)
    prev = jnp.where((t == 0) | (t == n), 0.0, pltpu.roll(x, 1, 0))
    nxt = jnp.where((t == n - 1) | (t == T - 1), 0.0, pltpu.roll(x, T - 1, 0))
    return prev, nxt


def _gate_specs(rows, f):
    tiles = f // FFN_TILE
    return [pl.BlockSpec((rows, FFN_TILE), lambda j: (0, j)), pl.BlockSpec((rows, FFN_TILE), lambda j: (0, j + tiles))]


def _conv3(x, cw_ref, cb_ref, n):
    prev, nxt = _neighbours(x, n)
    return prev * cw_ref[0:1, :] + x * cw_ref[1:2, :] + nxt * cw_ref[2:3, :] + cb_ref[...], prev, nxt


def _gate_fwd_call(a, cw, cb, n):
    T, f = a.shape[0], a.shape[1] // 2

    def body(ag_ref, av_ref, wg_ref, wv_ref, bg_ref, bv_ref, u_ref):
        g = _conv3(ag_ref[...], wg_ref, bg_ref, n)[0]
        v = _conv3(av_ref[...], wv_ref, bv_ref, n)[0]
        u_ref[...] = (g * jax.nn.sigmoid(g) * v).astype(u_ref.dtype)

    return _pcall(
        body, name='ffn_gate_fwd', grid=(f // FFN_TILE,),
        in_specs=_gate_specs(T, f) + _gate_specs(3, f) + _gate_specs(1, f),
        out_specs=pl.BlockSpec((T, FFN_TILE), lambda j: (0, j)),
        out_shape=jax.ShapeDtypeStruct((T, f), BF16),
        compiler_params=pltpu.CompilerParams(dimension_semantics=("parallel",), vmem_limit_bytes=VMEM_LIMIT),
    )(a, a, cw, cw, cb, cb)


def _gate_bwd_call(a, cw, cb, du, n):
    T, f = a.shape[0], a.shape[1] // 2

    def half(dz, x, prev, nxt, w_ref, da_ref, dw_ref):
        t = lax.broadcasted_iota(jnp.int32, dz.shape, 0)
        from_next = pltpu.roll(jnp.where((t == 0) | (t == n), 0.0, dz), T - 1, 0)
        from_prev = pltpu.roll(jnp.where((t == n - 1) | (t == T - 1), 0.0, dz), 1, 0)
        da_ref[...] = (dz * w_ref[1:2, :] + from_next * w_ref[0:1, :] + from_prev * w_ref[2:3, :]).astype(da_ref.dtype)
        dw_ref[0:1, :] = jnp.sum(dz * prev, axis=0, keepdims=True)
        dw_ref[1:2, :] = jnp.sum(dz * x, axis=0, keepdims=True)
        dw_ref[2:3, :] = jnp.sum(dz * nxt, axis=0, keepdims=True)
        dw_ref[3:4, :] = jnp.sum(dz, axis=0, keepdims=True)
        dw_ref[4:8, :] = jnp.zeros((4, FFN_TILE), F32)

    def body(ag_ref, av_ref, wg_ref, wv_ref, bg_ref, bv_ref, du_ref, dag_ref, dav_ref, dwg_ref, dwv_ref):
        xg, xv = ag_ref[...], av_ref[...]
        g, g_prev, g_next = _conv3(xg, wg_ref, bg_ref, n)
        v, v_prev, v_next = _conv3(xv, wv_ref, bv_ref, n)
        sg = jax.nn.sigmoid(g)
        du = du_ref[...]
        half(du * v * (sg * (1.0 + g * (1.0 - sg))), xg, g_prev, g_next, wg_ref, dag_ref, dwg_ref)
        half(du * (g * sg), xv, v_prev, v_next, wv_ref, dav_ref, dwv_ref)

    tile = lambda rows: pl.BlockSpec((rows, FFN_TILE), lambda j: (0, j))
    return _pcall(
        body, name='ffn_gate_bwd', grid=(f // FFN_TILE,),
        in_specs=_gate_specs(T, f) + _gate_specs(3, f) + _gate_specs(1, f) + [tile(T)],
        out_specs=[tile(T), tile(T), tile(8), tile(8)],
        out_shape=[jax.ShapeDtypeStruct((T, f), BF16), jax.ShapeDtypeStruct((T, f), BF16),
                   jax.ShapeDtypeStruct((8, f), F32), jax.ShapeDtypeStruct((8, f), F32)],
        compiler_params=pltpu.CompilerParams(dimension_semantics=("parallel",), vmem_limit_bytes=VMEM_LIMIT),
    )(a, a, cw, cw, cb, cb, du)


@functools.partial(jax.custom_vjp, nondiff_argnums=(9,))
def conv_ffn(x, ctx, vec, w_up, cw, cb, w_down, next_up, next_down, n):
    return _conv_ffn_fwd(x, ctx, vec, w_up, cw, cb, w_down, next_up, next_down, n)[0]


def _conv_ffn_fwd(x, ctx, vec, w_up, cw, cb, w_down, next_up, next_down, n):
    h16 = _normmod_fwd_call(x, ctx, vec)
    a, g_up = _proj_fwd(h16, w_up, next_up, 'ffn_up')
    u16 = _gate_fwd_call(a, cw, cb.reshape(1, -1), n)
    y, g_down = _proj_fwd(u16, w_down, next_down, 'ffn_down')
    return (y, g_up, g_down), (x, ctx, vec, h16, w_up, cw, cb, w_down, a, u16)


def _conv_ffn_bwd(n, res, cts):
    x, ctx, vec, h16, w_up, cw, cb, w_down, a, u16 = res
    dy, d_g_up, d_g_down = cts
    du, dw_down, d_next_down = _proj_bwd(u16, w_down, dy, d_g_down, 'ffn_down')
    dag, dav, dwg, dwv = _gate_bwd_call(a, cw, cb.reshape(1, -1), du, n)
    dcw = jnp.concatenate([dwg, dwv], axis=1)
    dh, dw_up, d_next_up = _proj_bwd(h16, w_up, jnp.concatenate([dag, dav], axis=1), d_g_up, 'ffn_up')
    dx, dctx, dvec = _normmod_bwd_call(x, ctx, vec, dh)
    return dx, dctx, dvec, dw_up, dcw[:3], dcw[3], dw_down, d_next_up, d_next_down


conv_ffn.defvjp(_conv_ffn_fwd, _conv_ffn_bwd)


def _attn_fwd_call(q, k, v, scale, comm=None):
    H, nq, dq = q.shape
    nk, dv = v.shape[1], v.shape[2]
    tq = _pick(nq, (256, 128))

    def body(q_ref, k_ref, v_ref, o_ref, lse_ref):
        s = lax.dot_general(q_ref[0], k_ref[0], (((1,), (1,)), ((), ())), preferred_element_type=F32) * scale
        m = jnp.max(s, axis=1, keepdims=True)
        p = jnp.exp(s - m)
        l = jnp.sum(p, axis=1, keepdims=True)
        o_ref[...] = jnp.dot(p.astype(BF16), v_ref[0], preferred_element_type=F32) * (1.0 / l)
        lse_ref[0] = m + jnp.log(l)

    return _carry_call(
        body, comm, 'mla_attn_fwd', (H, nq // tq),
        [pl.BlockSpec((1, tq, dq), lambda h, i: (h, i, 0)),
         pl.BlockSpec((1, nk, dq), lambda h, i: (h, 0, 0)),
         pl.BlockSpec((1, nk, dv), lambda h, i: (h, 0, 0))],
        [pl.BlockSpec((tq, dv), lambda h, i: (i, h)), pl.BlockSpec((1, tq, 1), lambda h, i: (h, i, 0))],
        [jax.ShapeDtypeStruct((nq, H * dv), F32), jax.ShapeDtypeStruct((H, nq, 1), F32)], [],
        ("parallel", "parallel"), (q, k, v))


def _attn_bwd_call(q, k, v, o, lse, do, scale, comm=None):
    H, nq, dq = q.shape
    nk, dv = v.shape[1], v.shape[2]
    tq = _pick(nq, (128,))

    def body(q_ref, k_ref, v_ref, o_ref, lse_ref, do_ref, dq_ref, dk_ref, dv_ref):
        @pl.when(pl.program_id(1) == 0)
        def _():
            dk_ref[...] = jnp.zeros_like(dk_ref)
            dv_ref[...] = jnp.zeros_like(dv_ref)

        q16, k16, v16 = q_ref[0], k_ref[0], v_ref[0]
        do = do_ref[...]
        do16 = do.astype(BF16)
        s = lax.dot_general(q16, k16, (((1,), (1,)), ((), ())), preferred_element_type=F32) * scale
        p = jnp.exp(s - lse_ref[0])
        dv_ref[0] += lax.dot_general(p.astype(BF16), do16, (((0,), (0,)), ((), ())), preferred_element_type=F32)
        dp = lax.dot_general(do16, v16, (((1,), (1,)), ((), ())), preferred_element_type=F32)
        delta = jnp.sum(do * o_ref[...], axis=1, keepdims=True)
        ds16 = (p * (dp - delta) * scale).astype(BF16)
        dq_ref[0] = jnp.dot(ds16, k16, preferred_element_type=F32)
        dk_ref[0] += lax.dot_general(ds16, q16, (((0,), (0,)), ((), ())), preferred_element_type=F32)

    return _carry_call(
        body, comm, 'mla_attn_bwd', (H, nq // tq),
        [pl.BlockSpec((1, tq, dq), lambda h, i: (h, i, 0)),
         pl.BlockSpec((1, nk, dq), lambda h, i: (h, 0, 0)),
         pl.BlockSpec((1, nk, dv), lambda h, i: (h, 0, 0)),
         pl.BlockSpec((tq, dv), lambda h, i: (i, h)),
         pl.BlockSpec((1, tq, 1), lambda h, i: (h, i, 0)),
         pl.BlockSpec((tq, dv), lambda h, i: (i, h))],
        [pl.BlockSpec((1, tq, dq), lambda h, i: (h, i, 0)),
         pl.BlockSpec((1, nk, dq), lambda h, i: (h, 0, 0)),
         pl.BlockSpec((1, nk, dv), lambda h, i: (h, 0, 0))],
        [jax.ShapeDtypeStruct((H, nq, dq), F32), jax.ShapeDtypeStruct((H, nk, dq), F32),
         jax.ShapeDtypeStruct((H, nk, dv), F32)], [], ("parallel", "arbitrary"), (q, k, v, o, lse, do))


def _gather_comm(shard):
    return None if shard is None else ('gather', shard.astype(BF16))


def _exchange_comm(d_gathered):
    return None if d_gathered is None else ('exchange', d_gathered)


def _reduce_moved(parts, name):
    return None if parts is None else _sum_parts(parts, name)


@functools.partial(jax.custom_vjp, nondiff_argnums=(4,))
def attention(q, k, v, shard, scale):
    (o, _), gathered = _attn_fwd_call(q.astype(BF16), k.astype(BF16), v.astype(BF16), scale, _gather_comm(shard))
    return o, gathered


def _attention_fwd(q, k, v, shard, scale):
    q16, k16, v16 = q.astype(BF16), k.astype(BF16), v.astype(BF16)
    (o, lse), gathered = _attn_fwd_call(q16, k16, v16, scale, _gather_comm(shard))
    return (o, gathered), (q16, k16, v16, o, lse)


def _attention_bwd(scale, res, cts):
    q16, k16, v16, o, lse = res
    grads, parts = _attn_bwd_call(q16, k16, v16, o, lse, cts[0], scale, _exchange_comm(cts[1]))
    return tuple(grads) + (_reduce_moved(parts, 'mla_attn_sum'),)


attention.defvjp(_attention_fwd, _attention_bwd)


def _win_geometry(kind, n):
    if kind == 'na':
        rows = n // GRID_W
        kh = min(NA_KH, rows)

        def start(i):
            return jnp.clip(i - kh // 2, 0, rows - kh) * GRID_W

        def bidx(i):
            return jnp.clip(i - kh // 2, 0, rows - kh) - i + (NA_KH - 1)

        return GRID_W, kh * GRID_W, start, bidx
    nb = n // SWA_BLOCK

    def start(i):
        return i * SWA_BLOCK

    def bidx(i):
        return jnp.where(i == 0, 0, jnp.where(i == nb - 1, 2, 1))

    return SWA_BLOCK, 3 * SWA_BLOCK, start, bidx


def _dot_nt(a, b):
    return lax.dot_general(a, b, (((1,), (1,)), ((), ())), preferred_element_type=F32)


def _dot_tn(a, b):
    return lax.dot_general(a, b, (((0,), (0,)), ((), ())), preferred_element_type=F32)


def _win_specs(q, k, kc, bias):
    hq, n, d = q.shape
    grp = hq // k.shape[0]
    hb = bias.shape[0]
    return [
        pl.BlockSpec((1, n, d), lambda h: (h, 0, 0)),
        pl.BlockSpec((1,) + k.shape[1:], lambda h: (h // grp, 0, 0)),
        pl.BlockSpec((1,) + k.shape[1:], lambda h: (h // grp, 0, 0)),
        pl.BlockSpec((1,) + kc.shape[1:], lambda h: (h // grp, 0, 0)),
        pl.BlockSpec((1,) + kc.shape[1:], lambda h: (h // grp, 0, 0)),
        pl.BlockSpec((1,) + bias.shape[1:], (lambda h: (h, 0, 0, 0)) if hb > 1 else (lambda h: (0, 0, 0, 0))),
        pl.BlockSpec(memory_space=pltpu.SMEM),
    ]


def _win_fwd_call(q, k, v, kc, vc, bias, sink, kind, comm=None):
    hq, n, d = q.shape
    scale = d ** -0.5
    qb, wk, start, bidx = _win_geometry(kind, n)
    has_sink = kind == 'swa'

    def body(q_ref, k_ref, v_ref, kc_ref, vc_ref, b_ref, sink_ref, o_ref, lse_ref):
        kc16, vc16 = kc_ref[0], vc_ref[0]
        snk = sink_ref[pl.program_id(0)]

        def step(i, carry):
            qs = pl.multiple_of(i * qb, qb)
            ks = pl.multiple_of(start(i), GRID_W)
            q16 = q_ref[0, pl.ds(qs, qb), :]
            s1 = _dot_nt(q16, k_ref[0, pl.ds(ks, wk), :]) * scale + b_ref[0, bidx(i)]
            s2 = _dot_nt(q16, kc16) * scale
            m = jnp.maximum(jnp.max(s1, axis=1, keepdims=True), jnp.max(s2, axis=1, keepdims=True))
            if has_sink:
                m = jnp.maximum(m, snk)
            p1 = jnp.exp(s1 - m)
            p2 = jnp.exp(s2 - m)
            l = jnp.sum(p1, axis=1, keepdims=True) + jnp.sum(p2, axis=1, keepdims=True)
            if has_sink:
                l = l + jnp.exp(snk - m)
            o = (jnp.dot(p1.astype(BF16), v_ref[0, pl.ds(ks, wk), :], preferred_element_type=F32)
                 + jnp.dot(p2.astype(BF16), vc16, preferred_element_type=F32))
            o_ref[0, pl.ds(qs, qb), :] = o * (1.0 / l)
            lse_ref[0, pl.ds(qs, qb), :] = m + jnp.log(l)
            return carry

        lax.fori_loop(0, n // qb, step, 0, unroll=8)

    return _carry_call(
        body, comm, kind + '_attn_fwd', (hq,), _win_specs(q, k, kc, bias),
        [pl.BlockSpec((1, n, d), lambda h: (h, 0, 0)), pl.BlockSpec((1, n, 1), lambda h: (h, 0, 0))],
        [jax.ShapeDtypeStruct((hq, n, d), F32), jax.ShapeDtypeStruct((hq, n, 1), F32)], [], ("parallel",),
        (q, k, v, kc, vc, bias, sink))


def _win_bwd_call(q, k, v, kc, vc, bias, sink, o, lse, do, kind, comm=None):
    hq, n, d = q.shape
    scale = d ** -0.5
    qb, wk, start, bidx = _win_geometry(kind, n)
    has_sink = kind == 'swa'
    bias_grad = kind == 'na'

    def body(q_ref, k_ref, v_ref, kc_ref, vc_ref, b_ref, sink_ref, o_ref, lse_ref, do_ref,
             dq_ref, dk_ref, dv_ref, dkc_ref, dvc_ref, db_ref, dsink_ref):
        kc16, vc16 = kc_ref[0], vc_ref[0]
        snk = sink_ref[pl.program_id(0)]
        dk_ref[...] = jnp.zeros_like(dk_ref)
        dv_ref[...] = jnp.zeros_like(dv_ref)
        dkc_ref[...] = jnp.zeros_like(dkc_ref)
        dvc_ref[...] = jnp.zeros_like(dvc_ref)
        db_ref[...] = jnp.zeros_like(db_ref)

        def step(i, dsink):
            qs = pl.multiple_of(i * qb, qb)
            ks = pl.multiple_of(start(i), GRID_W)
            q16 = q_ref[0, pl.ds(qs, qb), :]
            k16 = k_ref[0, pl.ds(ks, wk), :]
            v16 = v_ref[0, pl.ds(ks, wk), :]
            lse = lse_ref[0, pl.ds(qs, qb), :]
            do = do_ref[0, pl.ds(qs, qb), :]
            do16 = do.astype(BF16)
            p1 = jnp.exp(_dot_nt(q16, k16) * scale + b_ref[0, bidx(i)] - lse)
            p2 = jnp.exp(_dot_nt(q16, kc16) * scale - lse)
            delta = jnp.sum(do * o_ref[0, pl.ds(qs, qb), :], axis=1, keepdims=True)
            ds1 = p1 * (_dot_nt(do16, v16) - delta)
            ds2 = p2 * (_dot_nt(do16, vc16) - delta)
            if bias_grad:
                db_ref[0, bidx(i)] += ds1
            ds1 = (ds1 * scale).astype(BF16)
            ds2 = (ds2 * scale).astype(BF16)
            dq_ref[0, pl.ds(qs, qb), :] = (jnp.dot(ds1, k16, preferred_element_type=F32)
                                          + jnp.dot(ds2, kc16, preferred_element_type=F32))
            dk_ref[0, pl.ds(ks, wk), :] += _dot_tn(ds1, q16)
            dv_ref[0, pl.ds(ks, wk), :] += _dot_tn(p1.astype(BF16), do16)
            dkc_ref[0] += _dot_tn(ds2, q16)
            dvc_ref[0] += _dot_tn(p2.astype(BF16), do16)
            if has_sink:
                dsink = dsink - jnp.sum(jnp.exp(snk - lse) * delta)
            return dsink

        dsink = lax.fori_loop(0, n // qb, step, jnp.zeros((), F32), unroll=4)
        dsink_ref[...] = jnp.full(dsink_ref.shape, dsink, F32)

    per_head = lambda shape: pl.BlockSpec((1,) + shape[1:], lambda h: (h,) + (0,) * (len(shape) - 1))
    kq = (hq,) + k.shape[1:]
    cq = (hq,) + kc.shape[1:]
    bq = (hq,) + bias.shape[1:]
    in_specs = _win_specs(q, k, kc, bias) + [per_head(o.shape), per_head(lse.shape), per_head(do.shape)]
    out_shapes = [q.shape, kq, kq, cq, cq, bq, (hq, 8, 128)]
    return _carry_call(
        body, comm, kind + '_attn_bwd', (hq,), in_specs, [per_head(s) for s in out_shapes],
        [jax.ShapeDtypeStruct(s, F32) for s in out_shapes], [], ("parallel",),
        (q, k, v, kc, vc, bias, sink, o, lse, do))


@functools.partial(jax.custom_vjp, nondiff_argnums=(8,))
def win_attention(q, k, v, kc, vc, bias, sink, shard, kind):
    b16 = lambda t: t.astype(BF16)
    (o, _), gathered = _win_fwd_call(b16(q), b16(k), b16(v), b16(kc), b16(vc), bias, sink, kind, _gather_comm(shard))
    return o, gathered


def _win_attention_fwd(q, k, v, kc, vc, bias, sink, shard, kind):
    res = tuple(t.astype(BF16) for t in (q, k, v, kc, vc)) + (bias, sink)
    (o, lse), gathered = _win_fwd_call(*res, kind, _gather_comm(shard))
    return (o, gathered), res + (o, lse)


def _win_attention_bwd(kind, res, cts):
    q, k, v, kc, vc, bias, sink, o, lse = res
    (dq, dk, dv, dkc, dvc, db, dsink), parts = _win_bwd_call(
        q, k, v, kc, vc, bias, sink, o, lse, cts[0], kind, _exchange_comm(cts[1]))
    hkv = k.shape[0]
    fold = lambda t: t.reshape((hkv, -1) + t.shape[1:]).sum(axis=1)
    if bias.shape[0] == 1:
        db = jnp.zeros_like(bias)
    return (dq, fold(dk), fold(dv), fold(dkc), fold(dvc), db, dsink[:, 0, 0], _reduce_moved(parts, kind + '_attn_sum'))


win_attention.defvjp(_win_attention_fwd, _win_attention_bwd)


def _loss_head(y, target):
    n, d = y.shape
    tr = _pick(n, (512, 256, 128))
    nb = n // tr

    def body(y_ref, t_ref, dy_ref, part_ref):
        err = y_ref[...] - t_ref[...]
        dy_ref[...] = err * (1.0 / d)
        part_ref[...] = jnp.full(part_ref.shape, jnp.sum(err * err), F32)

    dy, part = _pcall(
        body, name='loss_head', grid=(nb,),
        in_specs=[pl.BlockSpec((tr, d), lambda i: (i, 0)), pl.BlockSpec((tr, d), lambda i: (i, 0))],
        out_specs=[pl.BlockSpec((tr, d), lambda i: (i, 0)), pl.BlockSpec((1, 8, 128), lambda i: (i, 0, 0))],
        out_shape=[jax.ShapeDtypeStruct((n, d), F32), jax.ShapeDtypeStruct((nb, 8, 128), F32)],
        compiler_params=pltpu.CompilerParams(dimension_semantics=("parallel",), vmem_limit_bytes=VMEM_LIMIT),
    )(y, target)
    return 0.5 * jnp.sum(part[:, 0, 0]) / d, dy


def _sum_parts(parts, name):
    P, R, C = parts.shape
    tr = _pick(R, (256, 128, 64, 32, 16, 8))

    def body(p_ref, o_ref):
        acc = p_ref[0].astype(F32)
        for i in range(1, P):
            acc = acc + p_ref[i].astype(F32)
        o_ref[...] = acc

    return _pcall(
        body, name=name, grid=(R // tr,),
        in_specs=[pl.BlockSpec((P, tr, C), lambda i: (0, i, 0))],
        out_specs=pl.BlockSpec((tr, C), lambda i: (i, 0)),
        out_shape=jax.ShapeDtypeStruct((R, C), F32),
        compiler_params=pltpu.CompilerParams(dimension_semantics=("parallel",), vmem_limit_bytes=VMEM_LIMIT),
    )(parts)


def _adamw(parts, w, m, v, name):
    P, R, C = parts.shape
    tr = _pick(R, (128, 64, 32, 16, 8))
    c1 = 1.0 / (1.0 - ADAM_B1 ** ADAM_STEP)
    c2 = 1.0 / (1.0 - ADAM_B2 ** ADAM_STEP)

    def body(p_ref, w_ref, m_ref, v_ref, g_out, d_out, m_out, v_out):
        g = p_ref[0].astype(F32)
        for i in range(1, P):
            g = g + p_ref[i].astype(F32)
        m_new = ADAM_B1 * m_ref[...] + (1.0 - ADAM_B1) * g
        v_new = ADAM_B2 * v_ref[...] + (1.0 - ADAM_B2) * (g * g)
        g_out[...] = g
        m_out[...] = m_new
        v_out[...] = v_new
        d_out[...] = -ADAM_LR * ((m_new * c1) / (jnp.sqrt(v_new * c2) + ADAM_EPS) + ADAM_WD * w_ref[...])

    blk = pl.BlockSpec((tr, C), lambda i: (i, 0))
    return _pcall(
        body, name=name, grid=(R // tr,),
        in_specs=[pl.BlockSpec((P, tr, C), lambda i: (0, i, 0)), blk, blk, blk],
        out_specs=[blk, blk, blk, blk],
        out_shape=[jax.ShapeDtypeStruct((R, C), F32)] * 4,
        compiler_params=pltpu.CompilerParams(dimension_semantics=("parallel",), vmem_limit_bytes=VMEM_LIMIT),
    )(parts, w, m, v)


def rms_norm(x, g):
    return x * lax.rsqrt(jnp.mean(x * x, axis=-1, keepdims=True) + NORM_EPS) * g


def modulate(h, shift, scale):
    return h * (1.0 + scale) + shift


def axial_angles(n, d_rot):
    t = jnp.arange(n)
    row = (t // GRID_W).astype(F32)
    col = (t % GRID_W).astype(F32)
    d_axis = d_rot // 2
    inv_freq = ROPE_BASE ** (-jnp.arange(0, d_axis, 2, dtype=F32) / d_axis)
    return (row[:, None] * inv_freq, col[:, None] * inv_freq)


def rope_segment(x, ang):
    cos = jnp.cos(ang)[:, None, :]
    sin = jnp.sin(ang)[:, None, :]
    x1, x2 = jnp.split(x, 2, axis=-1)
    return jnp.concatenate([x1 * cos - x2 * sin, x2 * cos + x1 * sin], axis=-1)


def axial_rope(x, ang):
    half = x.shape[-1] // 2
    return jnp.concatenate([rope_segment(x[..., :half], ang[0]), rope_segment(x[..., half:], ang[1])], axis=-1)


def rope_latent(t, n, ang):
    return jnp.concatenate([axial_rope(t[:n], ang), t[n:]], axis=0)


def mla_attend_ctx(q_nope, q_rope, k_nope, k_rope, v):
    s = (jnp.einsum('qhd,khd->hqk', q_nope, k_nope, preferred_element_type=F32)
         + jnp.einsum('qhr,kr->hqk', q_rope, k_rope, preferred_element_type=F32))
    p = jax.nn.softmax(s * MLA_SCALE, axis=-1)
    return jnp.einsum('hqk,khd->qhd', p, v)


def pool_mixer(u, w_pool, scale):
    n = u.shape[0]
    t = jnp.arange(n)
    diffs = []
    for g, w in enumerate(POOL_WINDOWS):
        ug = u[:, g * POOL_GROUP:(g + 1) * POOL_GROUP]
        padded = jnp.pad(ug, ((w // 2, w // 2), (0, 0)))
        total = padded[0:n]
        for j in range(1, w):
            total = total + padded[j:j + n]
        count = (jnp.clip(t + w // 2, 0, n) - jnp.clip(t - w // 2, 0, n)).astype(F32)
        diffs.append(total / count[:, None] - ug)
    d = jnp.stack(diffs, axis=1)
    y = jnp.einsum('ngc,gcd->ngd', d, w_pool).reshape(n, POOL_WIDTH)
    return y * scale


def swa_latent(q, k, v, k_ctx, v_ctx, sink, shard):
    n, hq, d = q.shape
    blk = SWA_BLOCK
    a = jnp.arange(blk)[:, None]
    j = jnp.arange(3 * blk)[None, :]
    near = jnp.abs(j - blk - a) <= SWA_WINDOW
    tiles = jnp.stack([near & (j >= blk), near, near & (j < 2 * blk)])
    bias = jnp.where(tiles, 0.0, NEG_INF).astype(F32)[None]
    heads = lambda t: t.transpose(1, 0, 2)
    pad = lambda t: jnp.pad(heads(t), ((0, 0), (blk, blk), (0, 0)))
    o, gathered = win_attention(heads(q), pad(k), pad(v), heads(k_ctx), heads(v_ctx), bias, sink, shard, 'swa')
    return o.transpose(1, 0, 2).reshape(n, hq * d), gathered


def ctx_attention(q, k, v, sink):
    nq, hq, d = q.shape
    hkv = k.shape[1]
    grp = hq // hkv
    nk = k.shape[0]
    qg = q.reshape(nq, hkv, grp, d)
    s = jnp.einsum('qhgd,khd->hgqk', qg, k, preferred_element_type=F32) * (d ** -0.5)
    if sink is not None:
        s_sink = jnp.broadcast_to(sink.reshape(hkv, grp)[:, :, None, None], s.shape[:-1] + (1,))
        s = jnp.concatenate([s, s_sink], axis=-1)
    p = jax.nn.softmax(s, axis=-1)[..., :nk]
    o = jnp.einsum('hgqk,khd->qhgd', p, v)
    return o.reshape(nq, hq * d)


def na_bias_tiles(rpb, n):
    rows = n // GRID_W
    kh = min(NA_KH, rows)
    qc = jnp.arange(GRID_W)[:, None]
    kc = jnp.arange(GRID_W)[None, :]
    dc = jnp.clip(kc - qc, 1 - NA_KW, NA_KW - 1) + (NA_KW - 1)
    onehot = (dc[None] == jnp.arange(2 * NA_KW - 1)[:, None, None]).astype(F32)
    toeplitz = jnp.einsum('hdt,tqk->hdqk', rpb, onehot, precision=lax.Precision.HIGHEST)
    q_col0 = jnp.clip(qc - NA_KW // 2, 0, GRID_W - NA_KW)
    valid = (kc >= q_col0) & (kc < q_col0 + NA_KW)
    masked = jnp.where(valid, toeplitz, NEG_INF)
    return jnp.stack([jnp.concatenate([masked[:, off + j] for j in range(kh)], axis=-1) for off in range(NA_KH)], axis=1)


def na_latent(q, k, v, k_ctx, v_ctx, rpb, shard):
    n, h, d = q.shape
    heads = lambda t: t.transpose(1, 0, 2)
    o, gathered = win_attention(heads(q), heads(k), heads(v), heads(k_ctx), heads(v_ctx), na_bias_tiles(rpb, n),
                                jnp.zeros((h,), F32), shard, 'na')
    return o.transpose(1, 0, 2).reshape(n, h * d), gathered


def _assemble(g, name):
    _, r, c = g.shape
    w = g.reshape(N_DEV * r, c) if BIG[name] == 1 else g.transpose(1, 0, 2).reshape(r, N_DEV * c)
    if name == 'w_in':
        w = jnp.pad(w, ((0, 0), (0, IN_COLS_PAD - IN_COLS)))
    return w


LAYER0_CARRIERS = {'mla': 'ffn_w_up', 'na': 'ffn_w_down', 'swa': 'w_out'}


def _forward(x, mod_x, mod_c, gathered0, shards, ws, ctx):
    n = x.shape[0]
    depth = mod_x.shape[0]
    ang_mla = axial_angles(n, MLA_ROPE)
    ang_swa = axial_angles(n, SWA_HEAD_DIM)
    gathered = dict(gathered0)
    for l in range(depth):
        update_ctx = l < depth - 1
        cur, gathered = gathered, {}
        nxt = {name: shards[name][l + 1] if update_ctx else None for name in BIG}
        carried = {kind: shards[name][0] if l == 0 else None for kind, name in LAYER0_CARRIERS.items()}

        def weight(name):
            return _assemble(cur[name], name)

        def proj(a, name):
            y, gathered[name] = pmm(a, weight(name), nxt[name], name)
            return y

        sh_m, sc_m, gt_m, sh_f, sc_f, gt_f = jnp.split(mod_x[l], 6)
        csh_m, csc_m, cgt_m, csh_f, csc_f, cgt_f = jnp.split(mod_c[l], 6)

        def adaln_vec(g, *mods):
            return jnp.stack((g,) + mods + (jnp.zeros_like(g),) * (7 - len(mods)))

        p_all, gathered['w_in'] = norm_proj(x, ctx, adaln_vec(ws['g_mix'][l], sh_m, sc_m, csh_m, csc_m),
                                            weight('w_in'), nxt['w_in'], 'w_in')
        p_all = p_all[:, :IN_COLS]
        mla_p, pool_p, swa_p, na_p = jnp.split(p_all, IN_SPLITS, axis=-1)
        T = p_all.shape[0]

        cq, ckv, kr = jnp.split(mla_p, [MLA_Q_LORA, MLA_Q_LORA + MLA_KV_LORA], axis=-1)
        q = proj(rms_norm(cq, ws['mla_q_a_norm'][l]), 'mla_w_qb').reshape(T, MLA_HEADS, MLA_NOPE + MLA_ROPE)
        kv = proj(rms_norm(ckv, ws['mla_kv_a_norm'][l]), 'mla_w_kvb').reshape(T, MLA_HEADS, MLA_NOPE + MLA_V)
        q_nope = rms_norm(q[..., :MLA_NOPE], ws['mla_q_nope_norm'][l])
        q_rope = rope_latent(rms_norm(q[..., MLA_NOPE:], ws['mla_q_rope_norm'][l]), n, ang_mla)
        k_nope = rms_norm(kv[..., :MLA_NOPE], ws['mla_k_nope_norm'][l])
        v_mla = kv[..., MLA_NOPE:]
        k_rope = rope_latent(rms_norm(kr, ws['mla_k_rope_norm'][l])[:, None, :], n, ang_mla)
        q_cat = jnp.concatenate([q_nope, q_rope], axis=-1).transpose(1, 0, 2)
        k_cat = jnp.concatenate([k_nope, jnp.broadcast_to(k_rope, (T, MLA_HEADS, MLA_ROPE))], axis=-1).transpose(1, 0, 2)
        out_a, moved = attention(q_cat[:, :n], k_cat, v_mla.transpose(1, 0, 2), carried['mla'], MLA_SCALE)
        if l == 0:
            cur[LAYER0_CARRIERS['mla']] = moved

        out_b = pool_mixer(pool_p[:n], ws['pool_w'][l], ws['pool_scale'][l])

        sq, sk, sv = jnp.split(swa_p, [SWA_HEADS * SWA_HEAD_DIM, (SWA_HEADS + SWA_KV_HEADS) * SWA_HEAD_DIM], axis=-1)
        sq = rope_latent(rms_norm(sq.reshape(T, SWA_HEADS, SWA_HEAD_DIM), ws['swa_q_norm'][l]), n, ang_swa)
        sk = rope_latent(rms_norm(sk.reshape(T, SWA_KV_HEADS, SWA_HEAD_DIM), ws['swa_k_norm'][l]), n, ang_swa)
        sv = sv.reshape(T, SWA_KV_HEADS, SWA_HEAD_DIM)
        out_c, moved = swa_latent(sq[:n], sk[:n], sv[:n], sk[n:], sv[n:], ws['swa_sink'][l], carried['swa'])
        if l == 0:
            cur[LAYER0_CARRIERS['swa']] = moved

        nq_, nk_, nv_ = jnp.split(na_p, 3, axis=-1)
        nq_ = rms_norm(nq_.reshape(T, NA_HEADS, NA_HEAD_DIM), ws['na_q_norm'][l])
        nk_ = rms_norm(nk_.reshape(T, NA_HEADS, NA_HEAD_DIM), ws['na_k_norm'][l])
        nv_ = nv_.reshape(T, NA_HEADS, NA_HEAD_DIM)
        out_d, moved = na_latent(nq_[:n], nk_[:n], nv_[:n], nk_[n:], nv_[n:], ws['na_rpb'][l], carried['na'])
        if l == 0:
            cur[LAYER0_CARRIERS['na']] = moved

        mix_x = jnp.concatenate([out_a, out_b, out_c, out_d], axis=-1)

        def ffn(x_rows, ctx_rows):
            y, gathered['ffn_w_up'], gathered['ffn_w_down'] = conv_ffn(
                x_rows, ctx_rows, adaln_vec(ws['g_ffn'][l], sh_f, sc_f, csh_f, csc_f),
                weight('ffn_w_up'), ws['ffn_conv_w'][l], ws['ffn_conv_b'][l], weight('ffn_w_down'),
                nxt['ffn_w_up'], nxt['ffn_w_down'], n)
            return y

        if update_ctx:
            L = T - n
            mix_c = jnp.concatenate([
                mla_attend_ctx(q_nope[n:], q_rope[n:], k_nope[n:], k_rope[n:, 0], v_mla[n:]).reshape(L, MLA_HEADS * MLA_V),
                pool_mixer(pool_p[n:], ws['pool_w'][l], ws['pool_scale'][l]),
                ctx_attention(sq[n:], sk[n:], sv[n:], ws['swa_sink'][l]),
                ctx_attention(nq_[n:], nk_[n:], nv_[n:], None),
            ], axis=-1)
            o_all = proj(jnp.concatenate([mix_x, mix_c], axis=0), 'w_out')
            x = x + gt_m * o_all[:n]
            ctx = ctx + cgt_m * o_all[n:]
            f_all = ffn(x, ctx)
            x = x + gt_f * f_all[:n]
            ctx = ctx + cgt_f * f_all[n:]
        else:
            x = x + gt_m * proj(mix_x, 'w_out')
            x = x + gt_f * ffn(x, None)
    return x


PACK_ROWS = 128


def _pack_rows(vecs):
    flat = jnp.concatenate([v.reshape(-1).astype(F32) for v in vecs])
    pad = (-flat.shape[0]) % (PACK_ROWS * 128)
    return jnp.pad(flat, (0, pad)).reshape(-1, 128)


def _unpack(flat, shapes):
    out, off = [], 0
    for s in shapes:
        size = 1
        for d in s:
            size *= d
        out.append(flat[off:off + size].reshape(s))
        off += size
    return out


def _silu_grad(z):
    s = jax.nn.sigmoid(z)
    return s * (1.0 + z * (1.0 - s))


def kernel(x, c, ctx, c_ctx, w_mod, b_mod, g_mix, g_ffn, w_in, w_out, mla_q_a_norm, mla_w_qb, mla_kv_a_norm, mla_w_kvb, mla_q_nope_norm, mla_q_rope_norm, mla_k_nope_norm, mla_k_rope_norm, pool_w, pool_scale, swa_q_norm, swa_k_norm, swa_sink, na_q_norm, na_k_norm, na_rpb, ffn_w_up, ffn_conv_w, ffn_conv_b, ffn_w_down, loss_target, m_c_ctx, m_w_mod, m_b_mod, m_g_mix, m_g_ffn, m_w_in, m_w_out, m_mla_q_a_norm, m_mla_w_qb, m_mla_kv_a_norm, m_mla_w_kvb, m_mla_q_nope_norm, m_mla_q_rope_norm, m_mla_k_nope_norm, m_mla_k_rope_norm, m_pool_w, m_pool_scale, m_swa_q_norm, m_swa_k_norm, m_swa_sink, m_na_q_norm, m_na_k_norm, m_na_rpb, m_ffn_w_up, m_ffn_conv_w, m_ffn_conv_b, m_ffn_w_down, v_c_ctx, v_w_mod, v_b_mod, v_g_mix, v_g_ffn, v_w_in, v_w_out, v_mla_q_a_norm, v_mla_w_qb, v_mla_kv_a_norm, v_mla_w_kvb, v_mla_q_nope_norm, v_mla_q_rope_norm, v_mla_k_nope_norm, v_mla_k_rope_norm, v_pool_w, v_pool_scale, v_swa_q_norm, v_swa_k_norm, v_swa_sink, v_na_q_norm, v_na_k_norm, v_na_rpb, v_ffn_w_up, v_ffn_conv_w, v_ffn_conv_b, v_ffn_w_down):
    return _step(x, c, ctx, c_ctx, w_mod, b_mod, g_mix, g_ffn, w_in, w_out, mla_q_a_norm, mla_w_qb, mla_kv_a_norm, mla_w_kvb, mla_q_nope_norm, mla_q_rope_norm, mla_k_nope_norm, mla_k_rope_norm, pool_w, pool_scale, swa_q_norm, swa_k_norm, swa_sink, na_q_norm, na_k_norm, na_rpb, ffn_w_up, ffn_conv_w, ffn_conv_b, ffn_w_down, loss_target, m_c_ctx, m_w_mod, m_b_mod, m_g_mix, m_g_ffn, m_w_in, m_w_out, m_mla_q_a_norm, m_mla_w_qb, m_mla_kv_a_norm, m_mla_w_kvb, m_mla_q_nope_norm, m_mla_q_rope_norm, m_mla_k_nope_norm, m_mla_k_rope_norm, m_pool_w, m_pool_scale, m_swa_q_norm, m_swa_k_norm, m_swa_sink, m_na_q_norm, m_na_k_norm, m_na_rpb, m_ffn_w_up, m_ffn_conv_w, m_ffn_conv_b, m_ffn_w_down, v_c_ctx, v_w_mod, v_b_mod, v_g_mix, v_g_ffn, v_w_in, v_w_out, v_mla_q_a_norm, v_mla_w_qb, v_mla_kv_a_norm, v_mla_w_kvb, v_mla_q_nope_norm, v_mla_q_rope_norm, v_mla_k_nope_norm, v_mla_k_rope_norm, v_pool_w, v_pool_scale, v_swa_q_norm, v_swa_k_norm, v_swa_sink, v_na_q_norm, v_na_k_norm, v_na_rpb, v_ffn_w_up, v_ffn_conv_w, v_ffn_conv_b, v_ffn_w_down)


def _step(*args):
    n_in = len(ARG_NAMES)
    n_w = len(WEIGHTS)
    given = dict(zip(ARG_NAMES, args[:n_in]))
    mom = dict(zip(WEIGHTS, args[n_in:n_in + n_w]))
    var = dict(zip(WEIGHTS, args[n_in + n_w:n_in + 2 * n_w]))
    me = _my_index()

    x = given['x'][0]
    ctx = given['ctx'][0]
    target = given['loss_target'][0]
    n, D = x.shape
    depth = given['w_mod'].shape[0]
    mod_cols = given['w_mod'].shape[2]
    conv_cols = given['ffn_conv_w'].shape[2]

    late0 = tuple(LAYER0_CARRIERS.values())
    gathered0 = {name: _all_gather(given[name][0].astype(BF16), 'ag_' + name) for name in BIG if name not in late0}
    misc = _all_gather(_pack_rows([given['c'], given['ffn_conv_w']]), 'ag_cond')
    misc = misc.reshape(N_DEV, -1)
    c_all = misc[:, :D]
    conv_w = misc[:, D:D + depth * 3 * conv_cols].reshape(N_DEV, depth, 3, conv_cols)
    conv_w = conv_w.transpose(1, 2, 0, 3).reshape(depth, 3, N_DEV * conv_cols)

    cond = jnp.concatenate([c_all, given['c_ctx'][None], jnp.zeros((16 - N_DEV - 1, D), F32)], axis=0)
    s16 = jax.nn.silu(cond).astype(BF16)
    wm16 = given['w_mod'].astype(BF16)
    b_loc = lax.dynamic_slice_in_dim(given['b_mod'], me * mod_cols, mod_cols, axis=1)
    mod_part = jnp.stack([_mm(s16, wm16[l], 'nn', F32, 'mod_fwd') + b_loc[l] for l in range(depth)])
    mod_all = _all_gather(mod_part, 'ag_mod').transpose(1, 2, 0, 3).reshape(depth, 16, N_DEV * mod_cols)
    mod_x = lax.dynamic_index_in_dim(mod_all, me, axis=1, keepdims=False)
    mod_c = mod_all[:, N_DEV]

    ws = {name: given[name] for name in SMALL if name not in ('c_ctx', 'b_mod')}
    ws['ffn_conv_w'] = conv_w
    shards = {name: tuple(given[name][l] if l or name in late0 else None for l in range(depth)) for name in BIG}
    y, vjp = jax.vjp(lambda *d: _forward(*d, ctx), x, mod_x, mod_c, gathered0, shards, ws)
    loss_local, dy = _loss_head(y, target)
    g_x, g_mod_x, g_mod_c, g_gathered0, g_shards, g_ws = vjp(dy)

    g_big = {}
    for name in BIG:
        g0 = g_shards[name][0]
        if name not in late0:
            g0 = _sum_parts(_all_to_all(g_gathered0[name], 'a2a_' + name), 'sum_' + name)
        g_big[name] = jnp.stack((g0,) + tuple(g_shards[name][1:]))

    g_mod = jnp.zeros((depth, 16, N_DEV * mod_cols), F32)
    g_mod = lax.dynamic_update_slice_in_dim(g_mod, g_mod_x[:, None, :], me, axis=1)
    g_mod = g_mod.at[:, N_DEV].set(g_mod_c)
    g_mod_parts = g_mod.reshape(depth, 16, N_DEV, mod_cols).transpose(2, 0, 1, 3).reshape(N_DEV, -1)
    g_conv_parts = g_ws['ffn_conv_w'].reshape(depth, 3, N_DEV, conv_cols).transpose(2, 0, 1, 3).reshape(N_DEV, -1)
    n_mod = depth * 16 * mod_cols
    n_conv = depth * 3 * conv_cols
    f32_parts = jnp.concatenate([g_mod_parts, g_conv_parts], axis=1)
    f32_pad = (-f32_parts.shape[1]) % (PACK_ROWS * 128)
    f32_parts = jnp.pad(f32_parts, ((0, 0), (0, f32_pad))).reshape(N_DEV, -1, 128)
    f32_parts = _all_to_all(f32_parts, 'a2a_f32')
    f32_sum = _sum_parts(f32_parts, 'sum_f32').reshape(-1)
    g_mod_loc = f32_sum[:n_mod].reshape(depth, 16, mod_cols)
    g_conv_loc = f32_sum[n_mod:n_mod + n_conv].reshape(depth * 3, conv_cols)

    g_mod16 = g_mod_loc.astype(BF16)
    g_w_mod = jnp.stack([_mm(s16, g_mod16[l], 'tn', F32, 'mod_dw') for l in range(depth)])
    d_silu = sum(_mm(g_mod16[l], wm16[l], 'nt', F32, 'mod_dc') for l in range(depth))
    g_c_ctx_part = d_silu[N_DEV] * _silu_grad(given['c_ctx'])

    small_grads = {name: g_ws[name] for name in SMALL if name not in ('c_ctx', 'b_mod')}
    small_grads['c_ctx'] = g_c_ctx_part
    small_grads['b_mod'] = g_mod_x + g_mod_c
    small_shapes = [given[name].shape for name in SMALL]
    n_small = sum(int(given[name].size) for name in SMALL)
    packed = _pack_rows([small_grads[name] for name in SMALL] + [loss_local.reshape(1)])
    small_parts = _all_gather(packed, 'ag_small')

    out_g, out_d, out_m, out_v = {}, {}, {}, {}

    def update(name, parts):
        shape = given[name].shape
        flat = (shape[0] * shape[1], shape[2])
        res = _adamw(parts.reshape((-1,) + flat), given[name].reshape(flat), mom[name].reshape(flat),
                     var[name].reshape(flat), 'adamw_' + name)
        out_g[name], out_d[name], out_m[name], out_v[name] = (r.reshape(shape) for r in res)

    for name in BIG:
        update(name, g_big[name])
    update('w_mod', g_w_mod)
    update('ffn_conv_w', g_conv_loc)

    zero1 = jnp.zeros((1,), F32)
    res = _adamw(small_parts, _pack_rows([given[k] for k in SMALL] + [zero1]), _pack_rows([mom[k] for k in SMALL] + [zero1]),
                 _pack_rows([var[k] for k in SMALL] + [zero1 + 1.0]), 'adamw_small')
    flats = [r.reshape(-1) for r in res]
    for name, g_, d_, m_, v_ in zip(SMALL, *[_unpack(f, small_shapes) for f in flats]):
        out_g[name], out_d[name], out_m[name], out_v[name] = g_, d_, m_, v_
    loss = flats[0][n_small]

    return (loss, g_x[None], *[out_g[k] for k in WEIGHTS], *[out_d[k] for k in WEIGHTS],
            *[out_m[k] for k in WEIGHTS], *[out_v[k] for k in WEIGHTS])
```
